```python
import math
import jax, jax.numpy as jnp
from jax import lax
import numpy as np

D_MODEL = 1024
BATCH = 8
SEQ = 8192
DEPTH = 4

N_MIXERS = 3
EPS = 1e-6
N_GLA = (DEPTH + 2) // 3
N_SGU = (DEPTH + 1) // 3
N_FOX = DEPTH // 3

GLA_HEADS = 4
GLA_DK = D_MODEL // (2 * GLA_HEADS)
GLA_DV = D_MODEL // GLA_HEADS
GLA_KD = GLA_HEADS * GLA_DK
GLA_VD = GLA_HEADS * GLA_DV
GLA_RANK = 16
GLA_NORMALIZER = 16.0
GLA_CHUNK = 64
GLA_IN = 2 * GLA_KD + 2 * GLA_VD + GLA_RANK

SGU_WIDTH = D_MODEL
SGU_GROUPS = 4
SGU_GDIM = SGU_WIDTH // SGU_GROUPS
SGU_CHUNK = 128
SGU_IN = 3 * SGU_WIDTH

FOX_HEADS = 16
FOX_DH = 64
FOX_AD = FOX_HEADS * FOX_DH
FOX_BLOCK = 128
FOX_IN = 4 * FOX_AD + FOX_HEADS
FORGET_BIAS_INIT = 2.0

kernel_name = "hybrid_gla_sgu_fox_trunk"


def rms_norm(x, g):
    xf = x.astype(jnp.float32)
    xf = xf * lax.rsqrt(jnp.mean(xf * xf, axis=-1, keepdims=True) + EPS)
    return xf.astype(x.dtype) * g


def layer_norm(x, g, b):
    xf = x.astype(jnp.float32)
    mu = jnp.mean(xf, axis=-1, keepdims=True)
    var = jnp.mean(jnp.square(xf - mu), axis=-1, keepdims=True)
    return ((xf - mu) * lax.rsqrt(var + EPS)).astype(x.dtype) * g + b


def gla_mixer(h, w_in, w_a2, b_a, g_head, w_out):
    B, S, _ = h.shape
    H, DK, DV, C = GLA_HEADS, GLA_DK, GLA_DV, GLA_CHUNK
    N = S // C
    q, k, v, z, a_low = jnp.split(h @ w_in, [GLA_KD, 2 * GLA_KD, 2 * GLA_KD + GLA_VD,
                                             2 * GLA_KD + 2 * GLA_VD], axis=-1)
    log_a = jax.nn.log_sigmoid((a_low @ w_a2 + b_a).astype(jnp.float32)) / GLA_NORMALIZER
    q = q.astype(jnp.float32) * (DK ** -0.5)

    def to_chunks(t, d):
        return t.astype(jnp.float32).reshape(B, N, C, H, d).transpose(1, 0, 3, 2, 4)

    qc, kc, vc, gc = to_chunks(q, DK), to_chunks(k, DK), to_chunks(v, DV), to_chunks(log_a, DK)
    causal = jnp.tril(jnp.ones((C, C), dtype=bool))

    def step(state, xs):
        qb, kb, vb, gb = xs
        b = jnp.cumsum(gb, axis=2)
        o_inter = jnp.einsum('bhik,bhkv->bhiv', qb * jnp.exp(b), state)
        diff = b[:, :, :, None, :] - b[:, :, None, :, :]
        decay = jnp.exp(jnp.where(causal[:, :, None], diff, -jnp.inf))
        attn = jnp.einsum('bhik,bhjk,bhijk->bhij', qb, kb, decay)
        o = o_inter + jnp.einsum('bhij,bhjv->bhiv', attn, vb)
        b_last = b[:, :, -1:, :]
        state = (jnp.exp(b_last[:, :, 0, :])[..., None] * state
                 + jnp.einsum('bhjk,bhjv->bhkv', kb * jnp.exp(b_last - b), vb))
        return state, o

    state0 = jnp.zeros((B, H, DK, DV), jnp.float32)
    _, o = lax.scan(step, state0, (qc, kc, vc, gc))
    o = o.transpose(1, 0, 3, 2, 4).reshape(B, S, H, DV)
    o = rms_norm(o, g_head).reshape(B, S, GLA_VD).astype(h.dtype)
    return (o * jax.nn.silu(z)) @ w_out


def sgu_mixer(h, w_in, ln_g, ln_b, w_s, b_s, w_out):
    B, S, _ = h.shape
    C, G, Dg = SGU_CHUNK, SGU_GROUPS, SGU_GDIM
    N = S // C
    u, v, z = jnp.split(h @ w_in, 3, axis=-1)
    u = jax.nn.gelu(u)
    v = layer_norm(jax.nn.gelu(v), ln_g, ln_b)
    w_causal = jnp.where(jnp.tril(jnp.ones((C, C), dtype=bool))[None], w_s, 0.0)
    vc = v.reshape(B, N, C, G, Dg)
    mixed = jnp.einsum('gts,bnsgd->bntgd', w_causal, vc) + b_s.T[None, None, :, :, None]
    sg = u * mixed.reshape(B, S, SGU_WIDTH)
    return (sg * jax.nn.silu(z)) @ w_out


def fox_mixer(h, w_in, b_f, g_q, g_k, w_out):
    B, S, _ = h.shape
    H, Dh, BLK = FOX_HEADS, FOX_DH, FOX_BLOCK
    q, k, v, z, f_logit = jnp.split(h @ w_in, [FOX_AD, 2 * FOX_AD, 3 * FOX_AD, 4 * FOX_AD], axis=-1)
    q = rms_norm(q.reshape(B, S, H, Dh), g_q).transpose(0, 2, 1, 3)
    k = rms_norm(k.reshape(B, S, H, Dh), g_k).transpose(0, 2, 1, 3)
    v = v.reshape(B, S, H, Dh).transpose(0, 2, 1, 3)
    log_f = jax.nn.log_sigmoid((f_logit + b_f).astype(jnp.float32))
    F = jnp.cumsum(log_f, axis=1).transpose(0, 2, 1)
    scale = Dh ** -0.5
    outs = []
    for i in range(S // BLK):
        lo, hi = i * BLK, (i + 1) * BLK
        s = jnp.einsum('bhqd,bhkd->bhqk', q[:, :, lo:hi], k[:, :, :hi]).astype(jnp.float32) * scale
        s = s + F[:, :, lo:hi, None] - F[:, :, None, :hi]
        causal = (lo + jnp.arange(BLK))[:, None] >= jnp.arange(hi)[None, :]
        p = jax.nn.softmax(jnp.where(causal, s, -jnp.inf), axis=-1).astype(v.dtype)
        outs.append(jnp.einsum('bhqk,bhkd->bhqd', p, v[:, :, :hi]))
    o = jnp.concatenate(outs, axis=2).transpose(0, 2, 1, 3).reshape(B, S, FOX_AD)
    return (o * jax.nn.silu(z)) @ w_out


def setup_inputs(seed: int = 0) -> dict:
    key = jax.random.key(seed)
    ks = jax.random.split(key, 32)
    D = D_MODEL
    nrm = lambda k, shape, s: jax.random.normal(k, shape, jnp.float32) * s
    return {
        "x": nrm(ks[0], (BATCH, SEQ, D), 1.0),
        "c": nrm(ks[1], (BATCH, D), 1.0),
        "norm_pre_g": 1.0 + nrm(ks[2], (DEPTH, D), 0.1),
        "norm_post_g": 1.0 + nrm(ks[3], (DEPTH, D), 0.1),
        "w_mod": nrm(ks[4], (DEPTH, D, 3 * D), D ** -0.5),
        "b_mod": nrm(ks[5], (DEPTH, 3 * D), 0.01),
        "gla_w_in": nrm(ks[6], (N_GLA, D, GLA_IN), D ** -0.5),
        "gla_w_a2": nrm(ks[7], (N_GLA, GLA_RANK, GLA_KD), GLA_RANK ** -0.5),
        "gla_b_a": nrm(ks[8], (N_GLA, GLA_KD), 0.1),
        "gla_g_head": 1.0 + nrm(ks[9], (N_GLA, GLA_HEADS, GLA_DV), 0.1),
        "gla_w_out": nrm(ks[10], (N_GLA, GLA_VD, D), GLA_VD ** -0.5),
        "sgu_w_in": nrm(ks[11], (N_SGU, D, SGU_IN), D ** -0.5),
        "sgu_ln_g": 1.0 + nrm(ks[12], (N_SGU, SGU_WIDTH), 0.1),
        "sgu_ln_b": nrm(ks[13], (N_SGU, SGU_WIDTH), 0.1),
        "sgu_w_s": nrm(ks[14], (N_SGU, SGU_GROUPS, SGU_CHUNK, SGU_CHUNK), SGU_CHUNK ** -0.5),
        "sgu_b_s": 1.0 + nrm(ks[15], (N_SGU, SGU_GROUPS, SGU_CHUNK), 0.1),
        "sgu_w_out": nrm(ks[16], (N_SGU, SGU_WIDTH, D), SGU_WIDTH ** -0.5),
        "fox_w_in": nrm(ks[17], (N_FOX, D, FOX_IN), D ** -0.5),
        "fox_b_f": FORGET_BIAS_INIT + nrm(ks[18], (N_FOX, FOX_HEADS), 0.1),
        "fox_g_q": 1.0 + nrm(ks[19], (N_FOX, FOX_DH), 0.1),
        "fox_g_k": 1.0 + nrm(ks[20], (N_FOX, FOX_DH), 0.1),
        "fox_w_out": nrm(ks[21], (N_FOX, FOX_AD, D), FOX_AD ** -0.5),
    }


def reference(x, c, norm_pre_g, norm_post_g, w_mod, b_mod,
              gla_w_in, gla_w_a2, gla_b_a, gla_g_head, gla_w_out,
              sgu_w_in, sgu_ln_g, sgu_ln_b, sgu_w_s, sgu_b_s, sgu_w_out,
              fox_w_in, fox_b_f, fox_g_q, fox_g_k, fox_w_out):
    cond = jax.nn.silu(c)
    for i in range(DEPTH):
        mod = (cond @ w_mod[i] + b_mod[i])[:, None, :]
        shift, scale, gate = jnp.split(mod, 3, axis=-1)
        h = rms_norm(x, norm_pre_g[i]) * (1.0 + scale) + shift
        kind, j = i % N_MIXERS, i // N_MIXERS
        if kind == 0:
            o = gla_mixer(h, gla_w_in[j], gla_w_a2[j], gla_b_a[j], gla_g_head[j], gla_w_out[j])
        elif kind == 1:
            o = sgu_mixer(h, sgu_w_in[j], sgu_ln_g[j], sgu_ln_b[j], sgu_w_s[j], sgu_b_s[j], sgu_w_out[j])
        else:
            o = fox_mixer(h, fox_w_in[j], fox_b_f[j], fox_g_q[j], fox_g_k[j], fox_w_out[j])
        x = x + gate * rms_norm(o, norm_post_g[i])
    return x
```

```python
import functools
import math

import numpy as np
import jax
import jax.numpy as jnp
from jax import lax
from jax.experimental import pallas as pl
from jax.experimental.pallas import tpu as pltpu

EPS = 1e-6
N_MIXERS = 3

LANES = 128
ROW_TILE = 512
COL_CHUNK = 512

GLA_HEADS = 4
GLA_RANK = 16
GLA_NORMALIZER = 16.0
GLA_CHUNK = 64
GLA_BLOCK = 512

SGU_GROUPS = 4
SGU_CHUNK = 128

FOX_HEADS = 16
FOX_DH = 64
FOX_TQ = 512
FOX_TK = 512
FOX_HEADS_PER_STEP = 2

BF16 = jnp.bfloat16
F32 = jnp.float32


def _dot(a, b):
    return jnp.dot(a, b, preferred_element_type=F32)


def _dot_nt(a, b):
    return lax.dot_general(a, b, (((1,), (1,)), ((), ())), preferred_element_type=F32)


def _dot_tn(a, b):
    return lax.dot_general(a, b, (((0,), (0,)), ((), ())), preferred_element_type=F32)


def _split2(x):
    hi = x.astype(BF16)
    lo = (x - hi.astype(F32)).astype(BF16)
    return hi, lo


def _split3(x):
    hi = x.astype(BF16)
    r = x - hi.astype(F32)
    mid = r.astype(BF16)
    lo = (r - mid.astype(F32)).astype(BF16)
    return hi, mid, lo


def _log_sigmoid(x):
    return jnp.minimum(x, 0.0) - jnp.log1p(jnp.exp(-jnp.abs(x)))


def _rms(x):
    return x * lax.rsqrt(jnp.mean(x * x, axis=-1, keepdims=True) + EPS)


def _mod_kernel(c_ref, w_ref, b_ref, o_ref):
    cond = jax.nn.silu(c_ref[...])
    o_ref[0] = _dot(cond, w_ref[0]) + b_ref[0]


def _modulation(c, w_mod, b_mod):
    depth, d, d3 = w_mod.shape
    bsz = c.shape[0]
    nblk = d3 // d
    return pl.pallas_call(
        _mod_kernel,
        grid=(depth, nblk),
        in_specs=[
            pl.BlockSpec((bsz, d), lambda i, j: (0, 0)),
            pl.BlockSpec((1, d, d), lambda i, j: (i, 0, j)),
            pl.BlockSpec((1, 1, d), lambda i, j: (i, 0, j)),
        ],
        out_specs=pl.BlockSpec((1, bsz, d), lambda i, j: (i, 0, j)),
        out_shape=jax.ShapeDtypeStruct((depth, bsz, d3), F32),
        name="adaln_mod",
    )(c, w_mod, b_mod.reshape(depth, 1, d3))


def _prenorm(x_ref, g_ref, shift_ref, scale_ref):
    h = _rms(x_ref[0]) * g_ref[...]
    h = h * (1.0 + scale_ref[0]) + shift_ref[0]
    return h.astype(BF16)


def _in_specs_common(tm, d):
    return [
        pl.BlockSpec((1, tm, d), lambda b, s: (b, s, 0)),
        pl.BlockSpec((1, d), lambda b, s: (0, 0)),
        pl.BlockSpec((1, 1, d), lambda b, s: (b, 0, 0)),
        pl.BlockSpec((1, 1, d), lambda b, s: (b, 0, 1)),
    ]


def _full(shape):
    nd = len(shape)
    return pl.BlockSpec(shape, lambda b, s: (0,) * nd)


def _row_spec(tm, width, col=0):
    return pl.BlockSpec((1, tm, width), lambda b, s: (b, s, col))


def _gla_in_kernel(x_ref, g_ref, shift_ref, scale_ref, w_ref, wa2_ref, ba_ref,
                   q_ref, k_ref, v_ref, z_ref, gk_ref, *, kd, vd, q_scale):
    h = _prenorm(x_ref, g_ref, shift_ref, scale_ref)
    col = 0
    for out_ref, width, mul in ((q_ref, kd, q_scale), (k_ref, kd, None),
                                (v_ref, vd, None), (z_ref, vd, None)):
        for c0 in range(0, width, COL_CHUNK):
            acc = _dot(h, w_ref[:, col + c0:col + c0 + COL_CHUNK])
            if mul is not None:
                acc = acc * mul
            out_ref[0, :, c0:c0 + COL_CHUNK] = acc.astype(out_ref.dtype)
        col += width
    a_low = _dot(h, w_ref[:, col:col + LANES])
    logit = _dot(a_low.astype(BF16), wa2_ref[...]) + ba_ref[...]
    gk_ref[0] = _log_sigmoid(logit) * (1.0 / GLA_NORMALIZER)


def _gla_in(x, g, mod, w_in, w_a2, b_a):
    bsz, seq, d = x.shape
    kd = w_a2.shape[1]
    vd = (w_in.shape[1] - 2 * kd - GLA_RANK) // 2
    n_main = 2 * kd + 2 * vd
    tm = min(ROW_TILE, seq)
    w = jnp.pad(w_in, ((0, 0), (0, LANES - GLA_RANK))).astype(BF16)
    wa2 = jnp.pad(w_a2, ((0, LANES - GLA_RANK), (0, 0))).astype(BF16)
    dk = kd // GLA_HEADS
    kern = functools.partial(_gla_in_kernel, kd=kd, vd=vd, q_scale=dk ** -0.5)
    return pl.pallas_call(
        kern,
        grid=(bsz, seq // tm),
        in_specs=_in_specs_common(tm, d) + [
            _full((d, n_main + LANES)), _full((LANES, kd)), _full((1, kd))],
        out_specs=[_row_spec(tm, kd), _row_spec(tm, kd), _row_spec(tm, vd),
                   _row_spec(tm, vd), _row_spec(tm, kd)],
        out_shape=[jax.ShapeDtypeStruct((bsz, seq, kd), BF16),
                   jax.ShapeDtypeStruct((bsz, seq, kd), BF16),
                   jax.ShapeDtypeStruct((bsz, seq, vd), BF16),
                   jax.ShapeDtypeStruct((bsz, seq, vd), BF16),
                   jax.ShapeDtypeStruct((bsz, seq, kd), F32)],
        name="gla_in_proj",
    )(x, g.reshape(1, d), mod, mod, w, wa2, b_a.reshape(1, kd))


def _gla_tables(c):
    tri = np.tril(np.ones((c, c), np.float32))
    rows = [tri, 1.0 - tri]
    n_levels = int(math.log2(c))
    level = np.full((c, c), -1, np.int32)
    idx = np.arange(c)
    for l in range(n_levels):
        blk = c >> l
        half = blk // 2
        mid = (idx // blk) * blk + half - 1
        rows.append(tri - tri[mid])
        same = (idx[:, None] // blk) == (idx[None, :] // blk)
        upper = (idx[:, None] % blk) >= half
        lower = (idx[None, :] % blk) < half
        level[same & upper & lower] = l
    level[idx, idx] = n_levels
    return np.concatenate(rows, axis=0), level, n_levels


def _gla_scan_kernel(q_ref, k_ref, v_ref, gk_ref, w_ref, lvl_ref, gh_ref, o_ref, st_ref,
                     *, chunk, n_levels, n_chunks):
    @pl.when(pl.program_id(2) == 0)
    def _():
        st_ref[...] = jnp.zeros_like(st_ref)

    w = w_ref[...]
    lvl = lvl_ref[...]
    gh = gh_ref[0]

    def body(ci, carry):
        r0 = pl.multiple_of(ci * chunk, chunk)
        rows = pl.ds(r0, chunk)
        g_hi, g_lo = _split2(gk_ref[0, rows, :])
        r = _dot(w, g_hi) + _dot(w, g_lo)
        b = r[0:chunk]
        rev = r[chunk:2 * chunk]
        qf = q_ref[0, rows, :].astype(F32)
        kf = k_ref[0, rows, :].astype(F32)
        v = v_ref[0, rows, :]
        st = st_ref[...]

        att = jnp.zeros((chunk, chunk), F32)
        for l in range(n_levels + 1):
            if l < n_levels:
                e = jnp.exp(-jnp.abs(r[(2 + l) * chunk:(3 + l) * chunk]))
                ql, kl = (qf * e).astype(BF16), (kf * e).astype(BF16)
            else:
                ql, kl = qf.astype(BF16), kf.astype(BF16)
            att = jnp.where(lvl == l, _dot_nt(ql, kl), att)

        o = _dot_nt((qf * jnp.exp(b)).astype(BF16), st.astype(BF16))
        o = o + _dot(att.astype(BF16), v)
        o_ref[0, rows, :] = (_rms(o) * gh).astype(o_ref.dtype)

        k_rev = (kf * jnp.exp(rev)).astype(BF16)
        st_ref[...] = st * jnp.exp(b[chunk - 1:chunk, :]) + _dot_tn(v, k_rev)
        return carry

    lax.fori_loop(0, n_chunks, body, 0)


def _gla_scan(q, k, v, gk, g_head):
    bsz, seq, kd = q.shape
    vd = v.shape[-1]
    dk, dv = kd // GLA_HEADS, vd // GLA_HEADS
    tb = min(GLA_BLOCK, seq)
    w_np, lvl_np, n_levels = _gla_tables(GLA_CHUNK)
    kern = functools.partial(_gla_scan_kernel, chunk=GLA_CHUNK, n_levels=n_levels,
                             n_chunks=tb // GLA_CHUNK)
    return pl.pallas_call(
        kern,
        grid=(bsz, GLA_HEADS, seq // tb),
        in_specs=[
            pl.BlockSpec((1, tb, dk), lambda b, h, s: (b, s, h)),
            pl.BlockSpec((1, tb, dk), lambda b, h, s: (b, s, h)),
            pl.BlockSpec((1, tb, dv), lambda b, h, s: (b, s, h)),
            pl.BlockSpec((1, tb, dk), lambda b, h, s: (b, s, h)),
            pl.BlockSpec(w_np.shape, lambda b, h, s: (0, 0)),
            pl.BlockSpec(lvl_np.shape, lambda b, h, s: (0, 0)),
            pl.BlockSpec((1, 1, dv), lambda b, h, s: (h, 0, 0)),
        ],
        out_specs=pl.BlockSpec((1, tb, dv), lambda b, h, s: (b, s, h)),
        out_shape=jax.ShapeDtypeStruct((bsz, seq, vd), BF16),
        scratch_shapes=[pltpu.VMEM((dv, dk), F32)],
        compiler_params=pltpu.CompilerParams(
            dimension_semantics=("arbitrary", "arbitrary", "arbitrary")),
        name="gla_scan",
    )(q, k, v, gk, jnp.asarray(w_np, BF16), jnp.asarray(lvl_np),
      g_head.reshape(GLA_HEADS, 1, dv))


def _sgu_in_kernel(x_ref, g_ref, shift_ref, scale_ref, w_ref, lng_ref, lnb_ref,
                   u_ref, v_ref, z_ref, vtmp_ref, *, width):
    h = _prenorm(x_ref, g_ref, shift_ref, scale_ref)
    for c0 in range(0, width, COL_CHUNK):
        cols = slice(c0, c0 + COL_CHUNK)
        u_ref[0, :, cols] = jax.nn.gelu(_dot(h, w_ref[:, cols])).astype(u_ref.dtype)
        vtmp_ref[:, cols] = jax.nn.gelu(_dot(h, w_ref[:, width + c0:width + c0 + COL_CHUNK]))
        z_ref[0, :, cols] = _dot(h, w_ref[:, 2 * width + c0:2 * width + c0 + COL_CHUNK]
                                 ).astype(z_ref.dtype)
    v = vtmp_ref[...]
    mu = jnp.mean(v, axis=-1, keepdims=True)
    vc = v - mu
    var = jnp.mean(vc * vc, axis=-1, keepdims=True)
    v_ref[0] = (vc * lax.rsqrt(var + EPS) * lng_ref[...] + lnb_ref[...]).astype(v_ref.dtype)


def _sgu_in(x, g, mod, w_in, ln_g, ln_b):
    bsz, seq, d = x.shape
    width = w_in.shape[1] // 3
    tm = min(ROW_TILE, seq)
    kern = functools.partial(_sgu_in_kernel, width=width)
    act = jax.ShapeDtypeStruct((bsz, seq, width), BF16)
    return pl.pallas_call(
        kern,
        grid=(bsz, seq // tm),
        in_specs=_in_specs_common(tm, d) + [
            _full((d, 3 * width)), _full((1, width)), _full((1, width))],
        out_specs=[_row_spec(tm, width)] * 3,
        out_shape=[act, act, act],
        scratch_shapes=[pltpu.VMEM((tm, width), F32)],
        name="sgu_in_proj",
    )(x, g.reshape(1, d), mod, mod, w_in.astype(BF16), ln_g.reshape(1, width),
      ln_b.reshape(1, width))


def _sgu_mix_kernel(u_ref, v_ref, ws_ref, bs_ref, o_ref, *, chunk, gdim, n_chunks):
    for g in range(SGU_GROUPS):
        cols = slice(g * gdim, (g + 1) * gdim)
        w = ws_ref[g]
        bias = bs_ref[g]
        bias = jnp.concatenate([bias] * (gdim // LANES), axis=1)
        for c in range(n_chunks):
            rows = slice(c * chunk, (c + 1) * chunk)
            mixed = _dot(w, v_ref[0, rows, cols]) + bias
            o_ref[0, rows, cols] = (u_ref[0, rows, cols].astype(F32) * mixed).astype(o_ref.dtype)


def _sgu_mix(u, v, w_s, b_s):
    bsz, seq, width = u.shape
    gdim = width // SGU_GROUPS
    tm = min(ROW_TILE, seq)
    chunk = SGU_CHUNK
    causal = jnp.tril(jnp.ones((chunk, chunk), bool))
    w_causal = jnp.where(causal[None], w_s, 0.0).astype(BF16)
    bias = jnp.broadcast_to(b_s[:, :, None], (SGU_GROUPS, chunk, LANES))
    kern = functools.partial(_sgu_mix_kernel, chunk=chunk, gdim=gdim, n_chunks=tm // chunk)
    return pl.pallas_call(
        kern,
        grid=(bsz, seq // tm),
        in_specs=[_row_spec(tm, width), _row_spec(tm, width),
                  _full((SGU_GROUPS, chunk, chunk)), _full((SGU_GROUPS, chunk, LANES))],
        out_specs=_row_spec(tm, width),
        out_shape=jax.ShapeDtypeStruct((bsz, seq, width), BF16),
        name="sgu_mix",
    )(u, v, w_causal, bias)


def _fox_in_kernel(x_ref, g_ref, shift_ref, scale_ref, w_ref, gq_ref, gk_ref, bf_ref,
                   q_ref, k_ref, v_ref, z_ref, lf_ref, *, ad, dh, q_scale):
    h = _prenorm(x_ref, g_ref, shift_ref, scale_ref)
    for out_ref, base, gain in ((q_ref, 0, gq_ref[...] * q_scale), (k_ref, ad, gk_ref[...])):
        for c0 in range(0, ad, COL_CHUNK):
            acc = _dot(h, w_ref[:, base + c0:base + c0 + COL_CHUNK])
            for h0 in range(0, COL_CHUNK, dh):
                out_ref[0, :, c0 + h0:c0 + h0 + dh] = (
                    _rms(acc[:, h0:h0 + dh]) * gain).astype(out_ref.dtype)
    for out_ref, base in ((v_ref, 2 * ad), (z_ref, 3 * ad)):
        for c0 in range(0, ad, COL_CHUNK):
            out_ref[0, :, c0:c0 + COL_CHUNK] = _dot(
                h, w_ref[:, base + c0:base + c0 + COL_CHUNK]).astype(out_ref.dtype)
    f_logit = _dot(h, w_ref[:, 4 * ad:4 * ad + LANES])
    lf_ref[0] = _log_sigmoid(f_logit + bf_ref[...])


def _fox_in(x, g, mod, w_in, b_f, g_q, g_k):
    bsz, seq, d = x.shape
    ad = FOX_HEADS * FOX_DH
    tm = min(ROW_TILE, seq)
    w = jnp.pad(w_in, ((0, 0), (0, LANES - FOX_HEADS))).astype(BF16)
    bf = jnp.pad(b_f, (0, LANES - FOX_HEADS)).reshape(1, LANES)
    kern = functools.partial(_fox_in_kernel, ad=ad, dh=FOX_DH, q_scale=FOX_DH ** -0.5)
    act = jax.ShapeDtypeStruct((bsz, seq, ad), BF16)
    return pl.pallas_call(
        kern,
        grid=(bsz, seq // tm),
        in_specs=_in_specs_common(tm, d) + [
            _full((d, 4 * ad + LANES)), _full((1, FOX_DH)), _full((1, FOX_DH)),
            _full((1, LANES))],
        out_specs=[_row_spec(tm, ad)] * 4 + [_row_spec(tm, LANES)],
        out_shape=[act, act, act, act, jax.ShapeDtypeStruct((bsz, seq, LANES), F32)],
        name="fox_in_proj",
    )(x, g.reshape(1, d), mod, mod, w, g_q.reshape(1, FOX_DH), g_k.reshape(1, FOX_DH), bf)


def _cumsum_kernel(x_ref, tri_ref, hi_ref, mid_ref, lo_ref, carry_ref):
    @pl.when(pl.program_id(1) == 0)
    def _():
        carry_ref[...] = jnp.zeros_like(carry_ref)

    tri = tri_ref[...]
    hi, mid, lo = _split3(x_ref[0])
    run = _dot(tri, hi) + _dot(tri, mid) + _dot(tri, lo) + carry_ref[...]
    carry_ref[...] = run[-1:, :]
    hi_ref[0], mid_ref[0], lo_ref[0] = _split3(run)


def _seq_cumsum_parts(x):
    bsz, seq, width = x.shape
    tm = min(ROW_TILE, seq)
    tri = jnp.asarray(np.tril(np.ones((tm, tm), np.float32)), BF16)
    part = jax.ShapeDtypeStruct((bsz, seq, width), BF16)
    return pl.pallas_call(
        _cumsum_kernel,
        grid=(bsz, seq // tm),
        in_specs=[_row_spec(tm, width), _full((tm, tm))],
        out_specs=[_row_spec(tm, width)] * 3,
        out_shape=[part, part, part],
        scratch_shapes=[pltpu.VMEM((1, width), F32)],
        compiler_params=pltpu.CompilerParams(dimension_semantics=("arbitrary", "arbitrary")),
        name="forget_cumsum",
    )(x, tri)


def _fox_attn_kernel(q_ref, k_ref, v_ref, o_ref, *, tq, tk, dh):
    qi = pl.program_id(2)
    row = lax.broadcasted_iota(jnp.int32, (tq, tk), 0)
    colm = lax.broadcasted_iota(jnp.int32, (tq, tk), 1)
    outs = []
    for hh in range(FOX_HEADS_PER_STEP):
        q = q_ref[0, hh]

        def step(kv, carry, masked):
            m, acc = carry
            k0 = pl.multiple_of(kv * tk, tk)
            s = _dot_nt(q, k_ref[0, hh, pl.ds(k0, tk), :])
            if masked:
                s = jnp.where(row >= colm, s, -jnp.inf)
            m_new = jnp.maximum(m, jnp.max(s, axis=-1, keepdims=True))
            p = jnp.exp(s - m_new)
            acc = acc * jnp.exp(m - m_new) + _dot(p.astype(BF16), v_ref[0, hh, pl.ds(k0, tk), :])
            return m_new, acc

        init = (jnp.full((tq, 1), -jnp.inf, F32), jnp.zeros((tq, LANES), F32))
        n_full = qi * (tq // tk)
        carry = lax.fori_loop(0, n_full, functools.partial(step, masked=False), init)
        m, acc = step(n_full, carry, masked=True)
        outs.append(acc[:, :dh] / acc[:, dh:dh + 1])
    o_ref[0] = jnp.concatenate(outs, axis=1).astype(o_ref.dtype)


def _fox_attn(q, k, v, f_cum_parts):
    bsz, seq, ad = q.shape
    nh, dh = FOX_HEADS, FOX_DH
    tq, tk = min(FOX_TQ, seq), min(FOX_TK, seq)
    assert tq == tk
    f_parts = jnp.stack(f_cum_parts, axis=-1)
    ones3 = jnp.ones((bsz, seq, nh, 3), BF16)
    pad = jnp.zeros((bsz, seq, nh, LANES - dh - 6), BF16)
    heads = lambda t: t.reshape(bsz, seq, nh, dh)
    qa = jnp.concatenate([heads(q), f_parts, ones3, pad], axis=-1).transpose(0, 2, 1, 3)
    ka = jnp.concatenate([heads(k), ones3, -f_parts, pad], axis=-1).transpose(0, 2, 1, 3)
    va = jnp.concatenate([heads(v), ones3[..., :1],
                          jnp.zeros((bsz, seq, nh, LANES - dh - 1), BF16)],
                         axis=-1).transpose(0, 2, 1, 3)
    hp = FOX_HEADS_PER_STEP
    kern = functools.partial(_fox_attn_kernel, tq=tq, tk=tk, dh=dh)
    return pl.pallas_call(
        kern,
        grid=(bsz, nh // hp, seq // tq),
        in_specs=[
            pl.BlockSpec((1, hp, tq, LANES), lambda b, h, s: (b, h, s, 0)),
            pl.BlockSpec((1, hp, seq, LANES), lambda b, h, s: (b, h, 0, 0)),
            pl.BlockSpec((1, hp, seq, LANES), lambda b, h, s: (b, h, 0, 0)),
        ],
        out_specs=pl.BlockSpec((1, tq, hp * dh), lambda b, h, s: (b, s, h)),
        out_shape=jax.ShapeDtypeStruct((bsz, seq, ad), BF16),
        name="fox_attn",
    )(qa, ka, va)


def _out_kernel(o_ref, z_ref, w_ref, g_ref, gate_ref, x_ref, y_ref):
    gated = o_ref[0].astype(F32) * jax.nn.silu(z_ref[0].astype(F32))
    y = _dot(gated.astype(BF16), w_ref[...])
    y_ref[0] = x_ref[0] + gate_ref[0] * (_rms(y) * g_ref[...])


def _out_proj(o, z, w_out, g, mod, x):
    bsz, seq, d = x.shape
    width = o.shape[-1]
    tm = min(ROW_TILE, seq)
    return pl.pallas_call(
        _out_kernel,
        grid=(bsz, seq // tm),
        in_specs=[_row_spec(tm, width), _row_spec(tm, width), _full((width, d)), _full((1, d)),
                  pl.BlockSpec((1, 1, d), lambda b, s: (b, 0, 2)), _row_spec(tm, d)],
        out_specs=_row_spec(tm, d),
        out_shape=jax.ShapeDtypeStruct((bsz, seq, d), F32),
        name="out_proj",
    )(o, z, w_out.astype(BF16), g.reshape(1, d), mod, x)


def _gla_layer(x, g_pre, mod, w_in, w_a2, b_a, g_head):
    q, k, v, z, gk = _gla_in(x, g_pre, mod, w_in, w_a2, b_a)
    return _gla_scan(q, k, v, gk, g_head), z


def _sgu_layer(x, g_pre, mod, w_in, ln_g, ln_b, w_s, b_s):
    u, v, z = _sgu_in(x, g_pre, mod, w_in, ln_g, ln_b)
    return _sgu_mix(u, v, w_s, b_s), z


def _fox_layer(x, g_pre, mod, w_in, b_f, g_q, g_k):
    q, k, v, z, lf = _fox_in(x, g_pre, mod, w_in, b_f, g_q, g_k)
    f_parts = [p[..., :FOX_HEADS] for p in _seq_cumsum_parts(lf)]
    return _fox_attn(q, k, v, f_parts), z


def kernel(x, c, norm_pre_g, norm_post_g, w_mod, b_mod, gla_w_in, gla_w_a2, gla_b_a, gla_g_head, gla_w_out, sgu_w_in, sgu_ln_g, sgu_ln_b, sgu_w_s, sgu_b_s, sgu_w_out, fox_w_in, fox_b_f, fox_g_q, fox_g_k, fox_w_out):
    depth = w_mod.shape[0]
    bsz = x.shape[0]
    mod_all = _modulation(c, w_mod, b_mod)
    for i in range(depth):
        mod = mod_all[i].reshape(bsz, 1, -1)
        kind, j = i % N_MIXERS, i // N_MIXERS
        if kind == 0:
            o, z = _gla_layer(x, norm_pre_g[i], mod, gla_w_in[j], gla_w_a2[j], gla_b_a[j],
                              gla_g_head[j])
            w_out = gla_w_out[j]
        elif kind == 1:
            o, z = _sgu_layer(x, norm_pre_g[i], mod, sgu_w_in[j], sgu_ln_g[j], sgu_ln_b[j],
                              sgu_w_s[j], sgu_b_s[j])
            w_out = sgu_w_out[j]
        else:
            o, z = _fox_layer(x, norm_pre_g[i], mod, fox_w_in[j], fox_b_f[j], fox_g_q[j],
                              fox_g_k[j])
            w_out = fox_w_out[j]
        x = _out_proj(o, z, w_out, norm_post_g[i], mod, x)
    return x
```

```python
import functools
import math

import numpy as np
import jax
import jax.numpy as jnp
from jax import lax
from jax.experimental import pallas as pl
from jax.experimental.pallas import tpu as pltpu

EPS = 1e-6
N_MIXERS = 3

LANES = 128
ROW_TILE = 512
COL_CHUNK = 512

GLA_HEADS = 4
GLA_RANK = 16
GLA_NORMALIZER = 16.0
GLA_CHUNK = 64
GLA_BLOCK = 512

SGU_GROUPS = 4
SGU_CHUNK = 128

FOX_HEADS = 16
FOX_DH = 64
FOX_TQ = 512
FOX_TK = 512
FOX_HEADS_PER_STEP = 2
FOX_PARTS = 3
LOG2E = math.log2(math.e)

BF16 = jnp.bfloat16
F32 = jnp.float32


def _dot(a, b):
    return jnp.dot(a, b, preferred_element_type=F32)


def _dot_nt(a, b):
    return lax.dot_general(a, b, (((1,), (1,)), ((), ())), preferred_element_type=F32)


def _dot_tn(a, b):
    return lax.dot_general(a, b, (((0,), (0,)), ((), ())), preferred_element_type=F32)


def _split2(x):
    hi = x.astype(BF16)
    lo = (x - hi.astype(F32)).astype(BF16)
    return hi, lo


def _split3(x):
    hi = x.astype(BF16)
    r = x - hi.astype(F32)
    mid = r.astype(BF16)
    lo = (r - mid.astype(F32)).astype(BF16)
    return hi, mid, lo


def _log_sigmoid(x):
    return jnp.minimum(x, 0.0) - jnp.log1p(jnp.exp(-jnp.abs(x)))


def _rms(x):
    return x * lax.rsqrt(jnp.mean(x * x, axis=-1, keepdims=True) + EPS)


def _mod_kernel(c_ref, w_ref, b_ref, o_ref):
    cond = jax.nn.silu(c_ref[...])
    o_ref[0] = _dot(cond, w_ref[0]) + b_ref[0]


def _modulation(c, w_mod, b_mod):
    depth, d, d3 = w_mod.shape
    bsz = c.shape[0]
    nblk = d3 // d
    return pl.pallas_call(
        _mod_kernel,
        grid=(depth, nblk),
        in_specs=[
            pl.BlockSpec((bsz, d), lambda i, j: (0, 0)),
            pl.BlockSpec((1, d, d), lambda i, j: (i, 0, j)),
            pl.BlockSpec((1, 1, d), lambda i, j: (i, 0, j)),
        ],
        out_specs=pl.BlockSpec((1, bsz, d), lambda i, j: (i, 0, j)),
        out_shape=jax.ShapeDtypeStruct((depth, bsz, d3), F32),
        name="adaln_mod",
    )(c, w_mod, b_mod.reshape(depth, 1, d3))


def _prenorm(x_ref, g_ref, shift_ref, scale_ref):
    h = _rms(x_ref[0]) * g_ref[...]
    h = h * (1.0 + scale_ref[0]) + shift_ref[0]
    return h.astype(BF16)


def _in_specs_common(tm, d):
    return [
        pl.BlockSpec((1, tm, d), lambda b, s: (b, s, 0)),
        pl.BlockSpec((1, d), lambda b, s: (0, 0)),
        pl.BlockSpec((1, 1, d), lambda b, s: (b, 0, 0)),
        pl.BlockSpec((1, 1, d), lambda b, s: (b, 0, 1)),
    ]


def _full(shape):
    nd = len(shape)
    return pl.BlockSpec(shape, lambda b, s: (0,) * nd)


def _row_spec(tm, width, col=0):
    return pl.BlockSpec((1, tm, width), lambda b, s: (b, s, col))


def _gla_in_kernel(x_ref, g_ref, shift_ref, scale_ref, w_ref, wa2_ref, ba_ref,
                   q_ref, k_ref, v_ref, z_ref, gk_ref, *, kd, vd, q_scale):
    h = _prenorm(x_ref, g_ref, shift_ref, scale_ref)
    col = 0
    for out_ref, width, mul in ((q_ref, kd, q_scale), (k_ref, kd, None),
                                (v_ref, vd, None), (z_ref, vd, None)):
        for c0 in range(0, width, COL_CHUNK):
            acc = _dot(h, w_ref[:, col + c0:col + c0 + COL_CHUNK])
            if mul is not None:
                acc = acc * mul
            out_ref[0, :, c0:c0 + COL_CHUNK] = acc.astype(out_ref.dtype)
        col += width
    a_low = _dot(h, w_ref[:, col:col + LANES])
    logit = _dot(a_low.astype(BF16), wa2_ref[...]) + ba_ref[...]
    gk_ref[0] = _log_sigmoid(logit) * (1.0 / GLA_NORMALIZER)


def _gla_in(x, g, mod, w_in, w_a2, b_a):
    bsz, seq, d = x.shape
    kd = w_a2.shape[1]
    vd = (w_in.shape[1] - 2 * kd - GLA_RANK) // 2
    n_main = 2 * kd + 2 * vd
    tm = min(ROW_TILE, seq)
    w = jnp.pad(w_in, ((0, 0), (0, LANES - GLA_RANK))).astype(BF16)
    wa2 = jnp.pad(w_a2, ((0, LANES - GLA_RANK), (0, 0))).astype(BF16)
    dk = kd // GLA_HEADS
    kern = functools.partial(_gla_in_kernel, kd=kd, vd=vd, q_scale=dk ** -0.5)
    return pl.pallas_call(
        kern,
        grid=(bsz, seq // tm),
        in_specs=_in_specs_common(tm, d) + [
            _full((d, n_main + LANES)), _full((LANES, kd)), _full((1, kd))],
        out_specs=[_row_spec(tm, kd), _row_spec(tm, kd), _row_spec(tm, vd),
                   _row_spec(tm, vd), _row_spec(tm, kd)],
        out_shape=[jax.ShapeDtypeStruct((bsz, seq, kd), BF16),
                   jax.ShapeDtypeStruct((bsz, seq, kd), BF16),
                   jax.ShapeDtypeStruct((bsz, seq, vd), BF16),
                   jax.ShapeDtypeStruct((bsz, seq, vd), BF16),
                   jax.ShapeDtypeStruct((bsz, seq, kd), F32)],
        name="gla_in_proj",
    )(x, g.reshape(1, d), mod, mod, w, wa2, b_a.reshape(1, kd))


def _gla_tables(c):
    tri = np.tril(np.ones((c, c), np.float32))
    rows = [tri, 1.0 - tri]
    n_levels = int(math.log2(c))
    level = np.full((c, c), -1, np.int32)
    idx = np.arange(c)
    for l in range(n_levels):
        blk = c >> l
        half = blk // 2
        mid = (idx // blk) * blk + half - 1
        rows.append(tri - tri[mid])
        same = (idx[:, None] // blk) == (idx[None, :] // blk)
        upper = (idx[:, None] % blk) >= half
        lower = (idx[None, :] % blk) < half
        level[same & upper & lower] = l
    level[idx, idx] = n_levels
    return np.concatenate(rows, axis=0), level, n_levels


def _gla_scan_kernel(q_ref, k_ref, v_ref, gk_ref, w_ref, lvl_ref, gh_ref, o_ref, st_ref,
                     *, chunk, n_levels, n_chunks):
    @pl.when(pl.program_id(2) == 0)
    def _():
        st_ref[...] = jnp.zeros_like(st_ref)

    w = w_ref[...]
    lvl = lvl_ref[...]
    gh = gh_ref[0]

    def body(ci, carry):
        r0 = pl.multiple_of(ci * chunk, chunk)
        rows = pl.ds(r0, chunk)
        g_hi, g_lo = _split2(gk_ref[0, rows, :])
        r = _dot(w, g_hi) + _dot(w, g_lo)
        b = r[0:chunk]
        rev = r[chunk:2 * chunk]
        qf = q_ref[0, rows, :].astype(F32)
        kf = k_ref[0, rows, :].astype(F32)
        v = v_ref[0, rows, :]
        st = st_ref[...]

        att = jnp.zeros((chunk, chunk), F32)
        for l in range(n_levels + 1):
            if l < n_levels:
                e = jnp.exp(-jnp.abs(r[(2 + l) * chunk:(3 + l) * chunk]))
                ql, kl = (qf * e).astype(BF16), (kf * e).astype(BF16)
            else:
                ql, kl = qf.astype(BF16), kf.astype(BF16)
            att = jnp.where(lvl == l, _dot_nt(ql, kl), att)

        o = _dot_nt((qf * jnp.exp(b)).astype(BF16), st.astype(BF16))
        o = o + _dot(att.astype(BF16), v)
        o_ref[0, rows, :] = (_rms(o) * gh).astype(o_ref.dtype)

        k_rev = (kf * jnp.exp(rev)).astype(BF16)
        st_ref[...] = st * jnp.exp(b[chunk - 1:chunk, :]) + _dot_tn(v, k_rev)
        return carry

    lax.fori_loop(0, n_chunks, body, 0)


def _gla_scan(q, k, v, gk, g_head):
    bsz, seq, kd = q.shape
    vd = v.shape[-1]
    dk, dv = kd // GLA_HEADS, vd // GLA_HEADS
    tb = min(GLA_BLOCK, seq)
    w_np, lvl_np, n_levels = _gla_tables(GLA_CHUNK)
    kern = functools.partial(_gla_scan_kernel, chunk=GLA_CHUNK, n_levels=n_levels,
                             n_chunks=tb // GLA_CHUNK)
    return pl.pallas_call(
        kern,
        grid=(bsz, GLA_HEADS, seq // tb),
        in_specs=[
            pl.BlockSpec((1, tb, dk), lambda b, h, s: (b, s, h)),
            pl.BlockSpec((1, tb, dk), lambda b, h, s: (b, s, h)),
            pl.BlockSpec((1, tb, dv), lambda b, h, s: (b, s, h)),
            pl.BlockSpec((1, tb, dk), lambda b, h, s: (b, s, h)),
            pl.BlockSpec(w_np.shape, lambda b, h, s: (0, 0)),
            pl.BlockSpec(lvl_np.shape, lambda b, h, s: (0, 0)),
            pl.BlockSpec((1, 1, dv), lambda b, h, s: (h, 0, 0)),
        ],
        out_specs=pl.BlockSpec((1, tb, dv), lambda b, h, s: (b, s, h)),
        out_shape=jax.ShapeDtypeStruct((bsz, seq, vd), BF16),
        scratch_shapes=[pltpu.VMEM((dv, dk), F32)],
        compiler_params=pltpu.CompilerParams(
            dimension_semantics=("arbitrary", "arbitrary", "arbitrary")),
        name="gla_scan",
    )(q, k, v, gk, jnp.asarray(w_np, BF16), jnp.asarray(lvl_np),
      g_head.reshape(GLA_HEADS, 1, dv))


def _sgu_in_kernel(x_ref, g_ref, shift_ref, scale_ref, w_ref, lng_ref, lnb_ref,
                   u_ref, v_ref, z_ref, vtmp_ref, *, width):
    h = _prenorm(x_ref, g_ref, shift_ref, scale_ref)
    for c0 in range(0, width, COL_CHUNK):
        cols = slice(c0, c0 + COL_CHUNK)
        u_ref[0, :, cols] = jax.nn.gelu(_dot(h, w_ref[:, cols])).astype(u_ref.dtype)
        vtmp_ref[:, cols] = jax.nn.gelu(_dot(h, w_ref[:, width + c0:width + c0 + COL_CHUNK]))
        z_ref[0, :, cols] = _dot(h, w_ref[:, 2 * width + c0:2 * width + c0 + COL_CHUNK]
                                 ).astype(z_ref.dtype)
    v = vtmp_ref[...]
    mu = jnp.mean(v, axis=-1, keepdims=True)
    vc = v - mu
    var = jnp.mean(vc * vc, axis=-1, keepdims=True)
    v_ref[0] = (vc * lax.rsqrt(var + EPS) * lng_ref[...] + lnb_ref[...]).astype(v_ref.dtype)


def _sgu_in(x, g, mod, w_in, ln_g, ln_b):
    bsz, seq, d = x.shape
    width = w_in.shape[1] // 3
    tm = min(ROW_TILE, seq)
    kern = functools.partial(_sgu_in_kernel, width=width)
    act = jax.ShapeDtypeStruct((bsz, seq, width), BF16)
    return pl.pallas_call(
        kern,
        grid=(bsz, seq // tm),
        in_specs=_in_specs_common(tm, d) + [
            _full((d, 3 * width)), _full((1, width)), _full((1, width))],
        out_specs=[_row_spec(tm, width)] * 3,
        out_shape=[act, act, act],
        scratch_shapes=[pltpu.VMEM((tm, width), F32)],
        name="sgu_in_proj",
    )(x, g.reshape(1, d), mod, mod, w_in.astype(BF16), ln_g.reshape(1, width),
      ln_b.reshape(1, width))


def _sgu_mix_kernel(u_ref, v_ref, ws_ref, bs_ref, o_ref, *, chunk, gdim, n_chunks):
    for g in range(SGU_GROUPS):
        cols = slice(g * gdim, (g + 1) * gdim)
        w = ws_ref[g]
        bias = bs_ref[g]
        bias = jnp.concatenate([bias] * (gdim // LANES), axis=1)
        for c in range(n_chunks):
            rows = slice(c * chunk, (c + 1) * chunk)
            mixed = _dot(w, v_ref[0, rows, cols]) + bias
            o_ref[0, rows, cols] = (u_ref[0, rows, cols].astype(F32) * mixed).astype(o_ref.dtype)


def _sgu_mix(u, v, w_s, b_s):
    bsz, seq, width = u.shape
    gdim = width // SGU_GROUPS
    tm = min(ROW_TILE, seq)
    chunk = SGU_CHUNK
    causal = jnp.tril(jnp.ones((chunk, chunk), bool))
    w_causal = jnp.where(causal[None], w_s, 0.0).astype(BF16)
    bias = jnp.broadcast_to(b_s[:, :, None], (SGU_GROUPS, chunk, LANES))
    kern = functools.partial(_sgu_mix_kernel, chunk=chunk, gdim=gdim, n_chunks=tm // chunk)
    return pl.pallas_call(
        kern,
        grid=(bsz, seq // tm),
        in_specs=[_row_spec(tm, width), _row_spec(tm, width),
                  _full((SGU_GROUPS, chunk, chunk)), _full((SGU_GROUPS, chunk, LANES))],
        out_specs=_row_spec(tm, width),
        out_shape=jax.ShapeDtypeStruct((bsz, seq, width), BF16),
        name="sgu_mix",
    )(u, v, w_causal, bias)


def _fox_tables(n_heads, dh):
    width = n_heads * dh
    pq = np.zeros((FOX_PARTS * LANES, width), np.float32)
    pk = np.zeros((FOX_PARTS * LANES, width), np.float32)
    cq = np.zeros((1, width), np.float32)
    ck = np.zeros((1, width), np.float32)
    for h in range(n_heads):
        base = (h // 2) * 2 * dh + (dh if h % 2 == 0 else 0)
        for p in range(FOX_PARTS):
            pq[p * LANES + h, base + p] = 1.0
            cq[0, base + FOX_PARTS + p] = 1.0
            ck[0, base + p] = 1.0
            pk[p * LANES + h, base + FOX_PARTS + p] = -1.0
    return pq, pk, cq, ck


def _fox_in_kernel(x_ref, g_ref, shift_ref, scale_ref, w_ref, gq_ref, gk_ref, bf_ref, tri_ref,
                   pq_ref, pk_ref, cq_ref, ck_ref,
                   qt_ref, ka_ref, vt_ref, z_ref, carry_ref, *, ad, dh):
    @pl.when(pl.program_id(1) == 0)
    def _():
        carry_ref[...] = jnp.zeros_like(carry_ref)

    h = _prenorm(x_ref, g_ref, shift_ref, scale_ref)
    tm = h.shape[0]
    pair = 2 * dh
    lane = lax.broadcasted_iota(jnp.int32, (tm, pair), 1)
    low = lane < dh

    f_logit = _dot(h, w_ref[:, 4 * ad:4 * ad + LANES])
    tri = tri_ref[...]
    lf_hi, lf_mid, lf_lo = _split3(_log_sigmoid(f_logit + bf_ref[...]))
    f_cum = _dot(tri, lf_hi) + _dot(tri, lf_mid) + _dot(tri, lf_lo) + carry_ref[...]
    carry_ref[...] = f_cum[tm - 1:tm, :]
    f3 = jnp.concatenate(_split3(f_cum * LOG2E), axis=1)
    aug_q = _dot(f3, pq_ref[...]) + cq_ref[...]
    aug_k = _dot(f3, pk_ref[...]) + ck_ref[...]

    def half_rms(x):
        sq = x * x
        s_lo = jnp.sum(jnp.where(low, sq, 0.0), axis=-1, keepdims=True)
        s_hi = jnp.sum(jnp.where(low, 0.0, sq), axis=-1, keepdims=True)
        return x * lax.rsqrt(jnp.where(low, s_lo, s_hi) * (1.0 / dh) + EPS)

    ones_col = jnp.where((lane == 0) | (lane == dh), 1.0, 0.0)
    for c0 in range(0, ad, COL_CHUNK):
        acc_q = _dot(h, w_ref[:, c0:c0 + COL_CHUNK])
        acc_k = _dot(h, w_ref[:, ad + c0:ad + c0 + COL_CHUNK])
        acc_v = _dot(h, w_ref[:, 2 * ad + c0:2 * ad + c0 + COL_CHUNK])
        z_ref[0, :, c0:c0 + COL_CHUNK] = _dot(
            h, w_ref[:, 3 * ad + c0:3 * ad + c0 + COL_CHUNK]).astype(z_ref.dtype)
        for g0 in range(0, COL_CHUNK, pair):
            cols = slice(g0, g0 + pair)
            gcols = slice(c0 + g0, c0 + g0 + pair)
            head = (c0 + g0) // dh
            qn = half_rms(acc_q[:, cols]) * gq_ref[...]
            kn = half_rms(acc_k[:, cols]) * gk_ref[...]
            v = acc_v[:, cols]
            aq, ak = aug_q[:, gcols], aug_k[:, gcols]
            qt_ref[0, head] = jnp.where(low, qn, aq).T.astype(qt_ref.dtype)
            qt_ref[0, head + 1] = jnp.where(low, aq, qn).T.astype(qt_ref.dtype)
            ka_ref[0, head] = jnp.where(low, kn, ak).astype(ka_ref.dtype)
            ka_ref[0, head + 1] = jnp.where(low, ak, kn).astype(ka_ref.dtype)
            vt_ref[0, head] = jnp.where(low, v, ones_col).T.astype(vt_ref.dtype)
            vt_ref[0, head + 1] = jnp.where(low, ones_col, v).T.astype(vt_ref.dtype)


def _fox_in(x, g, mod, w_in, b_f, g_q, g_k):
    bsz, seq, d = x.shape
    nh, dh = FOX_HEADS, FOX_DH
    ad = nh * dh
    tm = min(ROW_TILE, seq)
    w = jnp.pad(w_in, ((0, 0), (0, LANES - nh))).astype(BF16)
    bf = jnp.pad(b_f, (0, LANES - nh)).reshape(1, LANES)
    gq2 = jnp.tile(g_q * (dh ** -0.5 * LOG2E), 2).reshape(1, 2 * dh)
    gk2 = jnp.tile(g_k, 2).reshape(1, 2 * dh)
    tri = jnp.asarray(np.tril(np.ones((tm, tm), np.float32)), BF16)
    pq, pk, cq, ck = _fox_tables(nh, dh)
    kern = functools.partial(_fox_in_kernel, ad=ad, dh=dh)
    feat_major = jax.ShapeDtypeStruct((bsz, nh, LANES, seq), BF16)
    return pl.pallas_call(
        kern,
        grid=(bsz, seq // tm),
        in_specs=_in_specs_common(tm, d) + [
            _full((d, 4 * ad + LANES)), _full((1, 2 * dh)), _full((1, 2 * dh)), _full((1, LANES)),
            _full((tm, tm)), _full(pq.shape), _full(pk.shape), _full(cq.shape), _full(ck.shape)],
        out_specs=[
            pl.BlockSpec((1, nh, LANES, tm), lambda b, s: (b, 0, 0, s)),
            pl.BlockSpec((1, nh, tm, LANES), lambda b, s: (b, 0, s, 0)),
            pl.BlockSpec((1, nh, LANES, tm), lambda b, s: (b, 0, 0, s)),
            _row_spec(tm, ad)],
        out_shape=[feat_major, jax.ShapeDtypeStruct((bsz, nh, seq, LANES), BF16), feat_major,
                   jax.ShapeDtypeStruct((bsz, seq, ad), BF16)],
        scratch_shapes=[pltpu.VMEM((1, LANES), F32)],
        compiler_params=pltpu.CompilerParams(dimension_semantics=("arbitrary", "arbitrary")),
        name="fox_in_proj",
    )(x, g.reshape(1, d), mod, mod, w, gq2, gk2, bf, tri,
      jnp.asarray(pq, BF16), jnp.asarray(pk, BF16), jnp.asarray(cq), jnp.asarray(ck))


def _fox_attn_kernel(qt_ref, ka_ref, vt_ref, o_ref, s0_ref, s1_ref, m_ref, acc_ref,
                     *, tq, tk, dh):
    qi = pl.program_id(2)
    hp = FOX_HEADS_PER_STEP
    heads = range(hp)

    def scores(hh, kv):
        k0 = pl.multiple_of(kv * tk, tk)
        return _dot(ka_ref[0, hh, pl.ds(k0, tk), :], qt_ref[0, hh])

    def consume(hh, s_ref, kv, masked):
        k0 = pl.multiple_of(kv * tk, tk)
        if masked:
            visible = (lax.broadcasted_iota(jnp.int32, (tk, tq), 0)
                       <= lax.broadcasted_iota(jnp.int32, (tk, tq), 1))
            read = lambda: jnp.where(visible, s_ref[hh], -jnp.inf)
        else:
            read = lambda: s_ref[hh]
        m = m_ref[hh]
        m_new = jnp.maximum(m, jnp.max(read(), axis=0, keepdims=True))
        p = jnp.exp2(read() - m_new).astype(BF16)
        acc_ref[hh] = acc_ref[hh] * jnp.exp2(m - m_new) + _dot(vt_ref[0, hh, :, pl.ds(k0, tk)], p)
        m_ref[hh] = m_new

    m_ref[...] = jnp.full(m_ref.shape, -jnp.inf, F32)
    acc_ref[...] = jnp.zeros(acc_ref.shape, F32)
    for hh in heads:
        s0_ref[hh] = scores(hh, 0)

    def two_blocks(i, carry):
        j = 2 * i
        for hh in heads:
            s1_ref[hh] = scores(hh, j + 1)
            consume(hh, s0_ref, j, masked=False)
        for hh in heads:
            s0_ref[hh] = scores(hh, j + 2)
            consume(hh, s1_ref, j + 1, masked=False)
        return carry

    lax.fori_loop(0, qi // 2, two_blocks, 0)

    @pl.when(qi % 2 == 1)
    def _():
        for hh in heads:
            s1_ref[hh] = scores(hh, qi)
            consume(hh, s0_ref, qi - 1, masked=False)
        for hh in heads:
            consume(hh, s1_ref, qi, masked=True)

    @pl.when(qi % 2 == 0)
    def _():
        for hh in heads:
            consume(hh, s0_ref, qi, masked=True)

    acc_e, acc_o = acc_ref[0], acc_ref[1]
    out_t = jnp.concatenate([acc_e[:dh] / acc_e[dh:dh + 1],
                             acc_o[dh:] / acc_o[0:1]], axis=0)
    o_ref[0] = out_t.T.astype(o_ref.dtype)


def _fox_attn(qt, ka, vt):
    bsz, nh, _, seq = qt.shape
    dh = FOX_DH
    tq, tk = min(FOX_TQ, seq), min(FOX_TK, seq)
    assert tq == tk and FOX_HEADS_PER_STEP == 2
    hp = FOX_HEADS_PER_STEP
    kern = functools.partial(_fox_attn_kernel, tq=tq, tk=tk, dh=dh)
    return pl.pallas_call(
        kern,
        grid=(bsz, nh // hp, seq // tq),
        in_specs=[
            pl.BlockSpec((1, hp, LANES, tq), lambda b, h, s: (b, h, 0, s)),
            pl.BlockSpec((1, hp, seq, LANES), lambda b, h, s: (b, h, 0, 0)),
            pl.BlockSpec((1, hp, LANES, seq), lambda b, h, s: (b, h, 0, 0)),
        ],
        out_specs=pl.BlockSpec((1, tq, hp * dh), lambda b, h, s: (b, s, h)),
        out_shape=jax.ShapeDtypeStruct((bsz, seq, nh * dh), BF16),
        scratch_shapes=[pltpu.VMEM((hp, tk, tq), F32), pltpu.VMEM((hp, tk, tq), F32),
                        pltpu.VMEM((hp, 1, tq), F32), pltpu.VMEM((hp, LANES, tq), F32)],
        name="fox_attn",
    )(qt, ka, vt)


def _out_kernel(o_ref, z_ref, w_ref, g_ref, gate_ref, x_ref, y_ref):
    gated = o_ref[0].astype(F32) * jax.nn.silu(z_ref[0].astype(F32))
    y = _dot(gated.astype(BF16), w_ref[...])
    y_ref[0] = x_ref[0] + gate_ref[0] * (_rms(y) * g_ref[...])


def _out_proj(o, z, w_out, g, mod, x):
    bsz, seq, d = x.shape
    width = o.shape[-1]
    tm = min(ROW_TILE, seq)
    return pl.pallas_call(
        _out_kernel,
        grid=(bsz, seq // tm),
        in_specs=[_row_spec(tm, width), _row_spec(tm, width), _full((width, d)), _full((1, d)),
                  pl.BlockSpec((1, 1, d), lambda b, s: (b, 0, 2)), _row_spec(tm, d)],
        out_specs=_row_spec(tm, d),
        out_shape=jax.ShapeDtypeStruct((bsz, seq, d), F32),
        name="out_proj",
    )(o, z, w_out.astype(BF16), g.reshape(1, d), mod, x)


def _gla_layer(x, g_pre, mod, w_in, w_a2, b_a, g_head):
    q, k, v, z, gk = _gla_in(x, g_pre, mod, w_in, w_a2, b_a)
    return _gla_scan(q, k, v, gk, g_head), z


def _sgu_layer(x, g_pre, mod, w_in, ln_g, ln_b, w_s, b_s):
    u, v, z = _sgu_in(x, g_pre, mod, w_in, ln_g, ln_b)
    return _sgu_mix(u, v, w_s, b_s), z


def _fox_layer(x, g_pre, mod, w_in, b_f, g_q, g_k):
    qt, ka, vt, z = _fox_in(x, g_pre, mod, w_in, b_f, g_q, g_k)
    return _fox_attn(qt, ka, vt), z


def kernel(x, c, norm_pre_g, norm_post_g, w_mod, b_mod, gla_w_in, gla_w_a2, gla_b_a, gla_g_head, gla_w_out, sgu_w_in, sgu_ln_g, sgu_ln_b, sgu_w_s, sgu_b_s, sgu_w_out, fox_w_in, fox_b_f, fox_g_q, fox_g_k, fox_w_out):
    depth = w_mod.shape[0]
    bsz = x.shape[0]
    mod_all = _modulation(c, w_mod, b_mod)
    for i in range(depth):
        mod = mod_all[i].reshape(bsz, 1, -1)
        kind, j = i % N_MIXERS, i // N_MIXERS
        if kind == 0:
            o, z = _gla_layer(x, norm_pre_g[i], mod, gla_w_in[j], gla_w_a2[j], gla_b_a[j],
                              gla_g_head[j])
            w_out = gla_w_out[j]
        elif kind == 1:
            o, z = _sgu_layer(x, norm_pre_g[i], mod, sgu_w_in[j], sgu_ln_g[j], sgu_ln_b[j],
                              sgu_w_s[j], sgu_b_s[j])
            w_out = sgu_w_out[j]
        else:
            o, z = _fox_layer(x, norm_pre_g[i], mod, fox_w_in[j], fox_b_f[j], fox_g_q[j],
                              fox_g_k[j])
            w_out = fox_w_out[j]
        x = _out_proj(o, z, w_out, norm_post_g[i], mod, x)
    return x
```

```python
import functools
import math

import numpy as np
import jax
import jax.numpy as jnp
from jax import lax
from jax.experimental import pallas as pl
from jax.experimental.pallas import tpu as pltpu

EPS = 1e-6
N_MIXERS = 3

LANES = 128
ROW_TILE = 512
COL_CHUNK = 512

GLA_HEADS = 4
GLA_RANK = 16
GLA_NORMALIZER = 16.0
GLA_CHUNK = 64
GLA_BLOCK = 512

SGU_GROUPS = 4
SGU_CHUNK = 128

FOX_HEADS = 16
FOX_DH = 64
FOX_TQ = 512
FOX_TK = 512
FOX_HEADS_PER_STEP = 4
FOX_SUM_ROWS = 16
FOX_PARTS = 3
LOG2E = math.log2(math.e)

BF16 = jnp.bfloat16
F32 = jnp.float32


def _dot(a, b):
    return jnp.dot(a, b, preferred_element_type=F32)


def _dot_nt(a, b):
    return lax.dot_general(a, b, (((1,), (1,)), ((), ())), preferred_element_type=F32)


def _dot_tn(a, b):
    return lax.dot_general(a, b, (((0,), (0,)), ((), ())), preferred_element_type=F32)


def _split2(x):
    hi = x.astype(BF16)
    lo = (x - hi.astype(F32)).astype(BF16)
    return hi, lo


def _split3(x):
    hi = x.astype(BF16)
    r = x - hi.astype(F32)
    mid = r.astype(BF16)
    lo = (r - mid.astype(F32)).astype(BF16)
    return hi, mid, lo


def _log_sigmoid(x):
    return jnp.minimum(x, 0.0) - jnp.log1p(jnp.exp(-jnp.abs(x)))


def _rms(x):
    return x * lax.rsqrt(jnp.mean(x * x, axis=-1, keepdims=True) + EPS)


def _mod_kernel(c_ref, w_ref, b_ref, o_ref):
    cond = jax.nn.silu(c_ref[...])
    o_ref[0] = _dot(cond, w_ref[0]) + b_ref[0]


def _modulation(c, w_mod, b_mod):
    depth, d, d3 = w_mod.shape
    bsz = c.shape[0]
    nblk = d3 // d
    return pl.pallas_call(
        _mod_kernel,
        grid=(depth, nblk),
        in_specs=[
            pl.BlockSpec((bsz, d), lambda i, j: (0, 0)),
            pl.BlockSpec((1, d, d), lambda i, j: (i, 0, j)),
            pl.BlockSpec((1, 1, d), lambda i, j: (i, 0, j)),
        ],
        out_specs=pl.BlockSpec((1, bsz, d), lambda i, j: (i, 0, j)),
        out_shape=jax.ShapeDtypeStruct((depth, bsz, d3), F32),
        name="adaln_mod",
    )(c, w_mod, b_mod.reshape(depth, 1, d3))


def _prenorm(x_ref, g_ref, shift_ref, scale_ref):
    h = _rms(x_ref[0]) * g_ref[...]
    h = h * (1.0 + scale_ref[0]) + shift_ref[0]
    return h.astype(BF16)


def _in_specs_common(tm, d):
    return [
        pl.BlockSpec((1, tm, d), lambda b, s: (b, s, 0)),
        pl.BlockSpec((1, d), lambda b, s: (0, 0)),
        pl.BlockSpec((1, 1, d), lambda b, s: (b, 0, 0)),
        pl.BlockSpec((1, 1, d), lambda b, s: (b, 0, 1)),
    ]


def _full(shape):
    nd = len(shape)
    return pl.BlockSpec(shape, lambda b, s: (0,) * nd)


def _row_spec(tm, width, col=0):
    return pl.BlockSpec((1, tm, width), lambda b, s: (b, s, col))


def _gla_in_kernel(x_ref, g_ref, shift_ref, scale_ref, w_ref, wa2_ref, ba_ref,
                   q_ref, k_ref, v_ref, z_ref, gk_ref, *, kd, vd, q_scale):
    h = _prenorm(x_ref, g_ref, shift_ref, scale_ref)
    col = 0
    for out_ref, width, mul in ((q_ref, kd, q_scale), (k_ref, kd, None),
                                (v_ref, vd, None), (z_ref, vd, None)):
        for c0 in range(0, width, COL_CHUNK):
            acc = _dot(h, w_ref[:, col + c0:col + c0 + COL_CHUNK])
            if mul is not None:
                acc = acc * mul
            out_ref[0, :, c0:c0 + COL_CHUNK] = acc.astype(out_ref.dtype)
        col += width
    a_low = _dot(h, w_ref[:, col:col + LANES])
    logit = _dot(a_low.astype(BF16), wa2_ref[...]) + ba_ref[...]
    gk_ref[0] = _log_sigmoid(logit) * (1.0 / GLA_NORMALIZER)


def _gla_in(x, g, mod, w_in, w_a2, b_a):
    bsz, seq, d = x.shape
    kd = w_a2.shape[1]
    vd = (w_in.shape[1] - 2 * kd - GLA_RANK) // 2
    n_main = 2 * kd + 2 * vd
    tm = min(ROW_TILE, seq)
    w = jnp.pad(w_in, ((0, 0), (0, LANES - GLA_RANK))).astype(BF16)
    wa2 = jnp.pad(w_a2, ((0, LANES - GLA_RANK), (0, 0))).astype(BF16)
    dk = kd // GLA_HEADS
    kern = functools.partial(_gla_in_kernel, kd=kd, vd=vd, q_scale=dk ** -0.5)
    return pl.pallas_call(
        kern,
        grid=(bsz, seq // tm),
        in_specs=_in_specs_common(tm, d) + [
            _full((d, n_main + LANES)), _full((LANES, kd)), _full((1, kd))],
        out_specs=[_row_spec(tm, kd), _row_spec(tm, kd), _row_spec(tm, vd),
                   _row_spec(tm, vd), _row_spec(tm, kd)],
        out_shape=[jax.ShapeDtypeStruct((bsz, seq, kd), BF16),
                   jax.ShapeDtypeStruct((bsz, seq, kd), BF16),
                   jax.ShapeDtypeStruct((bsz, seq, vd), BF16),
                   jax.ShapeDtypeStruct((bsz, seq, vd), BF16),
                   jax.ShapeDtypeStruct((bsz, seq, kd), F32)],
        name="gla_in_proj",
    )(x, g.reshape(1, d), mod, mod, w, wa2, b_a.reshape(1, kd))


def _gla_tables(c):
    tri = np.tril(np.ones((c, c), np.float32))
    rows = [tri, 1.0 - tri]
    n_levels = int(math.log2(c))
    level = np.full((c, c), -1, np.int32)
    idx = np.arange(c)
    for l in range(n_levels):
        blk = c >> l
        half = blk // 2
        mid = (idx // blk) * blk + half - 1
        rows.append(tri - tri[mid])
        same = (idx[:, None] // blk) == (idx[None, :] // blk)
        upper = (idx[:, None] % blk) >= half
        lower = (idx[None, :] % blk) < half
        level[same & upper & lower] = l
    level[idx, idx] = n_levels
    return np.concatenate(rows, axis=0), level, n_levels


def _gla_scan_kernel(q_ref, k_ref, v_ref, gk_ref, w_ref, lvl_ref, gh_ref, o_ref, st_ref,
                     *, chunk, n_levels, n_chunks, dk, dv):
    @pl.when(pl.program_id(1) == 0)
    def _():
        st_ref[...] = jnp.zeros_like(st_ref)

    w = w_ref[...]
    lvl = lvl_ref[...]

    def head_chunk(hd, rows):
        kc = slice(hd * dk, (hd + 1) * dk)
        vc = slice(hd * dv, (hd + 1) * dv)
        r = _dot(w, jnp.concatenate(_split2(gk_ref[0, rows, kc]), axis=0))
        b = r[0:chunk]
        rev = r[chunk:2 * chunk]
        qf = q_ref[0, rows, kc].astype(F32)
        kf = k_ref[0, rows, kc].astype(F32)
        v = v_ref[0, rows, vc]
        st = st_ref[hd]

        att = jnp.zeros((chunk, chunk), F32)
        for l in range(n_levels + 1):
            if l < n_levels:
                e = jnp.exp(-jnp.abs(r[(2 + l) * chunk:(3 + l) * chunk]))
                ql, kl = (qf * e).astype(BF16), (kf * e).astype(BF16)
            else:
                ql, kl = qf.astype(BF16), kf.astype(BF16)
            att = jnp.where(lvl == l, _dot_nt(ql, kl), att)

        o = _dot_nt((qf * jnp.exp(b)).astype(BF16), st.astype(BF16))
        o = o + _dot(att.astype(BF16), v)
        o_ref[0, rows, vc] = (_rms(o) * gh_ref[:, vc]).astype(o_ref.dtype)

        k_rev = (kf * jnp.exp(rev)).astype(BF16)
        st_ref[hd] = st * jnp.exp(b[chunk - 1:chunk, :]) + _dot_tn(v, k_rev)

    def body(ci, carry):
        rows = pl.ds(pl.multiple_of(ci * chunk, chunk), chunk)
        for hd in range(GLA_HEADS):
            head_chunk(hd, rows)
        return carry

    lax.fori_loop(0, n_chunks, body, 0, unroll=2)


def _gla_scan(q, k, v, gk, g_head):
    bsz, seq, kd = q.shape
    vd = v.shape[-1]
    dk, dv = kd // GLA_HEADS, vd // GLA_HEADS
    tb = min(GLA_BLOCK, seq)
    w_np, lvl_np, n_levels = _gla_tables(GLA_CHUNK)
    w2 = jnp.asarray(np.concatenate([w_np, w_np], axis=1), BF16)
    kern = functools.partial(_gla_scan_kernel, chunk=GLA_CHUNK, n_levels=n_levels,
                             n_chunks=tb // GLA_CHUNK, dk=dk, dv=dv)
    return pl.pallas_call(
        kern,
        grid=(bsz, seq // tb),
        in_specs=[_row_spec(tb, kd), _row_spec(tb, kd), _row_spec(tb, vd), _row_spec(tb, kd),
                  _full(w2.shape), _full(lvl_np.shape), _full((1, vd))],
        out_specs=_row_spec(tb, vd),
        out_shape=jax.ShapeDtypeStruct((bsz, seq, vd), BF16),
        scratch_shapes=[pltpu.VMEM((GLA_HEADS, dv, dk), F32)],
        compiler_params=pltpu.CompilerParams(dimension_semantics=("arbitrary", "arbitrary")),
        name="gla_scan",
    )(q, k, v, gk, w2, jnp.asarray(lvl_np), g_head.reshape(1, vd))


def _sgu_in_kernel(x_ref, g_ref, shift_ref, scale_ref, w_ref, lng_ref, lnb_ref,
                   u_ref, v_ref, z_ref, vtmp_ref, *, width):
    h = _prenorm(x_ref, g_ref, shift_ref, scale_ref)
    for c0 in range(0, width, COL_CHUNK):
        cols = slice(c0, c0 + COL_CHUNK)
        u_ref[0, :, cols] = jax.nn.gelu(_dot(h, w_ref[:, cols])).astype(u_ref.dtype)
        vtmp_ref[:, cols] = jax.nn.gelu(_dot(h, w_ref[:, width + c0:width + c0 + COL_CHUNK]))
        z_ref[0, :, cols] = _dot(h, w_ref[:, 2 * width + c0:2 * width + c0 + COL_CHUNK]
                                 ).astype(z_ref.dtype)
    v = vtmp_ref[...]
    mu = jnp.mean(v, axis=-1, keepdims=True)
    vc = v - mu
    var = jnp.mean(vc * vc, axis=-1, keepdims=True)
    v_ref[0] = (vc * lax.rsqrt(var + EPS) * lng_ref[...] + lnb_ref[...]).astype(v_ref.dtype)


def _sgu_in(x, g, mod, w_in, ln_g, ln_b):
    bsz, seq, d = x.shape
    width = w_in.shape[1] // 3
    tm = min(ROW_TILE, seq)
    kern = functools.partial(_sgu_in_kernel, width=width)
    act = jax.ShapeDtypeStruct((bsz, seq, width), BF16)
    return pl.pallas_call(
        kern,
        grid=(bsz, seq // tm),
        in_specs=_in_specs_common(tm, d) + [
            _full((d, 3 * width)), _full((1, width)), _full((1, width))],
        out_specs=[_row_spec(tm, width)] * 3,
        out_shape=[act, act, act],
        scratch_shapes=[pltpu.VMEM((tm, width), F32)],
        name="sgu_in_proj",
    )(x, g.reshape(1, d), mod, mod, w_in.astype(BF16), ln_g.reshape(1, width),
      ln_b.reshape(1, width))


def _sgu_mix_kernel(u_ref, v_ref, ws_ref, bs_ref, o_ref, *, chunk, gdim, n_chunks):
    for g in range(SGU_GROUPS):
        cols = slice(g * gdim, (g + 1) * gdim)
        w = ws_ref[g]
        bias = bs_ref[g]
        bias = jnp.concatenate([bias] * (gdim // LANES), axis=1)
        for c in range(n_chunks):
            rows = slice(c * chunk, (c + 1) * chunk)
            mixed = _dot(w, v_ref[0, rows, cols]) + bias
            o_ref[0, rows, cols] = (u_ref[0, rows, cols].astype(F32) * mixed).astype(o_ref.dtype)


def _sgu_mix(u, v, w_s, b_s):
    bsz, seq, width = u.shape
    gdim = width // SGU_GROUPS
    tm = min(ROW_TILE, seq)
    chunk = SGU_CHUNK
    causal = jnp.tril(jnp.ones((chunk, chunk), bool))
    w_causal = jnp.where(causal[None], w_s, 0.0).astype(BF16)
    bias = jnp.broadcast_to(b_s[:, :, None], (SGU_GROUPS, chunk, LANES))
    kern = functools.partial(_sgu_mix_kernel, chunk=chunk, gdim=gdim, n_chunks=tm // chunk)
    return pl.pallas_call(
        kern,
        grid=(bsz, seq // tm),
        in_specs=[_row_spec(tm, width), _row_spec(tm, width),
                  _full((SGU_GROUPS, chunk, chunk)), _full((SGU_GROUPS, chunk, LANES))],
        out_specs=_row_spec(tm, width),
        out_shape=jax.ShapeDtypeStruct((bsz, seq, width), BF16),
        name="sgu_mix",
    )(u, v, w_causal, bias)


def _fox_tables(n_heads, dh):
    width = n_heads * dh
    pq = np.zeros((FOX_PARTS * LANES, width), np.float32)
    pk = np.zeros((FOX_PARTS * LANES, width), np.float32)
    cq = np.zeros((1, width), np.float32)
    ck = np.zeros((1, width), np.float32)
    for h in range(n_heads):
        base = (h // 2) * 2 * dh + (dh if h % 2 == 0 else 0)
        for p in range(FOX_PARTS):
            pq[p * LANES + h, base + p] = 1.0
            cq[0, base + FOX_PARTS + p] = 1.0
            ck[0, base + p] = 1.0
            pk[p * LANES + h, base + FOX_PARTS + p] = -1.0
    return pq, pk, cq, ck


def _fox_in_kernel(x_ref, g_ref, shift_ref, scale_ref, w_ref, gq_ref, gk_ref, bf_ref, tri_ref,
                   pq_ref, pk_ref, cq_ref, ck_ref,
                   qt_ref, ka_ref, vt_ref, z_ref, carry_ref, *, ad, dh):
    @pl.when(pl.program_id(1) == 0)
    def _():
        carry_ref[...] = jnp.zeros_like(carry_ref)

    h = _prenorm(x_ref, g_ref, shift_ref, scale_ref)
    tm = h.shape[0]
    pair = 2 * dh
    lane = lax.broadcasted_iota(jnp.int32, (tm, pair), 1)
    low = lane < dh

    f_logit = _dot(h, w_ref[:, 4 * ad:4 * ad + LANES])
    tri = tri_ref[...]
    lf_hi, lf_mid, lf_lo = _split3(_log_sigmoid(f_logit + bf_ref[...]))
    f_cum = _dot(tri, lf_hi) + _dot(tri, lf_mid) + _dot(tri, lf_lo) + carry_ref[...]
    carry_ref[...] = f_cum[tm - 1:tm, :]
    f3 = jnp.concatenate(_split3(f_cum * LOG2E), axis=1)
    aug_q = _dot(f3, pq_ref[...]) + cq_ref[...]
    aug_k = _dot(f3, pk_ref[...]) + ck_ref[...]

    def half_rms(x):
        sq = x * x
        s_lo = jnp.sum(jnp.where(low, sq, 0.0), axis=-1, keepdims=True)
        s_hi = jnp.sum(jnp.where(low, 0.0, sq), axis=-1, keepdims=True)
        return x * lax.rsqrt(jnp.where(low, s_lo, s_hi) * (1.0 / dh) + EPS)

    for c0 in range(0, ad, COL_CHUNK):
        acc_q = _dot(h, w_ref[:, c0:c0 + COL_CHUNK])
        acc_k = _dot(h, w_ref[:, ad + c0:ad + c0 + COL_CHUNK])
        acc_v = _dot(h, w_ref[:, 2 * ad + c0:2 * ad + c0 + COL_CHUNK])
        z_ref[0, :, c0:c0 + COL_CHUNK] = _dot(
            h, w_ref[:, 3 * ad + c0:3 * ad + c0 + COL_CHUNK]).astype(z_ref.dtype)
        for g0 in range(0, COL_CHUNK, pair):
            cols = slice(g0, g0 + pair)
            gcols = slice(c0 + g0, c0 + g0 + pair)
            head = (c0 + g0) // dh
            qn = half_rms(acc_q[:, cols]) * gq_ref[...]
            kn = half_rms(acc_k[:, cols]) * gk_ref[...]
            v = acc_v[:, cols]
            aq, ak = aug_q[:, gcols], aug_k[:, gcols]
            qt_ref[0, head] = jnp.where(low, qn, aq).T.astype(qt_ref.dtype)
            qt_ref[0, head + 1] = jnp.where(low, aq, qn).T.astype(qt_ref.dtype)
            ka_ref[0, head] = jnp.where(low, kn, ak).astype(ka_ref.dtype)
            ka_ref[0, head + 1] = jnp.where(low, ak, kn).astype(ka_ref.dtype)
            v_t = v.T.astype(vt_ref.dtype)
            vt_ref[0, head] = v_t[:dh]
            vt_ref[0, head + 1] = v_t[dh:]


def _fox_in(x, g, mod, w_in, b_f, g_q, g_k):
    bsz, seq, d = x.shape
    nh, dh = FOX_HEADS, FOX_DH
    ad = nh * dh
    tm = min(ROW_TILE, seq)
    w = jnp.pad(w_in, ((0, 0), (0, LANES - nh))).astype(BF16)
    bf = jnp.pad(b_f, (0, LANES - nh)).reshape(1, LANES)
    gq2 = jnp.tile(g_q * (dh ** -0.5 * LOG2E), 2).reshape(1, 2 * dh)
    gk2 = jnp.tile(g_k, 2).reshape(1, 2 * dh)
    tri = jnp.asarray(np.tril(np.ones((tm, tm), np.float32)), BF16)
    pq, pk, cq, ck = _fox_tables(nh, dh)
    kern = functools.partial(_fox_in_kernel, ad=ad, dh=dh)
    feat_major = jax.ShapeDtypeStruct((bsz, nh, LANES, seq), BF16)
    return pl.pallas_call(
        kern,
        grid=(bsz, seq // tm),
        in_specs=_in_specs_common(tm, d) + [
            _full((d, 4 * ad + LANES)), _full((1, 2 * dh)), _full((1, 2 * dh)), _full((1, LANES)),
            _full((tm, tm)), _full(pq.shape), _full(pk.shape), _full(cq.shape), _full(ck.shape)],
        out_specs=[
            pl.BlockSpec((1, nh, LANES, tm), lambda b, s: (b, 0, 0, s)),
            pl.BlockSpec((1, nh, tm, LANES), lambda b, s: (b, 0, s, 0)),
            pl.BlockSpec((1, nh, dh, tm), lambda b, s: (b, 0, 0, s)),
            _row_spec(tm, ad)],
        out_shape=[feat_major, jax.ShapeDtypeStruct((bsz, nh, seq, LANES), BF16),
                   jax.ShapeDtypeStruct((bsz, nh, dh, seq), BF16),
                   jax.ShapeDtypeStruct((bsz, seq, ad), BF16)],
        scratch_shapes=[pltpu.VMEM((1, LANES), F32)],
        compiler_params=pltpu.CompilerParams(dimension_semantics=("arbitrary", "arbitrary")),
        name="fox_in_proj",
    )(x, g.reshape(1, d), mod, mod, w, gq2, gk2, bf, tri,
      jnp.asarray(pq, BF16), jnp.asarray(pk, BF16), jnp.asarray(cq), jnp.asarray(ck))


def _fox_attn_kernel(qt_ref, ka_ref, vt_ref, o_ref, s0_ref, s1_ref, m_ref, acc_ref,
                     *, tq, tk, dh):
    qi = pl.program_id(2)
    hp = FOX_HEADS_PER_STEP
    heads = range(hp)
    sum_rows = jnp.ones((FOX_SUM_ROWS, tk), BF16)

    def scores(hh, kv):
        k0 = pl.multiple_of(kv * tk, tk)
        return _dot(ka_ref[0, hh, pl.ds(k0, tk), :], qt_ref[0, hh])

    def consume(hh, s_ref, kv, masked):
        k0 = pl.multiple_of(kv * tk, tk)
        if masked:
            visible = (lax.broadcasted_iota(jnp.int32, (tk, tq), 0)
                       <= lax.broadcasted_iota(jnp.int32, (tk, tq), 1))
            read = lambda: jnp.where(visible, s_ref[hh], -jnp.inf)
        else:
            read = lambda: s_ref[hh]
        m = m_ref[hh]
        m_new = jnp.maximum(m, jnp.max(read(), axis=0, keepdims=True))
        p = jnp.exp2(read() - m_new).astype(BF16)
        v_sum = jnp.concatenate([vt_ref[0, hh, :, pl.ds(k0, tk)], sum_rows], axis=0)
        acc_ref[hh] = acc_ref[hh] * jnp.exp2(m - m_new) + _dot(v_sum, p)
        m_ref[hh] = m_new

    m_ref[...] = jnp.full(m_ref.shape, -jnp.inf, F32)
    acc_ref[...] = jnp.zeros(acc_ref.shape, F32)
    for hh in heads:
        s0_ref[hh] = scores(hh, 0)

    def two_blocks(i, carry):
        j = 2 * i
        for hh in heads:
            s1_ref[hh] = scores(hh, j + 1)
            consume(hh, s0_ref, j, masked=False)
        for hh in heads:
            s0_ref[hh] = scores(hh, j + 2)
            consume(hh, s1_ref, j + 1, masked=False)
        return carry

    lax.fori_loop(0, qi // 2, two_blocks, 0)

    @pl.when(qi % 2 == 1)
    def _():
        for hh in heads:
            s1_ref[hh] = scores(hh, qi)
            consume(hh, s0_ref, qi - 1, masked=False)
        for hh in heads:
            consume(hh, s1_ref, qi, masked=True)

    @pl.when(qi % 2 == 0)
    def _():
        for hh in heads:
            consume(hh, s0_ref, qi, masked=True)

    out_t = jnp.concatenate([acc_ref[hh, :dh] / acc_ref[hh, dh:dh + 1] for hh in heads], axis=0)
    o_ref[0] = out_t.T.astype(o_ref.dtype)


def _fox_attn(qt, ka, vt):
    bsz, nh, _, seq = qt.shape
    dh = FOX_DH
    tq, tk = min(FOX_TQ, seq), min(FOX_TK, seq)
    hp = FOX_HEADS_PER_STEP
    assert tq == tk and (hp * dh) % LANES == 0
    kern = functools.partial(_fox_attn_kernel, tq=tq, tk=tk, dh=dh)
    return pl.pallas_call(
        kern,
        grid=(bsz, nh // hp, seq // tq),
        in_specs=[
            pl.BlockSpec((1, hp, LANES, tq), lambda b, h, s: (b, h, 0, s)),
            pl.BlockSpec((1, hp, seq, LANES), lambda b, h, s: (b, h, 0, 0)),
            pl.BlockSpec((1, hp, dh, seq), lambda b, h, s: (b, h, 0, 0)),
        ],
        out_specs=pl.BlockSpec((1, tq, hp * dh), lambda b, h, s: (b, s, h)),
        out_shape=jax.ShapeDtypeStruct((bsz, seq, nh * dh), BF16),
        scratch_shapes=[pltpu.VMEM((hp, tk, tq), F32), pltpu.VMEM((hp, tk, tq), F32),
                        pltpu.VMEM((hp, 1, tq), F32),
                        pltpu.VMEM((hp, dh + FOX_SUM_ROWS, tq), F32)],
        name="fox_attn",
    )(qt, ka, vt)


def _out_kernel(o_ref, z_ref, w_ref, g_ref, gate_ref, x_ref, y_ref):
    gated = o_ref[0].astype(F32) * jax.nn.silu(z_ref[0].astype(F32))
    y = _dot(gated.astype(BF16), w_ref[...])
    y_ref[0] = x_ref[0] + gate_ref[0] * (_rms(y) * g_ref[...])


def _out_proj(o, z, w_out, g, mod, x):
    bsz, seq, d = x.shape
    width = o.shape[-1]
    tm = min(ROW_TILE, seq)
    return pl.pallas_call(
        _out_kernel,
        grid=(bsz, seq // tm),
        in_specs=[_row_spec(tm, width), _row_spec(tm, width), _full((width, d)), _full((1, d)),
                  pl.BlockSpec((1, 1, d), lambda b, s: (b, 0, 2)), _row_spec(tm, d)],
        out_specs=_row_spec(tm, d),
        out_shape=jax.ShapeDtypeStruct((bsz, seq, d), F32),
        name="out_proj",
    )(o, z, w_out.astype(BF16), g.reshape(1, d), mod, x)


def _gla_layer(x, g_pre, mod, w_in, w_a2, b_a, g_head):
    q, k, v, z, gk = _gla_in(x, g_pre, mod, w_in, w_a2, b_a)
    return _gla_scan(q, k, v, gk, g_head), z


def _sgu_layer(x, g_pre, mod, w_in, ln_g, ln_b, w_s, b_s):
    u, v, z = _sgu_in(x, g_pre, mod, w_in, ln_g, ln_b)
    return _sgu_mix(u, v, w_s, b_s), z


def _fox_layer(x, g_pre, mod, w_in, b_f, g_q, g_k):
    qt, ka, vt, z = _fox_in(x, g_pre, mod, w_in, b_f, g_q, g_k)
    return _fox_attn(qt, ka, vt), z


def kernel(x, c, norm_pre_g, norm_post_g, w_mod, b_mod, gla_w_in, gla_w_a2, gla_b_a, gla_g_head, gla_w_out, sgu_w_in, sgu_ln_g, sgu_ln_b, sgu_w_s, sgu_b_s, sgu_w_out, fox_w_in, fox_b_f, fox_g_q, fox_g_k, fox_w_out):
    depth = w_mod.shape[0]
    bsz = x.shape[0]
    mod_all = _modulation(c, w_mod, b_mod)
    for i in range(depth):
        mod = mod_all[i].reshape(bsz, 1, -1)
        kind, j = i % N_MIXERS, i // N_MIXERS
        if kind == 0:
            o, z = _gla_layer(x, norm_pre_g[i], mod, gla_w_in[j], gla_w_a2[j], gla_b_a[j],
                              gla_g_head[j])
            w_out = gla_w_out[j]
        elif kind == 1:
            o, z = _sgu_layer(x, norm_pre_g[i], mod, sgu_w_in[j], sgu_ln_g[j], sgu_ln_b[j],
                              sgu_w_s[j], sgu_b_s[j])
            w_out = sgu_w_out[j]
        else:
            o, z = _fox_layer(x, norm_pre_g[i], mod, fox_w_in[j], fox_b_f[j], fox_g_q[j],
                              fox_g_k[j])
            w_out = fox_w_out[j]
        x = _out_proj(o, z, w_out, norm_post_g[i], mod, x)
    return x
```

```python
import functools
import math

import numpy as np
import jax
import jax.numpy as jnp
from jax import lax
from jax.experimental import pallas as pl
from jax.experimental.pallas import tpu as pltpu

EPS = 1e-6
N_MIXERS = 3

LANES = 128
ROW_TILE = 512
ROW_SUBTILE = 256
COL_CHUNK = 512

GLA_HEADS = 4
GLA_RANK = 16
GLA_NORMALIZER = 16.0
GLA_CHUNK = 64
GLA_FAST_CHUNK = 256
GLA_DECAY_GUARD = 70.0
GLA_BLOCK = 512

SGU_GROUPS = 4
SGU_CHUNK = 128

FOX_HEADS = 16
FOX_DH = 64
FOX_TQ = 512
FOX_TK = 512
FOX_HEADS_PER_STEP = 4
FOX_SUM_ROWS = 16
FOX_PARTS = 3
LOG2E = math.log2(math.e)

BF16 = jnp.bfloat16
F32 = jnp.float32


def _dot(a, b):
    return jnp.dot(a, b, preferred_element_type=F32)


def _dot_nt(a, b):
    return lax.dot_general(a, b, (((1,), (1,)), ((), ())), preferred_element_type=F32)


def _dot_tn(a, b):
    return lax.dot_general(a, b, (((0,), (0,)), ((), ())), preferred_element_type=F32)


def _split2(x):
    hi = x.astype(BF16)
    lo = (x - hi.astype(F32)).astype(BF16)
    return hi, lo


def _split3(x):
    hi = x.astype(BF16)
    r = x - hi.astype(F32)
    mid = r.astype(BF16)
    lo = (r - mid.astype(F32)).astype(BF16)
    return hi, mid, lo


def _log_sigmoid(x):
    return jnp.minimum(x, 0.0) - jnp.log1p(jnp.exp(-jnp.abs(x)))


def _rms(x):
    return x * lax.rsqrt(jnp.mean(x * x, axis=-1, keepdims=True) + EPS)


def _mod_kernel(c_ref, w_ref, b_ref, o_ref):
    cond = jax.nn.silu(c_ref[...])
    o_ref[0] = _dot(cond, w_ref[0]) + b_ref[0]


def _modulation(c, w_mod, b_mod):
    depth, d, d3 = w_mod.shape
    bsz = c.shape[0]
    nblk = d3 // d
    return pl.pallas_call(
        _mod_kernel,
        grid=(depth, nblk),
        in_specs=[
            pl.BlockSpec((bsz, d), lambda i, j: (0, 0)),
            pl.BlockSpec((1, d, d), lambda i, j: (i, 0, j)),
            pl.BlockSpec((1, 1, d), lambda i, j: (i, 0, j)),
        ],
        out_specs=pl.BlockSpec((1, bsz, d), lambda i, j: (i, 0, j)),
        out_shape=jax.ShapeDtypeStruct((depth, bsz, d3), F32),
        name="adaln_mod",
    )(c, w_mod, b_mod.reshape(depth, 1, d3))


def _prenorm(x_ref, g_ref, shift_ref, scale_ref, rows):
    h = _rms(x_ref[0, rows, :]) * g_ref[...]
    h = h * (1.0 + scale_ref[0]) + shift_ref[0]
    return h.astype(BF16)


def _subtiles(tm):
    sub = min(ROW_SUBTILE, tm)
    return [slice(r, r + sub) for r in range(0, tm, sub)]


def _in_specs_common(tm, d):
    return [
        pl.BlockSpec((1, tm, d), lambda b, s: (b, s, 0)),
        pl.BlockSpec((1, d), lambda b, s: (0, 0)),
        pl.BlockSpec((1, 1, d), lambda b, s: (b, 0, 0)),
        pl.BlockSpec((1, 1, d), lambda b, s: (b, 0, 1)),
    ]


def _full(shape):
    nd = len(shape)
    return pl.BlockSpec(shape, lambda b, s: (0,) * nd)


def _row_spec(tm, width, col=0):
    return pl.BlockSpec((1, tm, width), lambda b, s: (b, s, col))


def _gla_in_kernel(x_ref, g_ref, shift_ref, scale_ref, w_ref, wa2_ref, ba_ref,
                   q_ref, k_ref, v_ref, z_ref, gk_ref, *, kd, vd, q_scale):
    n_main = 2 * kd + 2 * vd
    for rows in _subtiles(x_ref.shape[1]):
        h = _prenorm(x_ref, g_ref, shift_ref, scale_ref, rows)
        a_low = _dot(h, w_ref[:, n_main:n_main + LANES])
        logit = _dot(a_low.astype(BF16), wa2_ref[...]) + ba_ref[...]
        gk_ref[0, rows, :] = _log_sigmoid(logit) * (1.0 / GLA_NORMALIZER)
        col = 0
        for out_ref, width, mul in ((q_ref, kd, q_scale), (k_ref, kd, None),
                                    (v_ref, vd, None), (z_ref, vd, None)):
            for c0 in range(0, width, COL_CHUNK):
                acc = _dot(h, w_ref[:, col + c0:col + c0 + COL_CHUNK])
                if mul is not None:
                    acc = acc * mul
                out_ref[0, rows, c0:c0 + COL_CHUNK] = acc.astype(out_ref.dtype)
            col += width


def _gla_in(x, g, mod, w_in, w_a2, b_a):
    bsz, seq, d = x.shape
    kd = w_a2.shape[1]
    vd = (w_in.shape[1] - 2 * kd - GLA_RANK) // 2
    n_main = 2 * kd + 2 * vd
    tm = min(ROW_TILE, seq)
    w = jnp.pad(w_in, ((0, 0), (0, LANES - GLA_RANK))).astype(BF16)
    wa2 = jnp.pad(w_a2, ((0, LANES - GLA_RANK), (0, 0))).astype(BF16)
    dk = kd // GLA_HEADS
    kern = functools.partial(_gla_in_kernel, kd=kd, vd=vd, q_scale=dk ** -0.5)
    return pl.pallas_call(
        kern,
        grid=(bsz, seq // tm),
        in_specs=_in_specs_common(tm, d) + [
            _full((d, n_main + LANES)), _full((LANES, kd)), _full((1, kd))],
        out_specs=[_row_spec(tm, kd), _row_spec(tm, kd), _row_spec(tm, vd),
                   _row_spec(tm, vd), _row_spec(tm, kd)],
        out_shape=[jax.ShapeDtypeStruct((bsz, seq, kd), BF16),
                   jax.ShapeDtypeStruct((bsz, seq, kd), BF16),
                   jax.ShapeDtypeStruct((bsz, seq, vd), BF16),
                   jax.ShapeDtypeStruct((bsz, seq, vd), BF16),
                   jax.ShapeDtypeStruct((bsz, seq, kd), F32)],
        name="gla_in_proj",
    )(x, g.reshape(1, d), mod, mod, w, wa2, b_a.reshape(1, kd))


def _gla_tables(c):
    tri = np.tril(np.ones((c, c), np.float32))
    rows = [tri, 1.0 - tri]
    n_levels = int(math.log2(c))
    level = np.full((c, c), -1, np.int32)
    idx = np.arange(c)
    for l in range(n_levels):
        blk = c >> l
        half = blk // 2
        mid = (idx // blk) * blk + half - 1
        rows.append(tri - tri[mid])
        same = (idx[:, None] // blk) == (idx[None, :] // blk)
        upper = (idx[:, None] % blk) >= half
        lower = (idx[None, :] % blk) < half
        level[same & upper & lower] = l
    level[idx, idx] = n_levels
    return np.concatenate(rows, axis=0), level, n_levels


def _gla_scan_kernel(q_ref, k_ref, v_ref, gk_ref, wf_ref, ws_ref, lvl_ref, gh_ref, o_ref, st_ref,
                     *, fast_chunk, safe_chunk, n_levels, dk, dv):
    tb, kd = gk_ref.shape[1], gk_ref.shape[2]

    @pl.when(pl.program_id(1) == 0)
    def _():
        st_ref[...] = jnp.zeros_like(st_ref)

    def head_chunk(hd, rows, chunk, safe):
        kc = slice(hd * dk, (hd + 1) * dk)
        vc = slice(hd * dv, (hd + 1) * dv)
        w = ws_ref[...] if safe else wf_ref[...]
        r = _dot(w, jnp.concatenate(_split2(gk_ref[0, rows, kc]), axis=0))
        b = r[0:chunk]
        rev = r[chunk:2 * chunk]
        qf = q_ref[0, rows, kc].astype(F32)
        kf = k_ref[0, rows, kc].astype(F32)
        v = v_ref[0, rows, vc]
        st = st_ref[hd]
        q_dec = (qf * jnp.exp(b)).astype(BF16)

        if safe:
            lvl = lvl_ref[...]
            att = jnp.zeros((chunk, chunk), F32)
            for l in range(n_levels + 1):
                if l < n_levels:
                    e = jnp.exp(-jnp.abs(r[(2 + l) * chunk:(3 + l) * chunk]))
                    ql, kl = (qf * e).astype(BF16), (kf * e).astype(BF16)
                else:
                    ql, kl = qf.astype(BF16), kf.astype(BF16)
                att = jnp.where(lvl == l, _dot_nt(ql, kl), att)
        else:
            causal = (lax.broadcasted_iota(jnp.int32, (chunk, chunk), 0)
                      >= lax.broadcasted_iota(jnp.int32, (chunk, chunk), 1))
            att = jnp.where(causal, _dot_nt(q_dec, (kf * jnp.exp(-b)).astype(BF16)), 0.0)

        o = _dot_nt(q_dec, st.astype(BF16)) + _dot(att.astype(BF16), v)
        o_ref[0, rows, vc] = (_rms(o) * gh_ref[:, vc]).astype(o_ref.dtype)
        k_rev = (kf * jnp.exp(rev)).astype(BF16)
        st_ref[hd] = st * jnp.exp(b[chunk - 1:chunk, :]) + _dot_tn(v, k_rev)

    def run(chunk, safe):
        def body(ci, carry):
            rows = pl.ds(pl.multiple_of(ci * chunk, chunk), chunk)
            for hd in range(GLA_HEADS):
                head_chunk(hd, rows, chunk, safe)
            return carry
        lax.fori_loop(0, tb // chunk, body, 0, unroll=2)

    chunk_sums = jnp.sum(gk_ref[0].reshape(tb // fast_chunk, fast_chunk, kd), axis=1)
    fast_ok = jnp.min(chunk_sums) >= -GLA_DECAY_GUARD

    @pl.when(fast_ok)
    def _():
        run(fast_chunk, safe=False)

    @pl.when(jnp.logical_not(fast_ok))
    def _():
        run(safe_chunk, safe=True)


def _gla_scan(q, k, v, gk, g_head):
    bsz, seq, kd = q.shape
    vd = v.shape[-1]
    dk, dv = kd // GLA_HEADS, vd // GLA_HEADS
    tb = min(GLA_BLOCK, seq)
    ws_np, lvl_np, n_levels = _gla_tables(GLA_CHUNK)
    wf_np = _gla_tables(GLA_FAST_CHUNK)[0][:2 * GLA_FAST_CHUNK]
    stack2 = lambda t: jnp.asarray(np.concatenate([t, t], axis=1), BF16)
    wf, ws = stack2(wf_np), stack2(ws_np)
    kern = functools.partial(_gla_scan_kernel, fast_chunk=GLA_FAST_CHUNK, safe_chunk=GLA_CHUNK,
                             n_levels=n_levels, dk=dk, dv=dv)
    return pl.pallas_call(
        kern,
        grid=(bsz, seq // tb),
        in_specs=[_row_spec(tb, kd), _row_spec(tb, kd), _row_spec(tb, vd), _row_spec(tb, kd),
                  _full(wf.shape), _full(ws.shape), _full(lvl_np.shape), _full((1, vd))],
        out_specs=_row_spec(tb, vd),
        out_shape=jax.ShapeDtypeStruct((bsz, seq, vd), BF16),
        scratch_shapes=[pltpu.VMEM((GLA_HEADS, dv, dk), F32)],
        compiler_params=pltpu.CompilerParams(dimension_semantics=("arbitrary", "arbitrary")),
        name="gla_scan",
    )(q, k, v, gk, wf, ws, jnp.asarray(lvl_np), g_head.reshape(1, vd))


def _sgu_in_kernel(x_ref, g_ref, shift_ref, scale_ref, w_ref, lng_ref, lnb_ref,
                   u_ref, v_ref, z_ref, vtmp_ref, *, width):
    for rows in _subtiles(x_ref.shape[1]):
        h = _prenorm(x_ref, g_ref, shift_ref, scale_ref, rows)
        for c0 in range(0, width, COL_CHUNK):
            vtmp_ref[rows, c0:c0 + COL_CHUNK] = jax.nn.gelu(
                _dot(h, w_ref[:, width + c0:width + c0 + COL_CHUNK]))
        v = vtmp_ref[rows, :]
        mu = jnp.mean(v, axis=-1, keepdims=True)
        vc = v - mu
        var = jnp.mean(vc * vc, axis=-1, keepdims=True)
        v_ref[0, rows, :] = (vc * lax.rsqrt(var + EPS) * lng_ref[...] + lnb_ref[...]
                             ).astype(v_ref.dtype)
        for c0 in range(0, width, COL_CHUNK):
            cols = slice(c0, c0 + COL_CHUNK)
            u_ref[0, rows, cols] = jax.nn.gelu(_dot(h, w_ref[:, cols])).astype(u_ref.dtype)
            z_ref[0, rows, cols] = _dot(h, w_ref[:, 2 * width + c0:2 * width + c0 + COL_CHUNK]
                                        ).astype(z_ref.dtype)


def _sgu_in(x, g, mod, w_in, ln_g, ln_b):
    bsz, seq, d = x.shape
    width = w_in.shape[1] // 3
    tm = min(ROW_TILE, seq)
    kern = functools.partial(_sgu_in_kernel, width=width)
    act = jax.ShapeDtypeStruct((bsz, seq, width), BF16)
    return pl.pallas_call(
        kern,
        grid=(bsz, seq // tm),
        in_specs=_in_specs_common(tm, d) + [
            _full((d, 3 * width)), _full((1, width)), _full((1, width))],
        out_specs=[_row_spec(tm, width)] * 3,
        out_shape=[act, act, act],
        scratch_shapes=[pltpu.VMEM((tm, width), F32)],
        name="sgu_in_proj",
    )(x, g.reshape(1, d), mod, mod, w_in.astype(BF16), ln_g.reshape(1, width),
      ln_b.reshape(1, width))


def _sgu_mix_kernel(u_ref, v_ref, ws_ref, bs_ref, o_ref, *, chunk, gdim, n_chunks):
    for g in range(SGU_GROUPS):
        cols = slice(g * gdim, (g + 1) * gdim)
        w = ws_ref[g]
        bias = bs_ref[g]
        bias = jnp.concatenate([bias] * (gdim // LANES), axis=1)
        for c in range(n_chunks):
            rows = slice(c * chunk, (c + 1) * chunk)
            mixed = _dot(w, v_ref[0, rows, cols]) + bias
            o_ref[0, rows, cols] = (u_ref[0, rows, cols].astype(F32) * mixed).astype(o_ref.dtype)


def _sgu_mix(u, v, w_s, b_s):
    bsz, seq, width = u.shape
    gdim = width // SGU_GROUPS
    tm = min(ROW_TILE, seq)
    chunk = SGU_CHUNK
    causal = jnp.tril(jnp.ones((chunk, chunk), bool))
    w_causal = jnp.where(causal[None], w_s, 0.0).astype(BF16)
    bias = jnp.broadcast_to(b_s[:, :, None], (SGU_GROUPS, chunk, LANES))
    kern = functools.partial(_sgu_mix_kernel, chunk=chunk, gdim=gdim, n_chunks=tm // chunk)
    return pl.pallas_call(
        kern,
        grid=(bsz, seq // tm),
        in_specs=[_row_spec(tm, width), _row_spec(tm, width),
                  _full((SGU_GROUPS, chunk, chunk)), _full((SGU_GROUPS, chunk, LANES))],
        out_specs=_row_spec(tm, width),
        out_shape=jax.ShapeDtypeStruct((bsz, seq, width), BF16),
        name="sgu_mix",
    )(u, v, w_causal, bias)


def _fox_tables(n_heads, dh):
    width = n_heads * dh
    pq = np.zeros((FOX_PARTS * LANES, width), np.float32)
    pk = np.zeros((FOX_PARTS * LANES, width), np.float32)
    cq = np.zeros((1, width), np.float32)
    ck = np.zeros((1, width), np.float32)
    for h in range(n_heads):
        base = (h // 2) * 2 * dh + (dh if h % 2 == 0 else 0)
        for p in range(FOX_PARTS):
            pq[p * LANES + h, base + p] = 1.0
            cq[0, base + FOX_PARTS + p] = 1.0
            ck[0, base + p] = 1.0
            pk[p * LANES + h, base + FOX_PARTS + p] = -1.0
    return pq, pk, cq, ck


def _fox_in_kernel(x_ref, g_ref, shift_ref, scale_ref, w_ref, gq_ref, gk_ref, bf_ref, tri_ref,
                   pq_ref, pk_ref, cq_ref, ck_ref,
                   qt_ref, ka_ref, vt_ref, z_ref, carry_ref, *, ad, dh):
    @pl.when(pl.program_id(1) == 0)
    def _():
        carry_ref[...] = jnp.zeros_like(carry_ref)

    pair = 2 * dh
    tri = tri_ref[...]
    sub = tri.shape[0]
    low = lax.broadcasted_iota(jnp.int32, (sub, pair), 1) < dh

    def half_rms(x):
        sq = x * x
        s_lo = jnp.sum(jnp.where(low, sq, 0.0), axis=-1, keepdims=True)
        s_hi = jnp.sum(jnp.where(low, 0.0, sq), axis=-1, keepdims=True)
        return x * lax.rsqrt(jnp.where(low, s_lo, s_hi) * (1.0 / dh) + EPS)

    for rows in _subtiles(x_ref.shape[1]):
        h = _prenorm(x_ref, g_ref, shift_ref, scale_ref, rows)

        f_logit = _dot(h, w_ref[:, 4 * ad:4 * ad + LANES])
        lf_hi, lf_mid, lf_lo = _split3(_log_sigmoid(f_logit + bf_ref[...]))
        f_cum = _dot(tri, lf_hi) + _dot(tri, lf_mid) + _dot(tri, lf_lo) + carry_ref[...]
        carry_ref[...] = f_cum[sub - 1:sub, :]
        f3 = jnp.concatenate(_split3(f_cum * LOG2E), axis=1)
        aug_q = _dot(f3, pq_ref[...]) + cq_ref[...]
        aug_k = _dot(f3, pk_ref[...]) + ck_ref[...]

        for c0 in range(0, ad, COL_CHUNK):
            acc_q = _dot(h, w_ref[:, c0:c0 + COL_CHUNK])
            acc_k = _dot(h, w_ref[:, ad + c0:ad + c0 + COL_CHUNK])
            acc_v = _dot(h, w_ref[:, 2 * ad + c0:2 * ad + c0 + COL_CHUNK])
            z_ref[0, rows, c0:c0 + COL_CHUNK] = _dot(
                h, w_ref[:, 3 * ad + c0:3 * ad + c0 + COL_CHUNK]).astype(z_ref.dtype)
            for g0 in range(0, COL_CHUNK, pair):
                cols = slice(g0, g0 + pair)
                gcols = slice(c0 + g0, c0 + g0 + pair)
                head = (c0 + g0) // dh
                qn = half_rms(acc_q[:, cols]) * gq_ref[...]
                kn = half_rms(acc_k[:, cols]) * gk_ref[...]
                aq, ak = aug_q[:, gcols], aug_k[:, gcols]
                qt_ref[0, head, :, rows] = jnp.where(low, qn, aq).T.astype(qt_ref.dtype)
                qt_ref[0, head + 1, :, rows] = jnp.where(low, aq, qn).T.astype(qt_ref.dtype)
                ka_ref[0, head, rows, :] = jnp.where(low, kn, ak).astype(ka_ref.dtype)
                ka_ref[0, head + 1, rows, :] = jnp.where(low, ak, kn).astype(ka_ref.dtype)
                v_t = acc_v[:, cols].T.astype(vt_ref.dtype)
                vt_ref[0, head, :, rows] = v_t[:dh]
                vt_ref[0, head + 1, :, rows] = v_t[dh:]


def _fox_in(x, g, mod, w_in, b_f, g_q, g_k):
    bsz, seq, d = x.shape
    nh, dh = FOX_HEADS, FOX_DH
    ad = nh * dh
    tm = min(ROW_TILE, seq)
    w = jnp.pad(w_in, ((0, 0), (0, LANES - nh))).astype(BF16)
    bf = jnp.pad(b_f, (0, LANES - nh)).reshape(1, LANES)
    gq2 = jnp.tile(g_q * (dh ** -0.5 * LOG2E), 2).reshape(1, 2 * dh)
    gk2 = jnp.tile(g_k, 2).reshape(1, 2 * dh)
    sub = min(ROW_SUBTILE, tm)
    tri = jnp.asarray(np.tril(np.ones((sub, sub), np.float32)), BF16)
    pq, pk, cq, ck = _fox_tables(nh, dh)
    kern = functools.partial(_fox_in_kernel, ad=ad, dh=dh)
    feat_major = jax.ShapeDtypeStruct((bsz, nh, LANES, seq), BF16)
    return pl.pallas_call(
        kern,
        grid=(bsz, seq // tm),
        in_specs=_in_specs_common(tm, d) + [
            _full((d, 4 * ad + LANES)), _full((1, 2 * dh)), _full((1, 2 * dh)), _full((1, LANES)),
            _full(tri.shape), _full(pq.shape), _full(pk.shape), _full(cq.shape), _full(ck.shape)],
        out_specs=[
            pl.BlockSpec((1, nh, LANES, tm), lambda b, s: (b, 0, 0, s)),
            pl.BlockSpec((1, nh, tm, LANES), lambda b, s: (b, 0, s, 0)),
            pl.BlockSpec((1, nh, dh, tm), lambda b, s: (b, 0, 0, s)),
            _row_spec(tm, ad)],
        out_shape=[feat_major, jax.ShapeDtypeStruct((bsz, nh, seq, LANES), BF16),
                   jax.ShapeDtypeStruct((bsz, nh, dh, seq), BF16),
                   jax.ShapeDtypeStruct((bsz, seq, ad), BF16)],
        scratch_shapes=[pltpu.VMEM((1, LANES), F32)],
        compiler_params=pltpu.CompilerParams(dimension_semantics=("arbitrary", "arbitrary")),
        name="fox_in_proj",
    )(x, g.reshape(1, d), mod, mod, w, gq2, gk2, bf, tri,
      jnp.asarray(pq, BF16), jnp.asarray(pk, BF16), jnp.asarray(cq), jnp.asarray(ck))


def _fox_attn_kernel(qt_ref, ka_ref, vt_ref, o_ref, s0_ref, s1_ref, m_ref, acc_ref,
                     *, tq, tk, dh):
    qi = pl.program_id(2)
    hp = FOX_HEADS_PER_STEP
    heads = range(hp)
    sum_rows = jnp.ones((FOX_SUM_ROWS, tk), BF16)

    def scores(hh, kv):
        k0 = pl.multiple_of(kv * tk, tk)
        return _dot(ka_ref[0, hh, pl.ds(k0, tk), :], qt_ref[0, hh])

    def consume(hh, s_ref, kv, masked):
        k0 = pl.multiple_of(kv * tk, tk)
        if masked:
            visible = (lax.broadcasted_iota(jnp.int32, (tk, tq), 0)
                       <= lax.broadcasted_iota(jnp.int32, (tk, tq), 1))
            read = lambda: jnp.where(visible, s_ref[hh], -jnp.inf)
        else:
            read = lambda: s_ref[hh]
        m = m_ref[hh]
        m_new = jnp.maximum(m, jnp.max(read(), axis=0, keepdims=True))
        p = jnp.exp2(read() - m_new).astype(BF16)
        v_sum = jnp.concatenate([vt_ref[0, hh, :, pl.ds(k0, tk)], sum_rows], axis=0)
        acc_ref[hh] = acc_ref[hh] * jnp.exp2(m - m_new) + _dot(v_sum, p)
        m_ref[hh] = m_new

    m_ref[...] = jnp.full(m_ref.shape, -jnp.inf, F32)
    acc_ref[...] = jnp.zeros(acc_ref.shape, F32)
    for hh in heads:
        s0_ref[hh] = scores(hh, 0)

    def two_blocks(i, carry):
        j = 2 * i
        for hh in heads:
            s1_ref[hh] = scores(hh, j + 1)
            consume(hh, s0_ref, j, masked=False)
        for hh in heads:
            s0_ref[hh] = scores(hh, j + 2)
            consume(hh, s1_ref, j + 1, masked=False)
        return carry

    lax.fori_loop(0, qi // 2, two_blocks, 0)

    @pl.when(qi % 2 == 1)
    def _():
        for hh in heads:
            s1_ref[hh] = scores(hh, qi)
            consume(hh, s0_ref, qi - 1, masked=False)
        for hh in heads:
            consume(hh, s1_ref, qi, masked=True)

    @pl.when(qi % 2 == 0)
    def _():
        for hh in heads:
            consume(hh, s0_ref, qi, masked=True)

    out_t = jnp.concatenate([acc_ref[hh, :dh] / acc_ref[hh, dh:dh + 1] for hh in heads], axis=0)
    o_ref[0] = out_t.T.astype(o_ref.dtype)


def _fox_attn(qt, ka, vt):
    bsz, nh, _, seq = qt.shape
    dh = FOX_DH
    tq, tk = min(FOX_TQ, seq), min(FOX_TK, seq)
    hp = FOX_HEADS_PER_STEP
    assert tq == tk and (hp * dh) % LANES == 0
    kern = functools.partial(_fox_attn_kernel, tq=tq, tk=tk, dh=dh)
    return pl.pallas_call(
        kern,
        grid=(bsz, nh // hp, seq // tq),
        in_specs=[
            pl.BlockSpec((1, hp, LANES, tq), lambda b, h, s: (b, h, 0, s)),
            pl.BlockSpec((1, hp, seq, LANES), lambda b, h, s: (b, h, 0, 0)),
            pl.BlockSpec((1, hp, dh, seq), lambda b, h, s: (b, h, 0, 0)),
        ],
        out_specs=pl.BlockSpec((1, tq, hp * dh), lambda b, h, s: (b, s, h)),
        out_shape=jax.ShapeDtypeStruct((bsz, seq, nh * dh), BF16),
        scratch_shapes=[pltpu.VMEM((hp, tk, tq), F32), pltpu.VMEM((hp, tk, tq), F32),
                        pltpu.VMEM((hp, 1, tq), F32),
                        pltpu.VMEM((hp, dh + FOX_SUM_ROWS, tq), F32)],
        name="fox_attn",
    )(qt, ka, vt)


def _out_kernel(o_ref, z_ref, w_ref, g_ref, gate_ref, x_ref, y_ref):
    gated = o_ref[0].astype(F32) * jax.nn.silu(z_ref[0].astype(F32))
    y = _dot(gated.astype(BF16), w_ref[...])
    y_ref[0] = x_ref[0] + gate_ref[0] * (_rms(y) * g_ref[...])


def _out_proj(o, z, w_out, g, mod, x):
    bsz, seq, d = x.shape
    width = o.shape[-1]
    tm = min(ROW_TILE, seq)
    return pl.pallas_call(
        _out_kernel,
        grid=(bsz, seq // tm),
        in_specs=[_row_spec(tm, width), _row_spec(tm, width), _full((width, d)), _full((1, d)),
                  pl.BlockSpec((1, 1, d), lambda b, s: (b, 0, 2)), _row_spec(tm, d)],
        out_specs=_row_spec(tm, d),
        out_shape=jax.ShapeDtypeStruct((bsz, seq, d), F32),
        name="out_proj",
    )(o, z, w_out.astype(BF16), g.reshape(1, d), mod, x)


def _gla_layer(x, g_pre, mod, w_in, w_a2, b_a, g_head):
    q, k, v, z, gk = _gla_in(x, g_pre, mod, w_in, w_a2, b_a)
    return _gla_scan(q, k, v, gk, g_head), z


def _sgu_layer(x, g_pre, mod, w_in, ln_g, ln_b, w_s, b_s):
    u, v, z = _sgu_in(x, g_pre, mod, w_in, ln_g, ln_b)
    return _sgu_mix(u, v, w_s, b_s), z


def _fox_layer(x, g_pre, mod, w_in, b_f, g_q, g_k):
    qt, ka, vt, z = _fox_in(x, g_pre, mod, w_in, b_f, g_q, g_k)
    return _fox_attn(qt, ka, vt), z


def kernel(x, c, norm_pre_g, norm_post_g, w_mod, b_mod, gla_w_in, gla_w_a2, gla_b_a, gla_g_head, gla_w_out, sgu_w_in, sgu_ln_g, sgu_ln_b, sgu_w_s, sgu_b_s, sgu_w_out, fox_w_in, fox_b_f, fox_g_q, fox_g_k, fox_w_out):
    depth = w_mod.shape[0]
    bsz = x.shape[0]
    mod_all = _modulation(c, w_mod, b_mod)
    for i in range(depth):
        mod = mod_all[i].reshape(bsz, 1, -1)
        kind, j = i % N_MIXERS, i // N_MIXERS
        if kind == 0:
            o, z = _gla_layer(x, norm_pre_g[i], mod, gla_w_in[j], gla_w_a2[j], gla_b_a[j],
                              gla_g_head[j])
            w_out = gla_w_out[j]
        elif kind == 1:
            o, z = _sgu_layer(x, norm_pre_g[i], mod, sgu_w_in[j], sgu_ln_g[j], sgu_ln_b[j],
                              sgu_w_s[j], sgu_b_s[j])
            w_out = sgu_w_out[j]
        else:
            o, z = _fox_layer(x, norm_pre_g[i], mod, fox_w_in[j], fox_b_f[j], fox_g_q[j],
                              fox_g_k[j])
            w_out = fox_w_out[j]
        x = _out_proj(o, z, w_out, norm_post_g[i], mod, x)
    return x
```

```python
import functools
import math

import numpy as np
import jax
import jax.numpy as jnp
from jax import lax
from jax.experimental import pallas as pl
from jax.experimental.pallas import tpu as pltpu

EPS = 1e-6
N_MIXERS = 3

LANES = 128
ROW_TILE = 512
COL_CHUNK = 512

GLA_HEADS = 4
GLA_RANK = 16
GLA_NORMALIZER = 16.0
GLA_CHUNK = 64
GLA_FAST_CHUNK = 256
GLA_DECAY_GUARD = 70.0
GLA_BLOCK = 512

SGU_GROUPS = 4
SGU_CHUNK = 128

FOX_HEADS = 16
FOX_DH = 64
FOX_TQ = 512
FOX_TK = 512
FOX_HEADS_PER_STEP = 4
FOX_SUM_ROWS = 16
FOX_PARTS = 3
LOG2E = math.log2(math.e)

BF16 = jnp.bfloat16
F32 = jnp.float32


def _dot(a, b):
    return jnp.dot(a, b, preferred_element_type=F32)


def _dot_nt(a, b):
    return lax.dot_general(a, b, (((1,), (1,)), ((), ())), preferred_element_type=F32)


def _dot_tn(a, b):
    return lax.dot_general(a, b, (((0,), (0,)), ((), ())), preferred_element_type=F32)


def _split2(x):
    hi = x.astype(BF16)
    lo = (x - hi.astype(F32)).astype(BF16)
    return hi, lo


def _split3(x):
    hi = x.astype(BF16)
    r = x - hi.astype(F32)
    mid = r.astype(BF16)
    lo = (r - mid.astype(F32)).astype(BF16)
    return hi, mid, lo


def _log_sigmoid(x):
    return jnp.minimum(x, 0.0) - jnp.log1p(jnp.exp(-jnp.abs(x)))


def _rms(x):
    return x * lax.rsqrt(jnp.mean(x * x, axis=-1, keepdims=True) + EPS)


def _mod_kernel(c_ref, w_ref, b_ref, o_ref):
    cond = jax.nn.silu(c_ref[...])
    o_ref[0] = _dot(cond, w_ref[0]) + b_ref[0]


def _modulation(c, w_mod, b_mod):
    depth, d, d3 = w_mod.shape
    bsz = c.shape[0]
    nblk = d3 // d
    return pl.pallas_call(
        _mod_kernel,
        grid=(depth, nblk),
        in_specs=[
            pl.BlockSpec((bsz, d), lambda i, j: (0, 0)),
            pl.BlockSpec((1, d, d), lambda i, j: (i, 0, j)),
            pl.BlockSpec((1, 1, d), lambda i, j: (i, 0, j)),
        ],
        out_specs=pl.BlockSpec((1, bsz, d), lambda i, j: (i, 0, j)),
        out_shape=jax.ShapeDtypeStruct((depth, bsz, d3), F32),
        name="adaln_mod",
    )(c, w_mod, b_mod.reshape(depth, 1, d3))


N_HEAD_REFS = 4
N_TAIL_REFS = 5


def _layer_tail(x, o_ref, z_ref, w_ref, g_ref, gate_ref):
    gated = o_ref[0].astype(F32) * jax.nn.silu(z_ref[0].astype(F32))
    y = _dot(gated.astype(BF16), w_ref[...])
    return x + gate_ref[0] * (_rms(y) * g_ref[...])


def _layer_input(refs, has_tail, n_scratch=0):
    x_ref, g_ref, shift_ref, scale_ref = refs[:N_HEAD_REFS]
    rest = list(refs[N_HEAD_REFS:])
    x = x_ref[0]
    if has_tail:
        x = _layer_tail(x, *rest[:N_TAIL_REFS])
        rest.pop(len(rest) - n_scratch - 1)[0] = x
        rest = rest[N_TAIL_REFS:]
    h = _rms(x) * g_ref[...]
    h = h * (1.0 + scale_ref[0]) + shift_ref[0]
    return h.astype(BF16), rest


def _full(shape):
    nd = len(shape)
    return pl.BlockSpec(shape, lambda b, s: (0,) * nd, pipeline_mode=pl.Buffered(1))


def _row_spec(tm, width, col=0):
    return pl.BlockSpec((1, tm, width), lambda b, s: (b, s, col))


def _mod_spec(d, part):
    return pl.BlockSpec((1, 1, d), lambda b, s: (b, 0, part))


def _in_proj_call(kern, x, g, mod, tail, consts, out_specs, out_shape, name, scratch=()):
    bsz, seq, d = x.shape
    tm = min(ROW_TILE, seq)
    operands = [x, g.reshape(1, d), mod, mod]
    in_specs = [_row_spec(tm, d), _full((1, d)), _mod_spec(d, 0), _mod_spec(d, 1)]
    if tail is not None:
        o, z, w_out, g_post, mod_prev = tail
        width = o.shape[-1]
        operands += [o, z, w_out.astype(BF16), g_post.reshape(1, d), mod_prev]
        in_specs += [_row_spec(tm, width), _row_spec(tm, width), _full((width, d)),
                     _full((1, d)), _mod_spec(d, 2)]
        out_specs = list(out_specs) + [_row_spec(tm, d)]
        out_shape = list(out_shape) + [jax.ShapeDtypeStruct((bsz, seq, d), F32)]
    operands += list(consts)
    in_specs += [_full(c.shape) for c in consts]
    return pl.pallas_call(
        functools.partial(kern, has_tail=tail is not None),
        grid=(bsz, seq // tm),
        in_specs=in_specs, out_specs=out_specs, out_shape=out_shape,
        scratch_shapes=list(scratch),
        compiler_params=pltpu.CompilerParams(dimension_semantics=("arbitrary", "arbitrary")),
        name=name,
    )(*operands)


def _gla_in_kernel(*refs, has_tail, kd, vd, q_scale):
    h, (w_ref, wa2_ref, ba_ref, q_ref, k_ref, v_ref, z_ref, gk_ref) = _layer_input(refs, has_tail)
    n_main = 2 * kd + 2 * vd
    a_low = _dot(h, w_ref[:, n_main:n_main + LANES])
    logit = _dot(a_low.astype(BF16), wa2_ref[...]) + ba_ref[...]
    gk_ref[0] = _log_sigmoid(logit) * (1.0 / GLA_NORMALIZER)
    col = 0
    for out_ref, width, mul in ((q_ref, kd, q_scale), (k_ref, kd, None),
                                (v_ref, vd, None), (z_ref, vd, None)):
        for c0 in range(0, width, COL_CHUNK):
            acc = _dot(h, w_ref[:, col + c0:col + c0 + COL_CHUNK])
            if mul is not None:
                acc = acc * mul
            out_ref[0, :, c0:c0 + COL_CHUNK] = acc.astype(out_ref.dtype)
        col += width


def _gla_in(x, g, mod, tail, w_in, w_a2, b_a):
    bsz, seq, d = x.shape
    kd = w_a2.shape[1]
    vd = (w_in.shape[1] - 2 * kd - GLA_RANK) // 2
    tm = min(ROW_TILE, seq)
    w = jnp.pad(w_in, ((0, 0), (0, LANES - GLA_RANK))).astype(BF16)
    wa2 = jnp.pad(w_a2, ((0, LANES - GLA_RANK), (0, 0))).astype(BF16)
    dk = kd // GLA_HEADS
    kern = functools.partial(_gla_in_kernel, kd=kd, vd=vd, q_scale=dk ** -0.5)
    act = lambda width, dtype: jax.ShapeDtypeStruct((bsz, seq, width), dtype)
    return _in_proj_call(
        kern, x, g, mod, tail, [w, wa2, b_a.reshape(1, kd)],
        out_specs=[_row_spec(tm, kd), _row_spec(tm, kd), _row_spec(tm, vd),
                   _row_spec(tm, vd), _row_spec(tm, kd)],
        out_shape=[act(kd, BF16), act(kd, BF16), act(vd, BF16), act(vd, BF16), act(kd, F32)],
        name="gla_in_proj")


def _gla_tables(c):
    tri = np.tril(np.ones((c, c), np.float32))
    rows = [tri, 1.0 - tri]
    n_levels = int(math.log2(c))
    level = np.full((c, c), -1, np.int32)
    idx = np.arange(c)
    for l in range(n_levels):
        blk = c >> l
        half = blk // 2
        mid = (idx // blk) * blk + half - 1
        rows.append(tri - tri[mid])
        same = (idx[:, None] // blk) == (idx[None, :] // blk)
        upper = (idx[:, None] % blk) >= half
        lower = (idx[None, :] % blk) < half
        level[same & upper & lower] = l
    level[idx, idx] = n_levels
    return np.concatenate(rows, axis=0), level, n_levels


def _gla_scan_kernel(q_ref, k_ref, v_ref, gk_ref, wf_ref, ws_ref, lvl_ref, gh_ref, o_ref, st_ref,
                     *, fast_chunk, safe_chunk, n_levels, dk, dv):
    tb, kd = gk_ref.shape[1], gk_ref.shape[2]

    @pl.when(pl.program_id(1) == 0)
    def _():
        st_ref[...] = jnp.zeros_like(st_ref)

    def head_chunk(hd, rows, chunk, safe):
        kc = slice(hd * dk, (hd + 1) * dk)
        vc = slice(hd * dv, (hd + 1) * dv)
        w = ws_ref[...] if safe else wf_ref[...]
        r = _dot(w, jnp.concatenate(_split2(gk_ref[0, rows, kc]), axis=0))
        b = r[0:chunk]
        rev = r[chunk:2 * chunk]
        qf = q_ref[0, rows, kc].astype(F32)
        kf = k_ref[0, rows, kc].astype(F32)
        v = v_ref[0, rows, vc]
        st = st_ref[hd]
        q_dec = (qf * jnp.exp(b)).astype(BF16)

        if safe:
            lvl = lvl_ref[...]
            att = jnp.zeros((chunk, chunk), F32)
            for l in range(n_levels + 1):
                if l < n_levels:
                    e = jnp.exp(-jnp.abs(r[(2 + l) * chunk:(3 + l) * chunk]))
                    ql, kl = (qf * e).astype(BF16), (kf * e).astype(BF16)
                else:
                    ql, kl = qf.astype(BF16), kf.astype(BF16)
                att = jnp.where(lvl == l, _dot_nt(ql, kl), att)
        else:
            causal = (lax.broadcasted_iota(jnp.int32, (chunk, chunk), 0)
                      >= lax.broadcasted_iota(jnp.int32, (chunk, chunk), 1))
            att = jnp.where(causal, _dot_nt(q_dec, (kf * jnp.exp(-b)).astype(BF16)), 0.0)

        o = _dot_nt(q_dec, st.astype(BF16)) + _dot(att.astype(BF16), v)
        o_ref[0, rows, vc] = (_rms(o) * gh_ref[:, vc]).astype(o_ref.dtype)
        k_rev = (kf * jnp.exp(rev)).astype(BF16)
        st_ref[hd] = st * jnp.exp(b[chunk - 1:chunk, :]) + _dot_tn(v, k_rev)

    def run(chunk, safe):
        def body(ci, carry):
            rows = pl.ds(pl.multiple_of(ci * chunk, chunk), chunk)
            for hd in range(GLA_HEADS):
                head_chunk(hd, rows, chunk, safe)
            return carry
        lax.fori_loop(0, tb // chunk, body, 0, unroll=2)

    chunk_sums = jnp.sum(gk_ref[0].reshape(tb // fast_chunk, fast_chunk, kd), axis=1)
    fast_ok = jnp.min(chunk_sums) >= -GLA_DECAY_GUARD

    @pl.when(fast_ok)
    def _():
        run(fast_chunk, safe=False)

    @pl.when(jnp.logical_not(fast_ok))
    def _():
        run(safe_chunk, safe=True)


def _gla_scan(q, k, v, gk, g_head):
    bsz, seq, kd = q.shape
    vd = v.shape[-1]
    dk, dv = kd // GLA_HEADS, vd // GLA_HEADS
    tb = min(GLA_BLOCK, seq)
    ws_np, lvl_np, n_levels = _gla_tables(GLA_CHUNK)
    wf_np = _gla_tables(GLA_FAST_CHUNK)[0][:2 * GLA_FAST_CHUNK]
    stack2 = lambda t: jnp.asarray(np.concatenate([t, t], axis=1), BF16)
    wf, ws = stack2(wf_np), stack2(ws_np)
    kern = functools.partial(_gla_scan_kernel, fast_chunk=GLA_FAST_CHUNK, safe_chunk=GLA_CHUNK,
                             n_levels=n_levels, dk=dk, dv=dv)
    return pl.pallas_call(
        kern,
        grid=(bsz, seq // tb),
        in_specs=[_row_spec(tb, kd), _row_spec(tb, kd), _row_spec(tb, vd), _row_spec(tb, kd),
                  _full(wf.shape), _full(ws.shape), _full(lvl_np.shape), _full((1, vd))],
        out_specs=_row_spec(tb, vd),
        out_shape=jax.ShapeDtypeStruct((bsz, seq, vd), BF16),
        scratch_shapes=[pltpu.VMEM((GLA_HEADS, dv, dk), F32)],
        compiler_params=pltpu.CompilerParams(dimension_semantics=("arbitrary", "arbitrary")),
        name="gla_scan",
    )(q, k, v, gk, wf, ws, jnp.asarray(lvl_np), g_head.reshape(1, vd))


def _sgu_in_kernel(*refs, has_tail, width):
    h, (w_ref, lng_ref, lnb_ref, u_ref, v_ref, z_ref, vtmp_ref) = _layer_input(
        refs, has_tail, n_scratch=1)
    for c0 in range(0, width, COL_CHUNK):
        vtmp_ref[:, c0:c0 + COL_CHUNK] = jax.nn.gelu(
            _dot(h, w_ref[:, width + c0:width + c0 + COL_CHUNK]))
    v = vtmp_ref[...]
    mu = jnp.mean(v, axis=-1, keepdims=True)
    vc = v - mu
    var = jnp.mean(vc * vc, axis=-1, keepdims=True)
    v_ref[0] = (vc * lax.rsqrt(var + EPS) * lng_ref[...] + lnb_ref[...]).astype(v_ref.dtype)
    for c0 in range(0, width, COL_CHUNK):
        cols = slice(c0, c0 + COL_CHUNK)
        u_ref[0, :, cols] = jax.nn.gelu(_dot(h, w_ref[:, cols])).astype(u_ref.dtype)
        z_ref[0, :, cols] = _dot(h, w_ref[:, 2 * width + c0:2 * width + c0 + COL_CHUNK]
                                 ).astype(z_ref.dtype)


def _sgu_in(x, g, mod, tail, w_in, ln_g, ln_b):
    bsz, seq, d = x.shape
    width = w_in.shape[1] // 3
    tm = min(ROW_TILE, seq)
    kern = functools.partial(_sgu_in_kernel, width=width)
    act = jax.ShapeDtypeStruct((bsz, seq, width), BF16)
    return _in_proj_call(
        kern, x, g, mod, tail,
        [w_in.astype(BF16), ln_g.reshape(1, width), ln_b.reshape(1, width)],
        out_specs=[_row_spec(tm, width)] * 3, out_shape=[act, act, act],
        name="sgu_in_proj", scratch=[pltpu.VMEM((tm, width), F32)])


def _sgu_mix_kernel(u_ref, v_ref, ws_ref, bs_ref, o_ref, *, chunk, gdim, n_chunks):
    for g in range(SGU_GROUPS):
        cols = slice(g * gdim, (g + 1) * gdim)
        w = ws_ref[g]
        bias = bs_ref[g]
        bias = jnp.concatenate([bias] * (gdim // LANES), axis=1)
        for c in range(n_chunks):
            rows = slice(c * chunk, (c + 1) * chunk)
            mixed = _dot(w, v_ref[0, rows, cols]) + bias
            o_ref[0, rows, cols] = (u_ref[0, rows, cols].astype(F32) * mixed).astype(o_ref.dtype)


def _sgu_mix(u, v, w_s, b_s):
    bsz, seq, width = u.shape
    gdim = width // SGU_GROUPS
    tm = min(ROW_TILE, seq)
    chunk = SGU_CHUNK
    causal = jnp.tril(jnp.ones((chunk, chunk), bool))
    w_causal = jnp.where(causal[None], w_s, 0.0).astype(BF16)
    bias = jnp.broadcast_to(b_s[:, :, None], (SGU_GROUPS, chunk, LANES))
    kern = functools.partial(_sgu_mix_kernel, chunk=chunk, gdim=gdim, n_chunks=tm // chunk)
    return pl.pallas_call(
        kern,
        grid=(bsz, seq // tm),
        in_specs=[_row_spec(tm, width), _row_spec(tm, width),
                  _full((SGU_GROUPS, chunk, chunk)), _full((SGU_GROUPS, chunk, LANES))],
        out_specs=_row_spec(tm, width),
        out_shape=jax.ShapeDtypeStruct((bsz, seq, width), BF16),
        name="sgu_mix",
    )(u, v, w_causal, bias)


def _fox_tables(n_heads, dh):
    width = n_heads * dh
    pq = np.zeros((FOX_PARTS * LANES, width), np.float32)
    pk = np.zeros((FOX_PARTS * LANES, width), np.float32)
    cq = np.zeros((1, width), np.float32)
    ck = np.zeros((1, width), np.float32)
    for h in range(n_heads):
        base = (h // 2) * 2 * dh + (dh if h % 2 == 0 else 0)
        for p in range(FOX_PARTS):
            pq[p * LANES + h, base + p] = 1.0
            cq[0, base + FOX_PARTS + p] = 1.0
            ck[0, base + p] = 1.0
            pk[p * LANES + h, base + FOX_PARTS + p] = -1.0
    return pq, pk, cq, ck


def _fox_in_kernel(*refs, has_tail, ad, dh):
    h, (w_ref, gq_ref, gk_ref, bf_ref, tri_ref, pq_ref, pk_ref, cq_ref, ck_ref,
        qt_ref, ka_ref, vt_ref, z_ref, carry_ref) = _layer_input(refs, has_tail, n_scratch=1)

    @pl.when(pl.program_id(1) == 0)
    def _():
        carry_ref[...] = jnp.zeros_like(carry_ref)

    tm = h.shape[0]
    pair = 2 * dh
    low = lax.broadcasted_iota(jnp.int32, (tm, pair), 1) < dh

    def half_rms(x):
        sq = x * x
        s_lo = jnp.sum(jnp.where(low, sq, 0.0), axis=-1, keepdims=True)
        s_hi = jnp.sum(jnp.where(low, 0.0, sq), axis=-1, keepdims=True)
        return x * lax.rsqrt(jnp.where(low, s_lo, s_hi) * (1.0 / dh) + EPS)

    f_logit = _dot(h, w_ref[:, 4 * ad:4 * ad + LANES])
    tri = tri_ref[...]
    lf_hi, lf_mid, lf_lo = _split3(_log_sigmoid(f_logit + bf_ref[...]))
    f_cum = _dot(tri, lf_hi) + _dot(tri, lf_mid) + _dot(tri, lf_lo) + carry_ref[...]
    carry_ref[...] = f_cum[tm - 1:tm, :]
    f3 = jnp.concatenate(_split3(f_cum * LOG2E), axis=1)
    aug_q = _dot(f3, pq_ref[...]) + cq_ref[...]
    aug_k = _dot(f3, pk_ref[...]) + ck_ref[...]

    for c0 in range(0, ad, COL_CHUNK):
        acc_q = _dot(h, w_ref[:, c0:c0 + COL_CHUNK])
        acc_k = _dot(h, w_ref[:, ad + c0:ad + c0 + COL_CHUNK])
        acc_v = _dot(h, w_ref[:, 2 * ad + c0:2 * ad + c0 + COL_CHUNK])
        z_ref[0, :, c0:c0 + COL_CHUNK] = _dot(
            h, w_ref[:, 3 * ad + c0:3 * ad + c0 + COL_CHUNK]).astype(z_ref.dtype)
        for g0 in range(0, COL_CHUNK, pair):
            cols = slice(g0, g0 + pair)
            gcols = slice(c0 + g0, c0 + g0 + pair)
            head = (c0 + g0) // dh
            qn = half_rms(acc_q[:, cols]) * gq_ref[...]
            kn = half_rms(acc_k[:, cols]) * gk_ref[...]
            aq, ak = aug_q[:, gcols], aug_k[:, gcols]
            qt_ref[0, head] = jnp.where(low, qn, aq).T.astype(qt_ref.dtype)
            qt_ref[0, head + 1] = jnp.where(low, aq, qn).T.astype(qt_ref.dtype)
            ka_ref[0, head] = jnp.where(low, kn, ak).astype(ka_ref.dtype)
            ka_ref[0, head + 1] = jnp.where(low, ak, kn).astype(ka_ref.dtype)
            v_t = acc_v[:, cols].T.astype(vt_ref.dtype)
            vt_ref[0, head] = v_t[:dh]
            vt_ref[0, head + 1] = v_t[dh:]


def _fox_in(x, g, mod, tail, w_in, b_f, g_q, g_k):
    bsz, seq, d = x.shape
    nh, dh = FOX_HEADS, FOX_DH
    ad = nh * dh
    tm = min(ROW_TILE, seq)
    w = jnp.pad(w_in, ((0, 0), (0, LANES - nh))).astype(BF16)
    bf = jnp.pad(b_f, (0, LANES - nh)).reshape(1, LANES)
    gq2 = jnp.tile(g_q * (dh ** -0.5 * LOG2E), 2).reshape(1, 2 * dh)
    gk2 = jnp.tile(g_k, 2).reshape(1, 2 * dh)
    tri = jnp.asarray(np.tril(np.ones((tm, tm), np.float32)), BF16)
    pq, pk, cq, ck = _fox_tables(nh, dh)
    kern = functools.partial(_fox_in_kernel, ad=ad, dh=dh)
    return _in_proj_call(
        kern, x, g, mod, tail,
        [w, gq2, gk2, bf, tri, jnp.asarray(pq, BF16), jnp.asarray(pk, BF16), jnp.asarray(cq),
         jnp.asarray(ck)],
        out_specs=[
            pl.BlockSpec((1, nh, LANES, tm), lambda b, s: (b, 0, 0, s)),
            pl.BlockSpec((1, nh, tm, LANES), lambda b, s: (b, 0, s, 0)),
            pl.BlockSpec((1, nh, dh, tm), lambda b, s: (b, 0, 0, s)),
            _row_spec(tm, ad)],
        out_shape=[jax.ShapeDtypeStruct((bsz, nh, LANES, seq), BF16),
                   jax.ShapeDtypeStruct((bsz, nh, seq, LANES), BF16),
                   jax.ShapeDtypeStruct((bsz, nh, dh, seq), BF16),
                   jax.ShapeDtypeStruct((bsz, seq, ad), BF16)],
        name="fox_in_proj", scratch=[pltpu.VMEM((1, LANES), F32)])


def _fox_attn_kernel(qt_ref, ka_ref, vt_ref, o_ref, s0_ref, s1_ref, m_ref, acc_ref,
                     *, tq, tk, dh):
    qi = pl.program_id(2)
    hp = FOX_HEADS_PER_STEP
    heads = range(hp)
    sum_rows = jnp.ones((FOX_SUM_ROWS, tk), BF16)

    def scores(hh, kv):
        k0 = pl.multiple_of(kv * tk, tk)
        return _dot(ka_ref[0, hh, pl.ds(k0, tk), :], qt_ref[0, hh])

    def consume(hh, s_ref, kv, masked):
        k0 = pl.multiple_of(kv * tk, tk)
        if masked:
            visible = (lax.broadcasted_iota(jnp.int32, (tk, tq), 0)
                       <= lax.broadcasted_iota(jnp.int32, (tk, tq), 1))
            read = lambda: jnp.where(visible, s_ref[hh], -jnp.inf)
        else:
            read = lambda: s_ref[hh]
        m = m_ref[hh]
        m_new = jnp.maximum(m, jnp.max(read(), axis=0, keepdims=True))
        p = jnp.exp2(read() - m_new).astype(BF16)
        v_sum = jnp.concatenate([vt_ref[0, hh, :, pl.ds(k0, tk)], sum_rows], axis=0)
        acc_ref[hh] = acc_ref[hh] * jnp.exp2(m - m_new) + _dot(v_sum, p)
        m_ref[hh] = m_new

    m_ref[...] = jnp.full(m_ref.shape, -jnp.inf, F32)
    acc_ref[...] = jnp.zeros(acc_ref.shape, F32)
    for hh in heads:
        s0_ref[hh] = scores(hh, 0)

    def two_blocks(i, carry):
        j = 2 * i
        for hh in heads:
            s1_ref[hh] = scores(hh, j + 1)
            consume(hh, s0_ref, j, masked=False)
        for hh in heads:
            s0_ref[hh] = scores(hh, j + 2)
            consume(hh, s1_ref, j + 1, masked=False)
        return carry

    lax.fori_loop(0, qi // 2, two_blocks, 0)

    @pl.when(qi % 2 == 1)
    def _():
        for hh in heads:
            s1_ref[hh] = scores(hh, qi)
            consume(hh, s0_ref, qi - 1, masked=False)
        for hh in heads:
            consume(hh, s1_ref, qi, masked=True)

    @pl.when(qi % 2 == 0)
    def _():
        for hh in heads:
            consume(hh, s0_ref, qi, masked=True)

    out_t = jnp.concatenate([acc_ref[hh, :dh] / acc_ref[hh, dh:dh + 1] for hh in heads], axis=0)
    o_ref[0] = out_t.T.astype(o_ref.dtype)


def _fox_attn(qt, ka, vt):
    bsz, nh, _, seq = qt.shape
    dh = FOX_DH
    tq, tk = min(FOX_TQ, seq), min(FOX_TK, seq)
    hp = FOX_HEADS_PER_STEP
    assert tq == tk and (hp * dh) % LANES == 0
    kern = functools.partial(_fox_attn_kernel, tq=tq, tk=tk, dh=dh)
    return pl.pallas_call(
        kern,
        grid=(bsz, nh // hp, seq // tq),
        in_specs=[
            pl.BlockSpec((1, hp, LANES, tq), lambda b, h, s: (b, h, 0, s)),
            pl.BlockSpec((1, hp, seq, LANES), lambda b, h, s: (b, h, 0, 0)),
            pl.BlockSpec((1, hp, dh, seq), lambda b, h, s: (b, h, 0, 0)),
        ],
        out_specs=pl.BlockSpec((1, tq, hp * dh), lambda b, h, s: (b, s, h)),
        out_shape=jax.ShapeDtypeStruct((bsz, seq, nh * dh), BF16),
        scratch_shapes=[pltpu.VMEM((hp, tk, tq), F32), pltpu.VMEM((hp, tk, tq), F32),
                        pltpu.VMEM((hp, 1, tq), F32),
                        pltpu.VMEM((hp, dh + FOX_SUM_ROWS, tq), F32)],
        name="fox_attn",
    )(qt, ka, vt)


def _out_kernel(o_ref, z_ref, w_ref, g_ref, gate_ref, x_ref, y_ref):
    y_ref[0] = _layer_tail(x_ref[0], o_ref, z_ref, w_ref, g_ref, gate_ref)


def _out_proj(x, tail):
    o, z, w_out, g_post, mod = tail
    bsz, seq, d = x.shape
    width = o.shape[-1]
    tm = min(ROW_TILE, seq)
    return pl.pallas_call(
        _out_kernel,
        grid=(bsz, seq // tm),
        in_specs=[_row_spec(tm, width), _row_spec(tm, width), _full((width, d)), _full((1, d)),
                  _mod_spec(d, 2), _row_spec(tm, d)],
        out_specs=_row_spec(tm, d),
        out_shape=jax.ShapeDtypeStruct((bsz, seq, d), F32),
        name="out_proj",
    )(o, z, w_out.astype(BF16), g_post.reshape(1, d), mod, x)


def _gla_layer(x, g_pre, mod, tail, w_in, w_a2, b_a, g_head):
    q, k, v, z, gk, *x_new = _gla_in(x, g_pre, mod, tail, w_in, w_a2, b_a)
    return (x_new or [x])[0], _gla_scan(q, k, v, gk, g_head), z


def _sgu_layer(x, g_pre, mod, tail, w_in, ln_g, ln_b, w_s, b_s):
    u, v, z, *x_new = _sgu_in(x, g_pre, mod, tail, w_in, ln_g, ln_b)
    return (x_new or [x])[0], _sgu_mix(u, v, w_s, b_s), z


def _fox_layer(x, g_pre, mod, tail, w_in, b_f, g_q, g_k):
    qt, ka, vt, z, *x_new = _fox_in(x, g_pre, mod, tail, w_in, b_f, g_q, g_k)
    return (x_new or [x])[0], _fox_attn(qt, ka, vt), z


def kernel(x, c, norm_pre_g, norm_post_g, w_mod, b_mod, gla_w_in, gla_w_a2, gla_b_a, gla_g_head, gla_w_out, sgu_w_in, sgu_ln_g, sgu_ln_b, sgu_w_s, sgu_b_s, sgu_w_out, fox_w_in, fox_b_f, fox_g_q, fox_g_k, fox_w_out):
    depth = w_mod.shape[0]
    bsz = x.shape[0]
    mod_all = _modulation(c, w_mod, b_mod)
    tail = None
    for i in range(depth):
        mod = mod_all[i].reshape(bsz, 1, -1)
        kind, j = i % N_MIXERS, i // N_MIXERS
        if kind == 0:
            x, o, z = _gla_layer(x, norm_pre_g[i], mod, tail, gla_w_in[j], gla_w_a2[j],
                                 gla_b_a[j], gla_g_head[j])
            w_out = gla_w_out[j]
        elif kind == 1:
            x, o, z = _sgu_layer(x, norm_pre_g[i], mod, tail, sgu_w_in[j], sgu_ln_g[j],
                                 sgu_ln_b[j], sgu_w_s[j], sgu_b_s[j])
            w_out = sgu_w_out[j]
        else:
            x, o, z = _fox_layer(x, norm_pre_g[i], mod, tail, fox_w_in[j], fox_b_f[j],
                                 fox_g_q[j], fox_g_k[j])
            w_out = fox_w_out[j]
        tail = (o, z, w_out, norm_post_g[i], mod)
    return _out_proj(x, tail)
```

```python
import functools
import math

import numpy as np
import jax
import jax.numpy as jnp
from jax import lax
from jax.experimental import pallas as pl
from jax.experimental.pallas import tpu as pltpu

EPS = 1e-6
N_MIXERS = 3

LANES = 128
ROW_TILE = 512
COL_CHUNK = 512

GLA_HEADS = 4
GLA_RANK = 16
GLA_NORMALIZER = 16.0
GLA_CHUNK = 64
GLA_FAST_CHUNK = 256
GLA_DECAY_GUARD = 70.0
GLA_BLOCK = 512

SGU_GROUPS = 4
SGU_CHUNK = 128

FOX_HEADS = 16
FOX_DH = 64
FOX_TQ = 512
FOX_TK = 512
FOX_HEADS_PER_STEP = 4
FOX_SUM_ROWS = 16
FOX_TERMS = 3
FOX_MAX_SHIFT = 50.0
LOG2E = math.log2(math.e)

BF16 = jnp.bfloat16
F32 = jnp.float32


def _dot(a, b):
    return jnp.dot(a, b, preferred_element_type=F32)


def _dot_nt(a, b):
    return lax.dot_general(a, b, (((1,), (1,)), ((), ())), preferred_element_type=F32)


def _dot_tn(a, b):
    return lax.dot_general(a, b, (((0,), (0,)), ((), ())), preferred_element_type=F32)


def _split2(x):
    hi = x.astype(BF16)
    lo = (x - hi.astype(F32)).astype(BF16)
    return hi, lo


def _split3(x):
    hi = x.astype(BF16)
    r = x - hi.astype(F32)
    mid = r.astype(BF16)
    lo = (r - mid.astype(F32)).astype(BF16)
    return hi, mid, lo


def _log_sigmoid(x):
    return jnp.minimum(x, 0.0) - jnp.log1p(jnp.exp(-jnp.abs(x)))


def _rms(x):
    return x * lax.rsqrt(jnp.mean(x * x, axis=-1, keepdims=True) + EPS)


def _mod_kernel(c_ref, w_ref, b_ref, o_ref):
    cond = jax.nn.silu(c_ref[...])
    o_ref[0] = _dot(cond, w_ref[0]) + b_ref[0]


def _modulation(c, w_mod, b_mod):
    depth, d, d3 = w_mod.shape
    bsz = c.shape[0]
    nblk = d3 // d
    return pl.pallas_call(
        _mod_kernel,
        grid=(depth, nblk),
        in_specs=[
            pl.BlockSpec((bsz, d), lambda i, j: (0, 0)),
            pl.BlockSpec((1, d, d), lambda i, j: (i, 0, j)),
            pl.BlockSpec((1, 1, d), lambda i, j: (i, 0, j)),
        ],
        out_specs=pl.BlockSpec((1, bsz, d), lambda i, j: (i, 0, j)),
        out_shape=jax.ShapeDtypeStruct((depth, bsz, d3), F32),
        name="adaln_mod",
    )(c, w_mod, b_mod.reshape(depth, 1, d3))


N_HEAD_REFS = 4
N_TAIL_REFS = 5


def _layer_tail(x, o_ref, z_ref, w_ref, g_ref, gate_ref):
    gated = o_ref[0].astype(F32) * jax.nn.silu(z_ref[0].astype(F32))
    y = _dot(gated.astype(BF16), w_ref[...])
    return x + gate_ref[0] * (_rms(y) * g_ref[...])


def _layer_input(refs, has_tail, n_scratch=0):
    x_ref, g_ref, shift_ref, scale_ref = refs[:N_HEAD_REFS]
    rest = list(refs[N_HEAD_REFS:])
    x = x_ref[0]
    if has_tail:
        x = _layer_tail(x, *rest[:N_TAIL_REFS])
        rest.pop(len(rest) - n_scratch - 1)[0] = x
        rest = rest[N_TAIL_REFS:]
    h = _rms(x) * g_ref[...]
    h = h * (1.0 + scale_ref[0]) + shift_ref[0]
    return h.astype(BF16), rest


def _full(shape):
    nd = len(shape)
    return pl.BlockSpec(shape, lambda b, s: (0,) * nd, pipeline_mode=pl.Buffered(1))


def _row_spec(tm, width, col=0):
    return pl.BlockSpec((1, tm, width), lambda b, s: (b, s, col))


def _mod_spec(d, part):
    return pl.BlockSpec((1, 1, d), lambda b, s: (b, 0, part))


def _in_proj_call(kern, x, g, mod, tail, consts, out_specs, out_shape, name, scratch=()):
    bsz, seq, d = x.shape
    tm = min(ROW_TILE, seq)
    operands = [x, g.reshape(1, d), mod, mod]
    in_specs = [_row_spec(tm, d), _full((1, d)), _mod_spec(d, 0), _mod_spec(d, 1)]
    if tail is not None:
        o, z, w_out, g_post, mod_prev = tail
        width = o.shape[-1]
        operands += [o, z, w_out.astype(BF16), g_post.reshape(1, d), mod_prev]
        in_specs += [_row_spec(tm, width), _row_spec(tm, width), _full((width, d)),
                     _full((1, d)), _mod_spec(d, 2)]
        out_specs = list(out_specs) + [_row_spec(tm, d)]
        out_shape = list(out_shape) + [jax.ShapeDtypeStruct((bsz, seq, d), F32)]
    operands += list(consts)
    in_specs += [_full(c.shape) for c in consts]
    return pl.pallas_call(
        functools.partial(kern, has_tail=tail is not None),
        grid=(bsz, seq // tm),
        in_specs=in_specs, out_specs=out_specs, out_shape=out_shape,
        scratch_shapes=list(scratch),
        compiler_params=pltpu.CompilerParams(dimension_semantics=("arbitrary", "arbitrary")),
        name=name,
    )(*operands)


def _gla_in_kernel(*refs, has_tail, kd, vd, q_scale):
    h, (w_ref, wa2_ref, ba_ref, q_ref, k_ref, v_ref, z_ref, gk_ref) = _layer_input(refs, has_tail)
    n_main = 2 * kd + 2 * vd
    a_low = _dot(h, w_ref[:, n_main:n_main + LANES])
    logit = _dot(a_low.astype(BF16), wa2_ref[...]) + ba_ref[...]
    gk_ref[0] = _log_sigmoid(logit) * (1.0 / GLA_NORMALIZER)
    col = 0
    for out_ref, width, mul in ((q_ref, kd, q_scale), (k_ref, kd, None),
                                (v_ref, vd, None), (z_ref, vd, None)):
        for c0 in range(0, width, COL_CHUNK):
            acc = _dot(h, w_ref[:, col + c0:col + c0 + COL_CHUNK])
            if mul is not None:
                acc = acc * mul
            out_ref[0, :, c0:c0 + COL_CHUNK] = acc.astype(out_ref.dtype)
        col += width


def _gla_in(x, g, mod, tail, w_in, w_a2, b_a):
    bsz, seq, d = x.shape
    kd = w_a2.shape[1]
    vd = (w_in.shape[1] - 2 * kd - GLA_RANK) // 2
    tm = min(ROW_TILE, seq)
    w = jnp.pad(w_in, ((0, 0), (0, LANES - GLA_RANK))).astype(BF16)
    wa2 = jnp.pad(w_a2, ((0, LANES - GLA_RANK), (0, 0))).astype(BF16)
    dk = kd // GLA_HEADS
    kern = functools.partial(_gla_in_kernel, kd=kd, vd=vd, q_scale=dk ** -0.5)
    act = lambda width, dtype: jax.ShapeDtypeStruct((bsz, seq, width), dtype)
    return _in_proj_call(
        kern, x, g, mod, tail, [w, wa2, b_a.reshape(1, kd)],
        out_specs=[_row_spec(tm, kd), _row_spec(tm, kd), _row_spec(tm, vd),
                   _row_spec(tm, vd), _row_spec(tm, kd)],
        out_shape=[act(kd, BF16), act(kd, BF16), act(vd, BF16), act(vd, BF16), act(kd, F32)],
        name="gla_in_proj")


def _gla_tables(c):
    tri = np.tril(np.ones((c, c), np.float32))
    rows = [tri, 1.0 - tri]
    n_levels = int(math.log2(c))
    level = np.full((c, c), -1, np.int32)
    idx = np.arange(c)
    for l in range(n_levels):
        blk = c >> l
        half = blk // 2
        mid = (idx // blk) * blk + half - 1
        rows.append(tri - tri[mid])
        same = (idx[:, None] // blk) == (idx[None, :] // blk)
        upper = (idx[:, None] % blk) >= half
        lower = (idx[None, :] % blk) < half
        level[same & upper & lower] = l
    level[idx, idx] = n_levels
    return np.concatenate(rows, axis=0), level, n_levels


def _gla_scan_kernel(q_ref, k_ref, v_ref, gk_ref, wf_ref, ws_ref, lvl_ref, gh_ref, o_ref, st_ref,
                     *, fast_chunk, safe_chunk, n_levels, dk, dv):
    tb, kd = gk_ref.shape[1], gk_ref.shape[2]

    @pl.when(pl.program_id(1) == 0)
    def _():
        st_ref[...] = jnp.zeros_like(st_ref)

    def head_chunk(hd, rows, chunk, safe):
        kc = slice(hd * dk, (hd + 1) * dk)
        vc = slice(hd * dv, (hd + 1) * dv)
        w = ws_ref[...] if safe else wf_ref[...]
        r = _dot(w, jnp.concatenate(_split2(gk_ref[0, rows, kc]), axis=0))
        b = r[0:chunk]
        rev = r[chunk:2 * chunk]
        qf = q_ref[0, rows, kc].astype(F32)
        kf = k_ref[0, rows, kc].astype(F32)
        v = v_ref[0, rows, vc]
        st = st_ref[hd]
        q_dec = (qf * jnp.exp(b)).astype(BF16)

        if safe:
            lvl = lvl_ref[...]
            att = jnp.zeros((chunk, chunk), F32)
            for l in range(n_levels + 1):
                if l < n_levels:
                    e = jnp.exp(-jnp.abs(r[(2 + l) * chunk:(3 + l) * chunk]))
                    ql, kl = (qf * e).astype(BF16), (kf * e).astype(BF16)
                else:
                    ql, kl = qf.astype(BF16), kf.astype(BF16)
                att = jnp.where(lvl == l, _dot_nt(ql, kl), att)
        else:
            causal = (lax.broadcasted_iota(jnp.int32, (chunk, chunk), 0)
                      >= lax.broadcasted_iota(jnp.int32, (chunk, chunk), 1))
            att = jnp.where(causal, _dot_nt(q_dec, (kf * jnp.exp(-b)).astype(BF16)), 0.0)

        o = _dot_nt(q_dec, st.astype(BF16)) + _dot(att.astype(BF16), v)
        o_ref[0, rows, vc] = (_rms(o) * gh_ref[:, vc]).astype(o_ref.dtype)
        k_rev = (kf * jnp.exp(rev)).astype(BF16)
        st_ref[hd] = st * jnp.exp(b[chunk - 1:chunk, :]) + _dot_tn(v, k_rev)

    def run(chunk, safe):
        def body(ci, carry):
            rows = pl.ds(pl.multiple_of(ci * chunk, chunk), chunk)
            for hd in range(GLA_HEADS):
                head_chunk(hd, rows, chunk, safe)
            return carry
        lax.fori_loop(0, tb // chunk, body, 0, unroll=2)

    chunk_sums = jnp.sum(gk_ref[0].reshape(tb // fast_chunk, fast_chunk, kd), axis=1)
    fast_ok = jnp.min(chunk_sums) >= -GLA_DECAY_GUARD

    @pl.when(fast_ok)
    def _():
        run(fast_chunk, safe=False)

    @pl.when(jnp.logical_not(fast_ok))
    def _():
        run(safe_chunk, safe=True)


def _gla_scan(q, k, v, gk, g_head):
    bsz, seq, kd = q.shape
    vd = v.shape[-1]
    dk, dv = kd // GLA_HEADS, vd // GLA_HEADS
    tb = min(GLA_BLOCK, seq)
    ws_np, lvl_np, n_levels = _gla_tables(GLA_CHUNK)
    wf_np = _gla_tables(GLA_FAST_CHUNK)[0][:2 * GLA_FAST_CHUNK]
    stack2 = lambda t: jnp.asarray(np.concatenate([t, t], axis=1), BF16)
    wf, ws = stack2(wf_np), stack2(ws_np)
    kern = functools.partial(_gla_scan_kernel, fast_chunk=GLA_FAST_CHUNK, safe_chunk=GLA_CHUNK,
                             n_levels=n_levels, dk=dk, dv=dv)
    return pl.pallas_call(
        kern,
        grid=(bsz, seq // tb),
        in_specs=[_row_spec(tb, kd), _row_spec(tb, kd), _row_spec(tb, vd), _row_spec(tb, kd),
                  _full(wf.shape), _full(ws.shape), _full(lvl_np.shape), _full((1, vd))],
        out_specs=_row_spec(tb, vd),
        out_shape=jax.ShapeDtypeStruct((bsz, seq, vd), BF16),
        scratch_shapes=[pltpu.VMEM((GLA_HEADS, dv, dk), F32)],
        compiler_params=pltpu.CompilerParams(dimension_semantics=("arbitrary", "arbitrary")),
        name="gla_scan",
    )(q, k, v, gk, wf, ws, jnp.asarray(lvl_np), g_head.reshape(1, vd))


def _sgu_in_kernel(*refs, has_tail, width):
    h, (w_ref, lng_ref, lnb_ref, u_ref, v_ref, z_ref, vtmp_ref) = _layer_input(
        refs, has_tail, n_scratch=1)
    for c0 in range(0, width, COL_CHUNK):
        vtmp_ref[:, c0:c0 + COL_CHUNK] = jax.nn.gelu(
            _dot(h, w_ref[:, width + c0:width + c0 + COL_CHUNK]))
    v = vtmp_ref[...]
    mu = jnp.mean(v, axis=-1, keepdims=True)
    vc = v - mu
    var = jnp.mean(vc * vc, axis=-1, keepdims=True)
    v_ref[0] = (vc * lax.rsqrt(var + EPS) * lng_ref[...] + lnb_ref[...]).astype(v_ref.dtype)
    for c0 in range(0, width, COL_CHUNK):
        cols = slice(c0, c0 + COL_CHUNK)
        u_ref[0, :, cols] = jax.nn.gelu(_dot(h, w_ref[:, cols])).astype(u_ref.dtype)
        z_ref[0, :, cols] = _dot(h, w_ref[:, 2 * width + c0:2 * width + c0 + COL_CHUNK]
                                 ).astype(z_ref.dtype)


def _sgu_in(x, g, mod, tail, w_in, ln_g, ln_b):
    bsz, seq, d = x.shape
    width = w_in.shape[1] // 3
    tm = min(ROW_TILE, seq)
    kern = functools.partial(_sgu_in_kernel, width=width)
    act = jax.ShapeDtypeStruct((bsz, seq, width), BF16)
    return _in_proj_call(
        kern, x, g, mod, tail,
        [w_in.astype(BF16), ln_g.reshape(1, width), ln_b.reshape(1, width)],
        out_specs=[_row_spec(tm, width)] * 3, out_shape=[act, act, act],
        name="sgu_in_proj", scratch=[pltpu.VMEM((tm, width), F32)])


def _sgu_mix_kernel(u_ref, v_ref, ws_ref, bs_ref, o_ref, *, chunk, gdim, n_chunks):
    for g in range(SGU_GROUPS):
        cols = slice(g * gdim, (g + 1) * gdim)
        w = ws_ref[g]
        bias = bs_ref[g]
        bias = jnp.concatenate([bias] * (gdim // LANES), axis=1)
        for c in range(n_chunks):
            rows = slice(c * chunk, (c + 1) * chunk)
            mixed = _dot(w, v_ref[0, rows, cols]) + bias
            o_ref[0, rows, cols] = (u_ref[0, rows, cols].astype(F32) * mixed).astype(o_ref.dtype)


def _sgu_mix(u, v, w_s, b_s):
    bsz, seq, width = u.shape
    gdim = width // SGU_GROUPS
    tm = min(ROW_TILE, seq)
    chunk = SGU_CHUNK
    causal = jnp.tril(jnp.ones((chunk, chunk), bool))
    w_causal = jnp.where(causal[None], w_s, 0.0).astype(BF16)
    bias = jnp.broadcast_to(b_s[:, :, None], (SGU_GROUPS, chunk, LANES))
    kern = functools.partial(_sgu_mix_kernel, chunk=chunk, gdim=gdim, n_chunks=tm // chunk)
    return pl.pallas_call(
        kern,
        grid=(bsz, seq // tm),
        in_specs=[_row_spec(tm, width), _row_spec(tm, width),
                  _full((SGU_GROUPS, chunk, chunk)), _full((SGU_GROUPS, chunk, LANES))],
        out_specs=_row_spec(tm, width),
        out_shape=jax.ShapeDtypeStruct((bsz, seq, width), BF16),
        name="sgu_mix",
    )(u, v, w_causal, bias)


def _fox_tables(n_heads, dh):
    width = n_heads * dh
    n_terms = FOX_TERMS
    assert n_terms * n_heads <= LANES and 2 * n_terms + 1 <= dh
    pq = np.zeros((LANES, width), np.float32)
    pk = np.zeros((LANES, width), np.float32)
    cq = np.zeros((1, width), np.float32)
    ck = np.zeros((1, width), np.float32)
    shift_slot = np.zeros((1, width), np.float32)
    for h in range(n_heads):
        base = (h // 2) * 2 * dh + (dh if h % 2 == 0 else 0)
        for p in range(n_terms):
            pq[p * n_heads + h, base + p] = 1.0
            cq[0, base + n_terms + p] = 1.0
            ck[0, base + p] = 1.0
            pk[p * n_heads + h, base + n_terms + p] = -1.0
        cq[0, base + 2 * n_terms] = 1.0
        shift_slot[0, base + 2 * n_terms] = 1.0
    return pq, pk, cq, ck, shift_slot


def _fox_in_kernel(*refs, has_tail, ad, dh, n_heads):
    h, (w_ref, gq_ref, gk_ref, bf_ref, tri_ref, pq_ref, pk_ref, cq_ref, ck_ref,
        qt_ref, ka_ref, vt_ref, z_ref, carry_ref) = _layer_input(refs, has_tail, n_scratch=1)

    @pl.when(pl.program_id(1) == 0)
    def _():
        carry_ref[...] = jnp.zeros_like(carry_ref)

    tm = h.shape[0]
    pair = 2 * dh
    lane = lax.broadcasted_iota(jnp.int32, (tm, pair), 1)
    low = lane < dh

    def half_rms(x):
        sq = x * x
        s_lo = jnp.sum(jnp.where(low, sq, 0.0), axis=-1, keepdims=True)
        s_hi = jnp.sum(jnp.where(low, 0.0, sq), axis=-1, keepdims=True)
        return x * lax.rsqrt(jnp.where(low, s_lo, s_hi) * (1.0 / dh) + EPS)

    f_logit = _dot(h, w_ref[:, 4 * ad:4 * ad + LANES])
    tri = tri_ref[...]
    lf_hi, lf_mid, lf_lo = _split3(_log_sigmoid(f_logit + bf_ref[...]))
    f_cum = _dot(tri, lf_hi) + _dot(tri, lf_mid) + _dot(tri, lf_lo) + carry_ref[...]
    carry_ref[...] = f_cum[tm - 1:tm, :]
    term_list = _split3(f_cum * LOG2E)
    terms = term_list[-1]
    for p in range(FOX_TERMS - 2, -1, -1):
        terms = jnp.where(lane < (p + 1) * n_heads, term_list[p], terms)
    aug_q = _dot(terms, pq_ref[...]) + cq_ref[...]
    aug_k = _dot(terms, pk_ref[...]) + ck_ref[...]

    for c0 in range(0, ad, COL_CHUNK):
        acc_q = _dot(h, w_ref[:, c0:c0 + COL_CHUNK])
        acc_k = _dot(h, w_ref[:, ad + c0:ad + c0 + COL_CHUNK])
        acc_v = _dot(h, w_ref[:, 2 * ad + c0:2 * ad + c0 + COL_CHUNK])
        z_ref[0, :, c0:c0 + COL_CHUNK] = _dot(
            h, w_ref[:, 3 * ad + c0:3 * ad + c0 + COL_CHUNK]).astype(z_ref.dtype)
        for g0 in range(0, COL_CHUNK, pair):
            cols = slice(g0, g0 + pair)
            gcols = slice(c0 + g0, c0 + g0 + pair)
            head = (c0 + g0) // dh
            qn = half_rms(acc_q[:, cols]) * gq_ref[...]
            kn = half_rms(acc_k[:, cols]) * gk_ref[...]
            aq, ak = aug_q[:, gcols], aug_k[:, gcols]
            qt_ref[0, head] = jnp.where(low, qn, aq).T.astype(qt_ref.dtype)
            qt_ref[0, head + 1] = jnp.where(low, aq, qn).T.astype(qt_ref.dtype)
            ka_ref[0, head] = jnp.where(low, kn, ak).astype(ka_ref.dtype)
            ka_ref[0, head + 1] = jnp.where(low, ak, kn).astype(ka_ref.dtype)
            v_t = acc_v[:, cols].T.astype(vt_ref.dtype)
            vt_ref[0, head] = v_t[:dh]
            vt_ref[0, head + 1] = v_t[dh:]


def _fox_in(x, g, mod, tail, w_in, b_f, g_q, g_k, shift):
    bsz, seq, d = x.shape
    nh, dh = FOX_HEADS, FOX_DH
    ad = nh * dh
    tm = min(ROW_TILE, seq)
    pad = LANES - FOX_TERMS * nh
    w = jnp.concatenate([w_in[:, :4 * ad]] + [w_in[:, 4 * ad:]] * FOX_TERMS
                        + [jnp.zeros((d, pad), w_in.dtype)], axis=1).astype(BF16)
    bf = jnp.pad(jnp.tile(b_f, FOX_TERMS), (0, pad)).reshape(1, LANES)
    gq2 = jnp.tile(g_q * (dh ** -0.5 * LOG2E), 2).reshape(1, 2 * dh)
    gk2 = jnp.tile(g_k, 2).reshape(1, 2 * dh)
    tri = jnp.asarray(np.tril(np.ones((tm, tm), np.float32)), BF16)
    pq, pk, cq, ck, shift_slot = _fox_tables(nh, dh)
    ck = jnp.asarray(ck) - shift * jnp.asarray(shift_slot)
    kern = functools.partial(_fox_in_kernel, ad=ad, dh=dh, n_heads=nh)
    return _in_proj_call(
        kern, x, g, mod, tail,
        [w, gq2, gk2, bf, tri, jnp.asarray(pq, BF16), jnp.asarray(pk, BF16), jnp.asarray(cq), ck],
        out_specs=[
            pl.BlockSpec((1, nh, LANES, tm), lambda b, s: (b, 0, 0, s)),
            pl.BlockSpec((1, nh, tm, LANES), lambda b, s: (b, 0, s, 0)),
            pl.BlockSpec((1, nh, dh, tm), lambda b, s: (b, 0, 0, s)),
            _row_spec(tm, ad)],
        out_shape=[jax.ShapeDtypeStruct((bsz, nh, LANES, seq), BF16),
                   jax.ShapeDtypeStruct((bsz, nh, seq, LANES), BF16),
                   jax.ShapeDtypeStruct((bsz, nh, dh, seq), BF16),
                   jax.ShapeDtypeStruct((bsz, seq, ad), BF16)],
        name="fox_in_proj", scratch=[pltpu.VMEM((1, LANES), F32)])


def _fox_attn_kernel(qt_ref, ka_ref, vt_ref, o_ref, s0_ref, s1_ref, m_ref, acc_ref,
                     *, tq, tk, dh):
    qi = pl.program_id(2)
    hp = FOX_HEADS_PER_STEP
    heads = range(hp)
    sum_rows = jnp.ones((FOX_SUM_ROWS, tk), BF16)

    def scores(hh, kv):
        k0 = pl.multiple_of(kv * tk, tk)
        return _dot(ka_ref[0, hh, pl.ds(k0, tk), :], qt_ref[0, hh])

    def consume(hh, s_ref, kv, masked):
        k0 = pl.multiple_of(kv * tk, tk)
        if masked:
            visible = (lax.broadcasted_iota(jnp.int32, (tk, tq), 0)
                       <= lax.broadcasted_iota(jnp.int32, (tk, tq), 1))
            read = lambda: jnp.where(visible, s_ref[hh], -jnp.inf)
        else:
            read = lambda: s_ref[hh]
        m = m_ref[hh]
        m_new = jnp.maximum(m, jnp.max(read(), axis=0, keepdims=True))
        p = jnp.exp2(read() - m_new).astype(BF16)
        v_sum = jnp.concatenate([vt_ref[0, hh, :, pl.ds(k0, tk)], sum_rows], axis=0)
        acc_ref[hh] = acc_ref[hh] * jnp.exp2(m - m_new) + _dot(v_sum, p)
        m_ref[hh] = m_new

    m_ref[...] = jnp.full(m_ref.shape, -jnp.inf, F32)
    acc_ref[...] = jnp.zeros(acc_ref.shape, F32)
    for hh in heads:
        s0_ref[hh] = scores(hh, 0)

    def two_blocks(i, carry):
        j = 2 * i
        for hh in heads:
            s1_ref[hh] = scores(hh, j + 1)
            consume(hh, s0_ref, j, masked=False)
        for hh in heads:
            s0_ref[hh] = scores(hh, j + 2)
            consume(hh, s1_ref, j + 1, masked=False)
        return carry

    lax.fori_loop(0, qi // 2, two_blocks, 0)

    @pl.when(qi % 2 == 1)
    def _():
        for hh in heads:
            s1_ref[hh] = scores(hh, qi)
            consume(hh, s0_ref, qi - 1, masked=False)
        for hh in heads:
            consume(hh, s1_ref, qi, masked=True)

    @pl.when(qi % 2 == 0)
    def _():
        for hh in heads:
            consume(hh, s0_ref, qi, masked=True)

    out_t = jnp.concatenate([acc_ref[hh, :dh] / acc_ref[hh, dh:dh + 1] for hh in heads], axis=0)
    o_ref[0] = out_t.T.astype(o_ref.dtype)


def _fox_attn_shifted_kernel(qt_ref, ka_ref, vt_ref, o_ref, p0_ref, p1_ref, acc_ref,
                             *, tq, tk, dh):
    qi = pl.program_id(2)
    heads = range(FOX_HEADS_PER_STEP)
    sum_rows = jnp.ones((FOX_SUM_ROWS, tk), BF16)

    def probs(hh, kv, masked=False):
        k0 = pl.multiple_of(kv * tk, tk)
        p = jnp.exp2(_dot(ka_ref[0, hh, pl.ds(k0, tk), :], qt_ref[0, hh]))
        if masked:
            visible = (lax.broadcasted_iota(jnp.int32, (tk, tq), 0)
                       <= lax.broadcasted_iota(jnp.int32, (tk, tq), 1))
            p = jnp.where(visible, p, 0.0)
        return p.astype(BF16)

    def fold(hh, p_ref, kv):
        k0 = pl.multiple_of(kv * tk, tk)
        v_sum = jnp.concatenate([vt_ref[0, hh, :, pl.ds(k0, tk)], sum_rows], axis=0)
        acc_ref[hh] += _dot(v_sum, p_ref[hh])

    def held(n):
        return jnp.where(n == 0, qi, 2 * n - 1)

    acc_ref[...] = jnp.zeros(acc_ref.shape, F32)
    for hh in heads:
        p0_ref[hh] = probs(hh, qi, masked=True)

    def two_blocks(i, carry):
        for hh in heads:
            p1_ref[hh] = probs(hh, 2 * i)
            fold(hh, p0_ref, held(i))
        for hh in heads:
            p0_ref[hh] = probs(hh, 2 * i + 1)
            fold(hh, p1_ref, 2 * i)
        return carry

    n_trips = qi // 2
    lax.fori_loop(0, n_trips, two_blocks, 0)

    @pl.when(qi % 2 == 1)
    def _():
        for hh in heads:
            p1_ref[hh] = probs(hh, qi - 1)
            fold(hh, p0_ref, held(n_trips))
        for hh in heads:
            fold(hh, p1_ref, qi - 1)

    @pl.when(qi % 2 == 0)
    def _():
        for hh in heads:
            fold(hh, p0_ref, held(n_trips))

    out_t = jnp.concatenate([acc_ref[hh, :dh] / acc_ref[hh, dh:dh + 1] for hh in heads], axis=0)
    o_ref[0] = out_t.T.astype(o_ref.dtype)


def _fox_attn(qt, ka, vt, shifted):
    bsz, nh, _, seq = qt.shape
    dh = FOX_DH
    tq, tk = min(FOX_TQ, seq), min(FOX_TK, seq)
    hp = FOX_HEADS_PER_STEP
    assert tq == tk and (hp * dh) % LANES == 0
    acc = pltpu.VMEM((hp, dh + FOX_SUM_ROWS, tq), F32)
    if shifted:
        kern = _fox_attn_shifted_kernel
        scratch = [pltpu.VMEM((hp, tk, tq), BF16), pltpu.VMEM((hp, tk, tq), BF16), acc]
    else:
        kern = _fox_attn_kernel
        scratch = [pltpu.VMEM((hp, tk, tq), F32), pltpu.VMEM((hp, tk, tq), F32),
                   pltpu.VMEM((hp, 1, tq), F32), acc]
    return pl.pallas_call(
        functools.partial(kern, tq=tq, tk=tk, dh=dh),
        grid=(bsz, nh // hp, seq // tq),
        in_specs=[
            pl.BlockSpec((1, hp, LANES, tq), lambda b, h, s: (b, h, 0, s)),
            pl.BlockSpec((1, hp, seq, LANES), lambda b, h, s: (b, h, 0, 0)),
            pl.BlockSpec((1, hp, dh, seq), lambda b, h, s: (b, h, 0, 0)),
        ],
        out_specs=pl.BlockSpec((1, tq, hp * dh), lambda b, h, s: (b, s, h)),
        out_shape=jax.ShapeDtypeStruct((bsz, seq, nh * dh), BF16),
        scratch_shapes=scratch,
        name="fox_attn_shifted" if shifted else "fox_attn",
    )(qt, ka, vt)


def _out_kernel(o_ref, z_ref, w_ref, g_ref, gate_ref, x_ref, y_ref):
    y_ref[0] = _layer_tail(x_ref[0], o_ref, z_ref, w_ref, g_ref, gate_ref)


def _out_proj(x, tail):
    o, z, w_out, g_post, mod = tail
    bsz, seq, d = x.shape
    width = o.shape[-1]
    tm = min(ROW_TILE, seq)
    return pl.pallas_call(
        _out_kernel,
        grid=(bsz, seq // tm),
        in_specs=[_row_spec(tm, width), _row_spec(tm, width), _full((width, d)), _full((1, d)),
                  _mod_spec(d, 2), _row_spec(tm, d)],
        out_specs=_row_spec(tm, d),
        out_shape=jax.ShapeDtypeStruct((bsz, seq, d), F32),
        name="out_proj",
    )(o, z, w_out.astype(BF16), g_post.reshape(1, d), mod, x)


def _gla_layer(x, g_pre, mod, tail, w_in, w_a2, b_a, g_head):
    q, k, v, z, gk, *x_new = _gla_in(x, g_pre, mod, tail, w_in, w_a2, b_a)
    return (x_new or [x])[0], _gla_scan(q, k, v, gk, g_head), z


def _sgu_layer(x, g_pre, mod, tail, w_in, ln_g, ln_b, w_s, b_s):
    u, v, z, *x_new = _sgu_in(x, g_pre, mod, tail, w_in, ln_g, ln_b)
    return (x_new or [x])[0], _sgu_mix(u, v, w_s, b_s), z


def _fox_layer(x, g_pre, mod, tail, w_in, b_f, g_q, g_k):
    bound = FOX_DH * (FOX_DH ** -0.5 * LOG2E) * jnp.max(jnp.abs(g_q)) * jnp.max(jnp.abs(g_k))
    shift = 1.02 * bound
    qt, ka, vt, z, *x_new = _fox_in(x, g_pre, mod, tail, w_in, b_f, g_q, g_k, shift)
    o = lax.cond(shift <= FOX_MAX_SHIFT,
                 functools.partial(_fox_attn, shifted=True),
                 functools.partial(_fox_attn, shifted=False), qt, ka, vt)
    return (x_new or [x])[0], o, z


def kernel(x, c, norm_pre_g, norm_post_g, w_mod, b_mod, gla_w_in, gla_w_a2, gla_b_a, gla_g_head, gla_w_out, sgu_w_in, sgu_ln_g, sgu_ln_b, sgu_w_s, sgu_b_s, sgu_w_out, fox_w_in, fox_b_f, fox_g_q, fox_g_k, fox_w_out):
    depth = w_mod.shape[0]
    bsz = x.shape[0]
    mod_all = _modulation(c, w_mod, b_mod)
    tail = None
    for i in range(depth):
        mod = mod_all[i].reshape(bsz, 1, -1)
        kind, j = i % N_MIXERS, i // N_MIXERS
        if kind == 0:
            x, o, z = _gla_layer(x, norm_pre_g[i], mod, tail, gla_w_in[j], gla_w_a2[j],
                                 gla_b_a[j], gla_g_head[j])
            w_out = gla_w_out[j]
        elif kind == 1:
            x, o, z = _sgu_layer(x, norm_pre_g[i], mod, tail, sgu_w_in[j], sgu_ln_g[j],
                                 sgu_ln_b[j], sgu_w_s[j], sgu_b_s[j])
            w_out = sgu_w_out[j]
        else:
            x, o, z = _fox_layer(x, norm_pre_g[i], mod, tail, fox_w_in[j], fox_b_f[j],
                                 fox_g_q[j], fox_g_k[j])
            w_out = fox_w_out[j]
        tail = (o, z, w_out, norm_post_g[i], mod)
    return _out_proj(x, tail)
```

```python
import functools
import math

import numpy as np
import jax
import jax.numpy as jnp
from jax import lax
from jax.experimental import pallas as pl
from jax.experimental.pallas import tpu as pltpu

EPS = 1e-6
N_MIXERS = 3

LANES = 128
ROW_TILE = 512
COL_CHUNK = 512

GLA_HEADS = 4
GLA_RANK = 16
GLA_NORMALIZER = 16.0
GLA_CHUNK = 64
GLA_FAST_CHUNK = 256
GLA_DECAY_GUARD = 70.0
GLA_BLOCK = 512

SGU_GROUPS = 4
SGU_CHUNK = 128

FOX_HEADS = 16
FOX_DH = 64
FOX_TQ = 512
FOX_TK = 512
FOX_HEADS_PER_STEP = {"shifted": 8, "online": 4}
FOX_SUM_ROWS = 16
FOX_TERMS = 3
FOX_MAX_SHIFT = 50.0
LOG2E = math.log2(math.e)

BF16 = jnp.bfloat16
F32 = jnp.float32


def _dot(a, b):
    return jnp.dot(a, b, preferred_element_type=F32)


def _dot_nt(a, b):
    return lax.dot_general(a, b, (((1,), (1,)), ((), ())), preferred_element_type=F32)


def _dot_tn(a, b):
    return lax.dot_general(a, b, (((0,), (0,)), ((), ())), preferred_element_type=F32)


def _split2(x):
    hi = x.astype(BF16)
    lo = (x - hi.astype(F32)).astype(BF16)
    return hi, lo


def _split3(x):
    hi = x.astype(BF16)
    r = x - hi.astype(F32)
    mid = r.astype(BF16)
    lo = (r - mid.astype(F32)).astype(BF16)
    return hi, mid, lo


def _log_sigmoid(x):
    return jnp.minimum(x, 0.0) - jnp.log1p(jnp.exp(-jnp.abs(x)))


def _rms(x):
    return x * lax.rsqrt(jnp.mean(x * x, axis=-1, keepdims=True) + EPS)


def _mod_kernel(c_ref, w_ref, b_ref, o_ref):
    cond = jax.nn.silu(c_ref[...])
    o_ref[0] = _dot(cond, w_ref[0]) + b_ref[0]


def _modulation(c, w_mod, b_mod):
    depth, d, d3 = w_mod.shape
    bsz = c.shape[0]
    nblk = d3 // d
    return pl.pallas_call(
        _mod_kernel,
        grid=(depth, nblk),
        in_specs=[
            pl.BlockSpec((bsz, d), lambda i, j: (0, 0)),
            pl.BlockSpec((1, d, d), lambda i, j: (i, 0, j)),
            pl.BlockSpec((1, 1, d), lambda i, j: (i, 0, j)),
        ],
        out_specs=pl.BlockSpec((1, bsz, d), lambda i, j: (i, 0, j)),
        out_shape=jax.ShapeDtypeStruct((depth, bsz, d3), F32),
        name="adaln_mod",
    )(c, w_mod, b_mod.reshape(depth, 1, d3))


N_HEAD_REFS = 4
N_TAIL_REFS = 5


def _layer_tail(x, o_ref, z_ref, w_ref, g_ref, gate_ref):
    gated = o_ref[0].astype(F32) * jax.nn.silu(z_ref[0].astype(F32))
    y = _dot(gated.astype(BF16), w_ref[...])
    return x + gate_ref[0] * (_rms(y) * g_ref[...])


def _layer_input(refs, has_tail, n_scratch=0):
    x_ref, g_ref, shift_ref, scale_ref = refs[:N_HEAD_REFS]
    rest = list(refs[N_HEAD_REFS:])
    x = x_ref[0]
    if has_tail:
        x = _layer_tail(x, *rest[:N_TAIL_REFS])
        rest.pop(len(rest) - n_scratch - 1)[0] = x
        rest = rest[N_TAIL_REFS:]
    h = _rms(x) * g_ref[...]
    h = h * (1.0 + scale_ref[0]) + shift_ref[0]
    return h.astype(BF16), rest


def _full(shape):
    nd = len(shape)
    return pl.BlockSpec(shape, lambda b, s: (0,) * nd, pipeline_mode=pl.Buffered(1))


def _row_spec(tm, width, col=0):
    return pl.BlockSpec((1, tm, width), lambda b, s: (b, s, col))


def _mod_spec(d, part):
    return pl.BlockSpec((1, 1, d), lambda b, s: (b, 0, part))


def _in_proj_call(kern, x, g, mod, tail, consts, out_specs, out_shape, name, scratch=()):
    bsz, seq, d = x.shape
    tm = min(ROW_TILE, seq)
    operands = [x, g.reshape(1, d), mod, mod]
    in_specs = [_row_spec(tm, d), _full((1, d)), _mod_spec(d, 0), _mod_spec(d, 1)]
    if tail is not None:
        o, z, w_out, g_post, mod_prev = tail
        width = o.shape[-1]
        operands += [o, z, w_out.astype(BF16), g_post.reshape(1, d), mod_prev]
        in_specs += [_row_spec(tm, width), _row_spec(tm, width), _full((width, d)),
                     _full((1, d)), _mod_spec(d, 2)]
        out_specs = list(out_specs) + [_row_spec(tm, d)]
        out_shape = list(out_shape) + [jax.ShapeDtypeStruct((bsz, seq, d), F32)]
    operands += list(consts)
    in_specs += [_full(c.shape) for c in consts]
    return pl.pallas_call(
        functools.partial(kern, has_tail=tail is not None),
        grid=(bsz, seq // tm),
        in_specs=in_specs, out_specs=out_specs, out_shape=out_shape,
        scratch_shapes=list(scratch),
        compiler_params=pltpu.CompilerParams(dimension_semantics=("arbitrary", "arbitrary")),
        name=name,
    )(*operands)


def _gla_in_kernel(*refs, has_tail, kd, vd, q_scale):
    h, (w_ref, wa2_ref, ba_ref, q_ref, k_ref, v_ref, z_ref, gk_ref) = _layer_input(refs, has_tail)
    n_main = 2 * kd + 2 * vd
    a_low = _dot(h, w_ref[:, n_main:n_main + LANES])
    logit = _dot(a_low.astype(BF16), wa2_ref[...]) + ba_ref[...]
    gk_ref[0] = _log_sigmoid(logit) * (1.0 / GLA_NORMALIZER)
    col = 0
    for out_ref, width, mul in ((q_ref, kd, q_scale), (k_ref, kd, None),
                                (v_ref, vd, None), (z_ref, vd, None)):
        for c0 in range(0, width, COL_CHUNK):
            acc = _dot(h, w_ref[:, col + c0:col + c0 + COL_CHUNK])
            if mul is not None:
                acc = acc * mul
            out_ref[0, :, c0:c0 + COL_CHUNK] = acc.astype(out_ref.dtype)
        col += width


def _gla_in(x, g, mod, tail, w_in, w_a2, b_a):
    bsz, seq, d = x.shape
    kd = w_a2.shape[1]
    vd = (w_in.shape[1] - 2 * kd - GLA_RANK) // 2
    tm = min(ROW_TILE, seq)
    w = jnp.pad(w_in, ((0, 0), (0, LANES - GLA_RANK))).astype(BF16)
    wa2 = jnp.pad(w_a2, ((0, LANES - GLA_RANK), (0, 0))).astype(BF16)
    dk = kd // GLA_HEADS
    kern = functools.partial(_gla_in_kernel, kd=kd, vd=vd, q_scale=dk ** -0.5)
    act = lambda width, dtype: jax.ShapeDtypeStruct((bsz, seq, width), dtype)
    return _in_proj_call(
        kern, x, g, mod, tail, [w, wa2, b_a.reshape(1, kd)],
        out_specs=[_row_spec(tm, kd), _row_spec(tm, kd), _row_spec(tm, vd),
                   _row_spec(tm, vd), _row_spec(tm, kd)],
        out_shape=[act(kd, BF16), act(kd, BF16), act(vd, BF16), act(vd, BF16), act(kd, F32)],
        name="gla_in_proj")


def _gla_tables(c):
    tri = np.tril(np.ones((c, c), np.float32))
    rows = [tri, 1.0 - tri]
    n_levels = int(math.log2(c))
    level = np.full((c, c), -1, np.int32)
    idx = np.arange(c)
    for l in range(n_levels):
        blk = c >> l
        half = blk // 2
        mid = (idx // blk) * blk + half - 1
        rows.append(tri - tri[mid])
        same = (idx[:, None] // blk) == (idx[None, :] // blk)
        upper = (idx[:, None] % blk) >= half
        lower = (idx[None, :] % blk) < half
        level[same & upper & lower] = l
    level[idx, idx] = n_levels
    return np.concatenate(rows, axis=0), level, n_levels


def _gla_scan_kernel(q_ref, k_ref, v_ref, gk_ref, wf_ref, ws_ref, lvl_ref, gh_ref, o_ref, st_ref,
                     *, fast_chunk, safe_chunk, n_levels, dk, dv):
    tb, kd = gk_ref.shape[1], gk_ref.shape[2]

    @pl.when(pl.program_id(1) == 0)
    def _():
        st_ref[...] = jnp.zeros_like(st_ref)

    def head_chunk(hd, rows, chunk, safe):
        kc = slice(hd * dk, (hd + 1) * dk)
        vc = slice(hd * dv, (hd + 1) * dv)
        w = ws_ref[...] if safe else wf_ref[...]
        r = _dot(w, jnp.concatenate(_split2(gk_ref[0, rows, kc]), axis=0))
        b = r[0:chunk]
        rev = r[chunk:2 * chunk]
        qf = q_ref[0, rows, kc].astype(F32)
        kf = k_ref[0, rows, kc].astype(F32)
        v = v_ref[0, rows, vc]
        st = st_ref[hd]
        q_dec = (qf * jnp.exp(b)).astype(BF16)

        if safe:
            lvl = lvl_ref[...]
            att = jnp.zeros((chunk, chunk), F32)
            for l in range(n_levels + 1):
                if l < n_levels:
                    e = jnp.exp(-jnp.abs(r[(2 + l) * chunk:(3 + l) * chunk]))
                    ql, kl = (qf * e).astype(BF16), (kf * e).astype(BF16)
                else:
                    ql, kl = qf.astype(BF16), kf.astype(BF16)
                att = jnp.where(lvl == l, _dot_nt(ql, kl), att)
        else:
            causal = (lax.broadcasted_iota(jnp.int32, (chunk, chunk), 0)
                      >= lax.broadcasted_iota(jnp.int32, (chunk, chunk), 1))
            att = jnp.where(causal, _dot_nt(q_dec, (kf * jnp.exp(-b)).astype(BF16)), 0.0)

        o = _dot_nt(q_dec, st.astype(BF16)) + _dot(att.astype(BF16), v)
        o_ref[0, rows, vc] = (_rms(o) * gh_ref[:, vc]).astype(o_ref.dtype)
        k_rev = (kf * jnp.exp(rev)).astype(BF16)
        st_ref[hd] = st * jnp.exp(b[chunk - 1:chunk, :]) + _dot_tn(v, k_rev)

    def run(chunk, safe):
        def body(ci, carry):
            rows = pl.ds(pl.multiple_of(ci * chunk, chunk), chunk)
            for hd in range(GLA_HEADS):
                head_chunk(hd, rows, chunk, safe)
            return carry
        lax.fori_loop(0, tb // chunk, body, 0, unroll=2)

    chunk_sums = jnp.sum(gk_ref[0].reshape(tb // fast_chunk, fast_chunk, kd), axis=1)
    fast_ok = jnp.min(chunk_sums) >= -GLA_DECAY_GUARD

    @pl.when(fast_ok)
    def _():
        run(fast_chunk, safe=False)

    @pl.when(jnp.logical_not(fast_ok))
    def _():
        run(safe_chunk, safe=True)


def _gla_scan(q, k, v, gk, g_head):
    bsz, seq, kd = q.shape
    vd = v.shape[-1]
    dk, dv = kd // GLA_HEADS, vd // GLA_HEADS
    tb = min(GLA_BLOCK, seq)
    ws_np, lvl_np, n_levels = _gla_tables(GLA_CHUNK)
    wf_np = _gla_tables(GLA_FAST_CHUNK)[0][:2 * GLA_FAST_CHUNK]
    stack2 = lambda t: jnp.asarray(np.concatenate([t, t], axis=1), BF16)
    wf, ws = stack2(wf_np), stack2(ws_np)
    kern = functools.partial(_gla_scan_kernel, fast_chunk=GLA_FAST_CHUNK, safe_chunk=GLA_CHUNK,
                             n_levels=n_levels, dk=dk, dv=dv)
    return pl.pallas_call(
        kern,
        grid=(bsz, seq // tb),
        in_specs=[_row_spec(tb, kd), _row_spec(tb, kd), _row_spec(tb, vd), _row_spec(tb, kd),
                  _full(wf.shape), _full(ws.shape), _full(lvl_np.shape), _full((1, vd))],
        out_specs=_row_spec(tb, vd),
        out_shape=jax.ShapeDtypeStruct((bsz, seq, vd), BF16),
        scratch_shapes=[pltpu.VMEM((GLA_HEADS, dv, dk), F32)],
        compiler_params=pltpu.CompilerParams(dimension_semantics=("arbitrary", "arbitrary")),
        name="gla_scan",
    )(q, k, v, gk, wf, ws, jnp.asarray(lvl_np), g_head.reshape(1, vd))


def _sgu_in_kernel(*refs, has_tail, width):
    h, (w_ref, lng_ref, lnb_ref, u_ref, v_ref, z_ref, vtmp_ref) = _layer_input(
        refs, has_tail, n_scratch=1)
    for c0 in range(0, width, COL_CHUNK):
        vtmp_ref[:, c0:c0 + COL_CHUNK] = jax.nn.gelu(
            _dot(h, w_ref[:, width + c0:width + c0 + COL_CHUNK]))
    v = vtmp_ref[...]
    mu = jnp.mean(v, axis=-1, keepdims=True)
    vc = v - mu
    var = jnp.mean(vc * vc, axis=-1, keepdims=True)
    v_ref[0] = (vc * lax.rsqrt(var + EPS) * lng_ref[...] + lnb_ref[...]).astype(v_ref.dtype)
    for c0 in range(0, width, COL_CHUNK):
        cols = slice(c0, c0 + COL_CHUNK)
        u_ref[0, :, cols] = jax.nn.gelu(_dot(h, w_ref[:, cols])).astype(u_ref.dtype)
        z_ref[0, :, cols] = _dot(h, w_ref[:, 2 * width + c0:2 * width + c0 + COL_CHUNK]
                                 ).astype(z_ref.dtype)


def _sgu_in(x, g, mod, tail, w_in, ln_g, ln_b):
    bsz, seq, d = x.shape
    width = w_in.shape[1] // 3
    tm = min(ROW_TILE, seq)
    kern = functools.partial(_sgu_in_kernel, width=width)
    act = jax.ShapeDtypeStruct((bsz, seq, width), BF16)
    return _in_proj_call(
        kern, x, g, mod, tail,
        [w_in.astype(BF16), ln_g.reshape(1, width), ln_b.reshape(1, width)],
        out_specs=[_row_spec(tm, width)] * 3, out_shape=[act, act, act],
        name="sgu_in_proj", scratch=[pltpu.VMEM((tm, width), F32)])


def _sgu_mix_kernel(u_ref, v_ref, ws_ref, bs_ref, o_ref, *, chunk, gdim, n_chunks):
    for g in range(SGU_GROUPS):
        cols = slice(g * gdim, (g + 1) * gdim)
        w = ws_ref[g]
        bias = bs_ref[g]
        bias = jnp.concatenate([bias] * (gdim // LANES), axis=1)
        for c in range(n_chunks):
            rows = slice(c * chunk, (c + 1) * chunk)
            mixed = _dot(w, v_ref[0, rows, cols]) + bias
            o_ref[0, rows, cols] = (u_ref[0, rows, cols].astype(F32) * mixed).astype(o_ref.dtype)


def _sgu_mix(u, v, w_s, b_s):
    bsz, seq, width = u.shape
    gdim = width // SGU_GROUPS
    tm = min(ROW_TILE, seq)
    chunk = SGU_CHUNK
    causal = jnp.tril(jnp.ones((chunk, chunk), bool))
    w_causal = jnp.where(causal[None], w_s, 0.0).astype(BF16)
    bias = jnp.broadcast_to(b_s[:, :, None], (SGU_GROUPS, chunk, LANES))
    kern = functools.partial(_sgu_mix_kernel, chunk=chunk, gdim=gdim, n_chunks=tm // chunk)
    return pl.pallas_call(
        kern,
        grid=(bsz, seq // tm),
        in_specs=[_row_spec(tm, width), _row_spec(tm, width),
                  _full((SGU_GROUPS, chunk, chunk)), _full((SGU_GROUPS, chunk, LANES))],
        out_specs=_row_spec(tm, width),
        out_shape=jax.ShapeDtypeStruct((bsz, seq, width), BF16),
        name="sgu_mix",
    )(u, v, w_causal, bias)


def _fox_tables(n_heads, dh):
    width = n_heads * dh
    n_terms = FOX_TERMS
    assert n_terms * n_heads <= LANES and 2 * n_terms + 1 <= dh
    pq = np.zeros((LANES, width), np.float32)
    pk = np.zeros((LANES, width), np.float32)
    cq = np.zeros((1, width), np.float32)
    ck = np.zeros((1, width), np.float32)
    shift_slot = np.zeros((1, width), np.float32)
    for h in range(n_heads):
        base = (h // 2) * 2 * dh + (dh if h % 2 == 0 else 0)
        for p in range(n_terms):
            pq[p * n_heads + h, base + p] = 1.0
            cq[0, base + n_terms + p] = 1.0
            ck[0, base + p] = 1.0
            pk[p * n_heads + h, base + n_terms + p] = -1.0
        cq[0, base + 2 * n_terms] = 1.0
        shift_slot[0, base + 2 * n_terms] = 1.0
    return pq, pk, cq, ck, shift_slot


def _fox_in_kernel(*refs, has_tail, ad, dh, n_heads):
    h, (w_ref, gq_ref, gk_ref, bf_ref, tri_ref, pq_ref, pk_ref, cq_ref, ck_ref,
        qt_ref, ka_ref, vt_ref, z_ref, carry_ref) = _layer_input(refs, has_tail, n_scratch=1)

    @pl.when(pl.program_id(1) == 0)
    def _():
        carry_ref[...] = jnp.zeros_like(carry_ref)

    tm = h.shape[0]
    pair = 2 * dh
    lane = lax.broadcasted_iota(jnp.int32, (tm, pair), 1)
    low = lane < dh

    def half_rms(x):
        sq = x * x
        s_lo = jnp.sum(jnp.where(low, sq, 0.0), axis=-1, keepdims=True)
        s_hi = jnp.sum(jnp.where(low, 0.0, sq), axis=-1, keepdims=True)
        return x * lax.rsqrt(jnp.where(low, s_lo, s_hi) * (1.0 / dh) + EPS)

    f_logit = _dot(h, w_ref[:, 4 * ad:4 * ad + LANES])
    tri = tri_ref[...]
    lf_hi, lf_mid, lf_lo = _split3(_log_sigmoid(f_logit + bf_ref[...]))
    f_cum = _dot(tri, lf_hi) + _dot(tri, lf_mid) + _dot(tri, lf_lo) + carry_ref[...]
    carry_ref[...] = f_cum[tm - 1:tm, :]
    term_list = _split3(f_cum * LOG2E)
    terms = term_list[-1]
    for p in range(FOX_TERMS - 2, -1, -1):
        terms = jnp.where(lane < (p + 1) * n_heads, term_list[p], terms)
    aug_q = _dot(terms, pq_ref[...]) + cq_ref[...]
    aug_k = _dot(terms, pk_ref[...]) + ck_ref[...]

    for c0 in range(0, ad, COL_CHUNK):
        acc_q = _dot(h, w_ref[:, c0:c0 + COL_CHUNK])
        acc_k = _dot(h, w_ref[:, ad + c0:ad + c0 + COL_CHUNK])
        acc_v = _dot(h, w_ref[:, 2 * ad + c0:2 * ad + c0 + COL_CHUNK])
        z_ref[0, :, c0:c0 + COL_CHUNK] = _dot(
            h, w_ref[:, 3 * ad + c0:3 * ad + c0 + COL_CHUNK]).astype(z_ref.dtype)
        for g0 in range(0, COL_CHUNK, pair):
            cols = slice(g0, g0 + pair)
            gcols = slice(c0 + g0, c0 + g0 + pair)
            head = (c0 + g0) // dh
            qn = half_rms(acc_q[:, cols]) * gq_ref[...]
            kn = half_rms(acc_k[:, cols]) * gk_ref[...]
            aq, ak = aug_q[:, gcols], aug_k[:, gcols]
            qt_ref[0, head] = jnp.where(low, qn, aq).T.astype(qt_ref.dtype)
            qt_ref[0, head + 1] = jnp.where(low, aq, qn).T.astype(qt_ref.dtype)
            ka_ref[0, head] = jnp.where(low, kn, ak).astype(ka_ref.dtype)
            ka_ref[0, head + 1] = jnp.where(low, ak, kn).astype(ka_ref.dtype)
            v_t = acc_v[:, cols].T.astype(vt_ref.dtype)
            vt_ref[0, head] = v_t[:dh]
            vt_ref[0, head + 1] = v_t[dh:]


def _fox_in(x, g, mod, tail, w_in, b_f, g_q, g_k, shift):
    bsz, seq, d = x.shape
    nh, dh = FOX_HEADS, FOX_DH
    ad = nh * dh
    tm = min(ROW_TILE, seq)
    pad = LANES - FOX_TERMS * nh
    w = jnp.concatenate([w_in[:, :4 * ad]] + [w_in[:, 4 * ad:]] * FOX_TERMS
                        + [jnp.zeros((d, pad), w_in.dtype)], axis=1).astype(BF16)
    bf = jnp.pad(jnp.tile(b_f, FOX_TERMS), (0, pad)).reshape(1, LANES)
    gq2 = jnp.tile(g_q * (dh ** -0.5 * LOG2E), 2).reshape(1, 2 * dh)
    gk2 = jnp.tile(g_k, 2).reshape(1, 2 * dh)
    tri = jnp.asarray(np.tril(np.ones((tm, tm), np.float32)), BF16)
    pq, pk, cq, ck, shift_slot = _fox_tables(nh, dh)
    ck = jnp.asarray(ck) - shift * jnp.asarray(shift_slot)
    kern = functools.partial(_fox_in_kernel, ad=ad, dh=dh, n_heads=nh)
    return _in_proj_call(
        kern, x, g, mod, tail,
        [w, gq2, gk2, bf, tri, jnp.asarray(pq, BF16), jnp.asarray(pk, BF16), jnp.asarray(cq), ck],
        out_specs=[
            pl.BlockSpec((1, nh, LANES, tm), lambda b, s: (b, 0, 0, s)),
            pl.BlockSpec((1, nh, tm, LANES), lambda b, s: (b, 0, s, 0)),
            pl.BlockSpec((1, nh, dh, tm), lambda b, s: (b, 0, 0, s)),
            _row_spec(tm, ad)],
        out_shape=[jax.ShapeDtypeStruct((bsz, nh, LANES, seq), BF16),
                   jax.ShapeDtypeStruct((bsz, nh, seq, LANES), BF16),
                   jax.ShapeDtypeStruct((bsz, nh, dh, seq), BF16),
                   jax.ShapeDtypeStruct((bsz, seq, ad), BF16)],
        name="fox_in_proj", scratch=[pltpu.VMEM((1, LANES), F32)])


def _fox_attn_kernel(qt_ref, ka_ref, vt_ref, o_ref, s0_ref, s1_ref, m_ref, acc_ref,
                     *, tq, tk, dh):
    qi = pl.program_id(2)
    hp = qt_ref.shape[1]
    heads = range(hp)
    sum_rows = jnp.ones((FOX_SUM_ROWS, tk), BF16)

    def scores(hh, kv):
        k0 = pl.multiple_of(kv * tk, tk)
        return _dot(ka_ref[0, hh, pl.ds(k0, tk), :], qt_ref[0, hh])

    def consume(hh, s_ref, kv, masked):
        k0 = pl.multiple_of(kv * tk, tk)
        if masked:
            visible = (lax.broadcasted_iota(jnp.int32, (tk, tq), 0)
                       <= lax.broadcasted_iota(jnp.int32, (tk, tq), 1))
            read = lambda: jnp.where(visible, s_ref[hh], -jnp.inf)
        else:
            read = lambda: s_ref[hh]
        m = m_ref[hh]
        m_new = jnp.maximum(m, jnp.max(read(), axis=0, keepdims=True))
        p = jnp.exp2(read() - m_new).astype(BF16)
        v_sum = jnp.concatenate([vt_ref[0, hh, :, pl.ds(k0, tk)], sum_rows], axis=0)
        acc_ref[hh] = acc_ref[hh] * jnp.exp2(m - m_new) + _dot(v_sum, p)
        m_ref[hh] = m_new

    m_ref[...] = jnp.full(m_ref.shape, -jnp.inf, F32)
    acc_ref[...] = jnp.zeros(acc_ref.shape, F32)
    for hh in heads:
        s0_ref[hh] = scores(hh, 0)

    def two_blocks(i, carry):
        j = 2 * i
        for hh in heads:
            s1_ref[hh] = scores(hh, j + 1)
            consume(hh, s0_ref, j, masked=False)
        for hh in heads:
            s0_ref[hh] = scores(hh, j + 2)
            consume(hh, s1_ref, j + 1, masked=False)
        return carry

    lax.fori_loop(0, qi // 2, two_blocks, 0)

    @pl.when(qi % 2 == 1)
    def _():
        for hh in heads:
            s1_ref[hh] = scores(hh, qi)
            consume(hh, s0_ref, qi - 1, masked=False)
        for hh in heads:
            consume(hh, s1_ref, qi, masked=True)

    @pl.when(qi % 2 == 0)
    def _():
        for hh in heads:
            consume(hh, s0_ref, qi, masked=True)

    out_t = jnp.concatenate([acc_ref[hh, :dh] / acc_ref[hh, dh:dh + 1] for hh in heads], axis=0)
    o_ref[0] = out_t.T.astype(o_ref.dtype)


def _fox_attn_shifted_kernel(qt_ref, ka_ref, vt_ref, o_ref, p0_ref, p1_ref, acc_ref,
                             *, tq, tk, dh):
    qi = pl.program_id(2)
    hp = qt_ref.shape[1]
    bufs = (p0_ref, p1_ref)
    sum_rows = jnp.ones((FOX_SUM_ROWS, tk), BF16)

    def probs(hh, kv, masked):
        k0 = pl.multiple_of(kv * tk, tk)
        p = jnp.exp2(_dot(ka_ref[0, hh, pl.ds(k0, tk), :], qt_ref[0, hh]))
        if masked:
            visible = (lax.broadcasted_iota(jnp.int32, (tk, tq), 0)
                       <= lax.broadcasted_iota(jnp.int32, (tk, tq), 1))
            p = jnp.where(visible, p, 0.0)
        return p.astype(BF16)

    def fold(hh, kv, p_ref):
        k0 = pl.multiple_of(kv * tk, tk)
        v_sum = jnp.concatenate([vt_ref[0, hh, :, pl.ds(k0, tk)], sum_rows], axis=0)
        acc_ref[hh] += _dot(v_sum, p_ref[...])

    def section(blocks, masked, following):
        units = [(kv, hh) for kv in blocks for hh in range(hp)]
        for n, (kv, hh) in enumerate(units):
            nxt = units[n + 1] if n + 1 < len(units) else (following, 0)
            if nxt[0] is not None:
                bufs[(n + 1) % 2][...] = probs(nxt[1], nxt[0], masked and n + 1 < len(units))
            fold(hh, kv, bufs[n % 2])

    acc_ref[...] = jnp.zeros(acc_ref.shape, F32)
    p0_ref[...] = probs(0, qi, masked=True)
    section([qi], masked=True, following=0)

    def two_blocks(i, carry):
        section([2 * i, 2 * i + 1], masked=False, following=2 * i + 2)
        return carry

    lax.fori_loop(0, qi // 2, two_blocks, 0)

    @pl.when(qi % 2 == 1)
    def _():
        section([qi - 1], masked=False, following=None)

    out_t = jnp.concatenate([acc_ref[hh, :dh] / acc_ref[hh, dh:dh + 1] for hh in range(hp)],
                            axis=0)
    o_ref[0] = out_t.T.astype(o_ref.dtype)


def _fox_attn(qt, ka, vt, shifted):
    bsz, nh, _, seq = qt.shape
    dh = FOX_DH
    tq, tk = min(FOX_TQ, seq), min(FOX_TK, seq)
    hp = FOX_HEADS_PER_STEP["shifted" if shifted else "online"]
    assert tq == tk and (hp * dh) % LANES == 0
    acc = pltpu.VMEM((hp, dh + FOX_SUM_ROWS, tq), F32)
    if shifted:
        kern = _fox_attn_shifted_kernel
        scratch = [pltpu.VMEM((tk, tq), BF16), pltpu.VMEM((tk, tq), BF16), acc]
    else:
        kern = _fox_attn_kernel
        scratch = [pltpu.VMEM((hp, tk, tq), F32), pltpu.VMEM((hp, tk, tq), F32),
                   pltpu.VMEM((hp, 1, tq), F32), acc]
    return pl.pallas_call(
        functools.partial(kern, tq=tq, tk=tk, dh=dh),
        grid=(bsz, nh // hp, seq // tq),
        in_specs=[
            pl.BlockSpec((1, hp, LANES, tq), lambda b, h, s: (b, h, 0, s)),
            pl.BlockSpec((1, hp, seq, LANES), lambda b, h, s: (b, h, 0, 0)),
            pl.BlockSpec((1, hp, dh, seq), lambda b, h, s: (b, h, 0, 0)),
        ],
        out_specs=pl.BlockSpec((1, tq, hp * dh), lambda b, h, s: (b, s, h)),
        out_shape=jax.ShapeDtypeStruct((bsz, seq, nh * dh), BF16),
        scratch_shapes=scratch,
        name="fox_attn_shifted" if shifted else "fox_attn",
    )(qt, ka, vt)


def _out_kernel(o_ref, z_ref, w_ref, g_ref, gate_ref, x_ref, y_ref):
    y_ref[0] = _layer_tail(x_ref[0], o_ref, z_ref, w_ref, g_ref, gate_ref)


def _out_proj(x, tail):
    o, z, w_out, g_post, mod = tail
    bsz, seq, d = x.shape
    width = o.shape[-1]
    tm = min(ROW_TILE, seq)
    return pl.pallas_call(
        _out_kernel,
        grid=(bsz, seq // tm),
        in_specs=[_row_spec(tm, width), _row_spec(tm, width), _full((width, d)), _full((1, d)),
                  _mod_spec(d, 2), _row_spec(tm, d)],
        out_specs=_row_spec(tm, d),
        out_shape=jax.ShapeDtypeStruct((bsz, seq, d), F32),
        name="out_proj",
    )(o, z, w_out.astype(BF16), g_post.reshape(1, d), mod, x)


def _gla_layer(x, g_pre, mod, tail, w_in, w_a2, b_a, g_head):
    q, k, v, z, gk, *x_new = _gla_in(x, g_pre, mod, tail, w_in, w_a2, b_a)
    return (x_new or [x])[0], _gla_scan(q, k, v, gk, g_head), z


def _sgu_layer(x, g_pre, mod, tail, w_in, ln_g, ln_b, w_s, b_s):
    u, v, z, *x_new = _sgu_in(x, g_pre, mod, tail, w_in, ln_g, ln_b)
    return (x_new or [x])[0], _sgu_mix(u, v, w_s, b_s), z


def _fox_layer(x, g_pre, mod, tail, w_in, b_f, g_q, g_k):
    bound = FOX_DH * (FOX_DH ** -0.5 * LOG2E) * jnp.max(jnp.abs(g_q)) * jnp.max(jnp.abs(g_k))
    shift = 1.02 * bound
    qt, ka, vt, z, *x_new = _fox_in(x, g_pre, mod, tail, w_in, b_f, g_q, g_k, shift)
    o = lax.cond(shift <= FOX_MAX_SHIFT,
                 functools.partial(_fox_attn, shifted=True),
                 functools.partial(_fox_attn, shifted=False), qt, ka, vt)
    return (x_new or [x])[0], o, z


def kernel(x, c, norm_pre_g, norm_post_g, w_mod, b_mod, gla_w_in, gla_w_a2, gla_b_a, gla_g_head, gla_w_out, sgu_w_in, sgu_ln_g, sgu_ln_b, sgu_w_s, sgu_b_s, sgu_w_out, fox_w_in, fox_b_f, fox_g_q, fox_g_k, fox_w_out):
    depth = w_mod.shape[0]
    bsz = x.shape[0]
    mod_all = _modulation(c, w_mod, b_mod)
    tail = None
    for i in range(depth):
        mod = mod_all[i].reshape(bsz, 1, -1)
        kind, j = i % N_MIXERS, i // N_MIXERS
        if kind == 0:
            x, o, z = _gla_layer(x, norm_pre_g[i], mod, tail, gla_w_in[j], gla_w_a2[j],
                                 gla_b_a[j], gla_g_head[j])
            w_out = gla_w_out[j]
        elif kind == 1:
            x, o, z = _sgu_layer(x, norm_pre_g[i], mod, tail, sgu_w_in[j], sgu_ln_g[j],
                                 sgu_ln_b[j], sgu_w_s[j], sgu_b_s[j])
            w_out = sgu_w_out[j]
        else:
            x, o, z = _fox_layer(x, norm_pre_g[i], mod, tail, fox_w_in[j], fox_b_f[j],
                                 fox_g_q[j], fox_g_k[j])
            w_out = fox_w_out[j]
        tail = (o, z, w_out, norm_post_g[i], mod)
    return _out_proj(x, tail)
```

```python
import functools
import math

import numpy as np
import jax
import jax.numpy as jnp
from jax import lax
from jax.experimental import pallas as pl
from jax.experimental.pallas import tpu as pltpu

EPS = 1e-6
N_MIXERS = 3

LANES = 128
ROW_TILE = 512
COL_CHUNK = 512

GLA_HEADS = 4
GLA_RANK = 16
GLA_NORMALIZER = 16.0
GLA_CHUNK = 64
GLA_FAST_CHUNK = 256
GLA_DECAY_GUARD = 70.0
GLA_BLOCK = 1024

SGU_GROUPS = 4
SGU_CHUNK = 128

FOX_HEADS = 16
FOX_DH = 64
FOX_TQ = 512
FOX_TK = 512
FOX_HEADS_PER_STEP = {"shifted": 8, "online": 4}
FOX_SUM_ROWS = 16
FOX_TERMS = 3
FOX_MAX_SHIFT = 50.0
LOG2E = math.log2(math.e)

BF16 = jnp.bfloat16
F32 = jnp.float32


def _dot(a, b):
    return jnp.dot(a, b, preferred_element_type=F32)


def _dot_nt(a, b):
    return lax.dot_general(a, b, (((1,), (1,)), ((), ())), preferred_element_type=F32)


def _dot_tn(a, b):
    return lax.dot_general(a, b, (((0,), (0,)), ((), ())), preferred_element_type=F32)


def _split2(x):
    hi = x.astype(BF16)
    lo = (x - hi.astype(F32)).astype(BF16)
    return hi, lo


def _split3(x):
    hi = x.astype(BF16)
    r = x - hi.astype(F32)
    mid = r.astype(BF16)
    lo = (r - mid.astype(F32)).astype(BF16)
    return hi, mid, lo


def _log_sigmoid(x):
    return jnp.minimum(x, 0.0) - jnp.log1p(jnp.exp(-jnp.abs(x)))


def _rms(x):
    return x * lax.rsqrt(jnp.mean(x * x, axis=-1, keepdims=True) + EPS)


def _mod_kernel(c_ref, w_ref, b_ref, o_ref):
    cond = jax.nn.silu(c_ref[...])
    o_ref[0] = _dot(cond, w_ref[0]) + b_ref[0]


def _modulation(c, w_mod, b_mod):
    depth, d, d3 = w_mod.shape
    bsz = c.shape[0]
    nblk = d3 // d
    return pl.pallas_call(
        _mod_kernel,
        grid=(depth, nblk),
        in_specs=[
            pl.BlockSpec((bsz, d), lambda i, j: (0, 0)),
            pl.BlockSpec((1, d, d), lambda i, j: (i, 0, j)),
            pl.BlockSpec((1, 1, d), lambda i, j: (i, 0, j)),
        ],
        out_specs=pl.BlockSpec((1, bsz, d), lambda i, j: (i, 0, j)),
        out_shape=jax.ShapeDtypeStruct((depth, bsz, d3), F32),
        name="adaln_mod",
    )(c, w_mod, b_mod.reshape(depth, 1, d3))


N_HEAD_REFS = 4
N_TAIL_REFS = 4


def _silu_gated(o, z):
    return (o * jax.nn.silu(z.astype(F32))).astype(BF16)


def _layer_tail(x, a_ref, w_ref, g_ref, gate_ref):
    y = _dot(a_ref[0], w_ref[...])
    return x + gate_ref[0] * (_rms(y) * g_ref[...])


def _layer_input(refs, has_tail, n_scratch=0):
    x_ref, g_ref, shift_ref, scale_ref = refs[:N_HEAD_REFS]
    rest = list(refs[N_HEAD_REFS:])
    x = x_ref[0]
    if has_tail:
        x = _layer_tail(x, *rest[:N_TAIL_REFS])
        rest.pop(len(rest) - n_scratch - 1)[0] = x
        rest = rest[N_TAIL_REFS:]
    h = _rms(x) * g_ref[...]
    h = h * (1.0 + scale_ref[0]) + shift_ref[0]
    return h.astype(BF16), rest


def _full(shape):
    nd = len(shape)
    return pl.BlockSpec(shape, lambda b, s: (0,) * nd, pipeline_mode=pl.Buffered(1))


def _row_spec(tm, width, col=0):
    return pl.BlockSpec((1, tm, width), lambda b, s: (b, s, col))


def _mod_spec(d, part):
    return pl.BlockSpec((1, 1, d), lambda b, s: (b, 0, part))


def _in_proj_call(kern, x, g, mod, tail, consts, out_specs, out_shape, name, scratch=()):
    bsz, seq, d = x.shape
    tm = min(ROW_TILE, seq)
    operands = [x, g.reshape(1, d), mod, mod]
    in_specs = [_row_spec(tm, d), _full((1, d)), _mod_spec(d, 0), _mod_spec(d, 1)]
    if tail is not None:
        a, w_out, g_post, mod_prev = tail
        width = a.shape[-1]
        operands += [a, w_out.astype(BF16), g_post.reshape(1, d), mod_prev]
        in_specs += [_row_spec(tm, width), _full((width, d)), _full((1, d)), _mod_spec(d, 2)]
        out_specs = list(out_specs) + [_row_spec(tm, d)]
        out_shape = list(out_shape) + [jax.ShapeDtypeStruct((bsz, seq, d), F32)]
    operands += list(consts)
    in_specs += [_full(c.shape) for c in consts]
    return pl.pallas_call(
        functools.partial(kern, has_tail=tail is not None),
        grid=(bsz, seq // tm),
        in_specs=in_specs, out_specs=out_specs, out_shape=out_shape,
        scratch_shapes=list(scratch),
        compiler_params=pltpu.CompilerParams(dimension_semantics=("arbitrary", "arbitrary")),
        name=name,
    )(*operands)


def _gla_in_kernel(*refs, has_tail, kd, vd, q_scale):
    h, (w_ref, wa2_ref, ba_ref, q_ref, k_ref, v_ref, z_ref, gk_ref) = _layer_input(refs, has_tail)
    n_main = 2 * kd + 2 * vd
    a_low = _dot(h, w_ref[:, n_main:n_main + LANES])
    logit = _dot(a_low.astype(BF16), wa2_ref[...]) + ba_ref[...]
    gk_ref[0] = _log_sigmoid(logit) * (1.0 / GLA_NORMALIZER)
    col = 0
    for out_ref, width, mul in ((q_ref, kd, q_scale), (k_ref, kd, None),
                                (v_ref, vd, None), (z_ref, vd, None)):
        for c0 in range(0, width, COL_CHUNK):
            acc = _dot(h, w_ref[:, col + c0:col + c0 + COL_CHUNK])
            if mul is not None:
                acc = acc * mul
            out_ref[0, :, c0:c0 + COL_CHUNK] = acc.astype(out_ref.dtype)
        col += width


def _gla_in(x, g, mod, tail, w_in, w_a2, b_a):
    bsz, seq, d = x.shape
    kd = w_a2.shape[1]
    vd = (w_in.shape[1] - 2 * kd - GLA_RANK) // 2
    tm = min(ROW_TILE, seq)
    w = jnp.pad(w_in, ((0, 0), (0, LANES - GLA_RANK))).astype(BF16)
    wa2 = jnp.pad(w_a2, ((0, LANES - GLA_RANK), (0, 0))).astype(BF16)
    dk = kd // GLA_HEADS
    kern = functools.partial(_gla_in_kernel, kd=kd, vd=vd, q_scale=dk ** -0.5)
    act = lambda width, dtype: jax.ShapeDtypeStruct((bsz, seq, width), dtype)
    return _in_proj_call(
        kern, x, g, mod, tail, [w, wa2, b_a.reshape(1, kd)],
        out_specs=[_row_spec(tm, kd), _row_spec(tm, kd), _row_spec(tm, vd),
                   _row_spec(tm, vd), _row_spec(tm, kd)],
        out_shape=[act(kd, BF16), act(kd, BF16), act(vd, BF16), act(vd, BF16), act(kd, F32)],
        name="gla_in_proj")


def _gla_tables(c):
    tri = np.tril(np.ones((c, c), np.float32))
    rows = [tri, 1.0 - tri]
    n_levels = int(math.log2(c))
    level = np.full((c, c), -1, np.int32)
    idx = np.arange(c)
    for l in range(n_levels):
        blk = c >> l
        half = blk // 2
        mid = (idx // blk) * blk + half - 1
        rows.append(tri - tri[mid])
        same = (idx[:, None] // blk) == (idx[None, :] // blk)
        upper = (idx[:, None] % blk) >= half
        lower = (idx[None, :] % blk) < half
        level[same & upper & lower] = l
    level[idx, idx] = n_levels
    return np.concatenate(rows, axis=0), level, n_levels


def _gla_scan_kernel(q_ref, k_ref, v_ref, gk_ref, z_ref, wf_ref, ws_ref, lvl_ref, gh_ref, o_ref,
                     st_ref, *, fast_chunk, safe_chunk, n_levels, dk, dv):
    tb, kd = gk_ref.shape[1], gk_ref.shape[2]

    @pl.when(pl.program_id(1) == 0)
    def _():
        st_ref[...] = jnp.zeros_like(st_ref)

    def head_chunk(hd, rows, chunk, safe):
        kc = slice(hd * dk, (hd + 1) * dk)
        vc = slice(hd * dv, (hd + 1) * dv)
        w = ws_ref[...] if safe else wf_ref[...]
        r = _dot(w, jnp.concatenate(_split2(gk_ref[0, rows, kc]), axis=0))
        b = r[0:chunk]
        rev = r[chunk:2 * chunk]
        qf = q_ref[0, rows, kc].astype(F32)
        kf = k_ref[0, rows, kc].astype(F32)
        v = v_ref[0, rows, vc]
        st = st_ref[hd]
        q_dec = (qf * jnp.exp(b)).astype(BF16)

        if safe:
            lvl = lvl_ref[...]
            att = jnp.zeros((chunk, chunk), F32)
            for l in range(n_levels + 1):
                if l < n_levels:
                    e = jnp.exp(-jnp.abs(r[(2 + l) * chunk:(3 + l) * chunk]))
                    ql, kl = (qf * e).astype(BF16), (kf * e).astype(BF16)
                else:
                    ql, kl = qf.astype(BF16), kf.astype(BF16)
                att = jnp.where(lvl == l, _dot_nt(ql, kl), att)
        else:
            causal = (lax.broadcasted_iota(jnp.int32, (chunk, chunk), 0)
                      >= lax.broadcasted_iota(jnp.int32, (chunk, chunk), 1))
            att = jnp.where(causal, _dot_nt(q_dec, (kf * jnp.exp(-b)).astype(BF16)), 0.0)

        o = _dot_nt(q_dec, st.astype(BF16)) + _dot(att.astype(BF16), v)
        o_ref[0, rows, vc] = _silu_gated(_rms(o) * gh_ref[:, vc], z_ref[0, rows, vc])
        k_rev = (kf * jnp.exp(rev)).astype(BF16)
        st_ref[hd] = st * jnp.exp(b[chunk - 1:chunk, :]) + _dot_tn(v, k_rev)

    def run(chunk, safe):
        def body(ci, carry):
            rows = pl.ds(pl.multiple_of(ci * chunk, chunk), chunk)
            for hd in range(GLA_HEADS):
                head_chunk(hd, rows, chunk, safe)
            return carry
        lax.fori_loop(0, tb // chunk, body, 0, unroll=2)

    chunk_sums = jnp.sum(gk_ref[0].reshape(tb // fast_chunk, fast_chunk, kd), axis=1)
    fast_ok = jnp.min(chunk_sums) >= -GLA_DECAY_GUARD

    @pl.when(fast_ok)
    def _():
        run(fast_chunk, safe=False)

    @pl.when(jnp.logical_not(fast_ok))
    def _():
        run(safe_chunk, safe=True)


def _gla_scan(q, k, v, gk, z, g_head):
    bsz, seq, kd = q.shape
    vd = v.shape[-1]
    dk, dv = kd // GLA_HEADS, vd // GLA_HEADS
    tb = min(GLA_BLOCK, seq)
    ws_np, lvl_np, n_levels = _gla_tables(GLA_CHUNK)
    wf_np = _gla_tables(GLA_FAST_CHUNK)[0][:2 * GLA_FAST_CHUNK]
    stack2 = lambda t: jnp.asarray(np.concatenate([t, t], axis=1), BF16)
    wf, ws = stack2(wf_np), stack2(ws_np)
    kern = functools.partial(_gla_scan_kernel, fast_chunk=GLA_FAST_CHUNK, safe_chunk=GLA_CHUNK,
                             n_levels=n_levels, dk=dk, dv=dv)
    return pl.pallas_call(
        kern,
        grid=(bsz, seq // tb),
        in_specs=[_row_spec(tb, kd), _row_spec(tb, kd), _row_spec(tb, vd), _row_spec(tb, kd),
                  _row_spec(tb, vd),
                  _full(wf.shape), _full(ws.shape), _full(lvl_np.shape), _full((1, vd))],
        out_specs=_row_spec(tb, vd),
        out_shape=jax.ShapeDtypeStruct((bsz, seq, vd), BF16),
        scratch_shapes=[pltpu.VMEM((GLA_HEADS, dv, dk), F32)],
        compiler_params=pltpu.CompilerParams(dimension_semantics=("arbitrary", "arbitrary")),
        name="gla_scan",
    )(q, k, v, gk, z, wf, ws, jnp.asarray(lvl_np), g_head.reshape(1, vd))


def _sgu_in_kernel(*refs, has_tail, width):
    h, (w_ref, lng_ref, lnb_ref, u_ref, v_ref, z_ref, vtmp_ref) = _layer_input(
        refs, has_tail, n_scratch=1)
    for c0 in range(0, width, COL_CHUNK):
        vtmp_ref[:, c0:c0 + COL_CHUNK] = jax.nn.gelu(
            _dot(h, w_ref[:, width + c0:width + c0 + COL_CHUNK]))
    v = vtmp_ref[...]
    mu = jnp.mean(v, axis=-1, keepdims=True)
    vc = v - mu
    var = jnp.mean(vc * vc, axis=-1, keepdims=True)
    v_ref[0] = (vc * lax.rsqrt(var + EPS) * lng_ref[...] + lnb_ref[...]).astype(v_ref.dtype)
    for c0 in range(0, width, COL_CHUNK):
        cols = slice(c0, c0 + COL_CHUNK)
        u_ref[0, :, cols] = jax.nn.gelu(_dot(h, w_ref[:, cols])).astype(u_ref.dtype)
        z_ref[0, :, cols] = _dot(h, w_ref[:, 2 * width + c0:2 * width + c0 + COL_CHUNK]
                                 ).astype(z_ref.dtype)


def _sgu_in(x, g, mod, tail, w_in, ln_g, ln_b):
    bsz, seq, d = x.shape
    width = w_in.shape[1] // 3
    tm = min(ROW_TILE, seq)
    kern = functools.partial(_sgu_in_kernel, width=width)
    act = jax.ShapeDtypeStruct((bsz, seq, width), BF16)
    return _in_proj_call(
        kern, x, g, mod, tail,
        [w_in.astype(BF16), ln_g.reshape(1, width), ln_b.reshape(1, width)],
        out_specs=[_row_spec(tm, width)] * 3, out_shape=[act, act, act],
        name="sgu_in_proj", scratch=[pltpu.VMEM((tm, width), F32)])


def _sgu_mix_kernel(u_ref, v_ref, z_ref, ws_ref, bs_ref, o_ref, *, chunk, gdim, n_chunks):
    for g in range(SGU_GROUPS):
        cols = slice(g * gdim, (g + 1) * gdim)
        w = ws_ref[g]
        bias = bs_ref[g]
        bias = jnp.concatenate([bias] * (gdim // LANES), axis=1)
        for c in range(n_chunks):
            rows = slice(c * chunk, (c + 1) * chunk)
            mixed = _dot(w, v_ref[0, rows, cols]) + bias
            o_ref[0, rows, cols] = _silu_gated(u_ref[0, rows, cols].astype(F32) * mixed,
                                               z_ref[0, rows, cols])


def _sgu_mix(u, v, z, w_s, b_s):
    bsz, seq, width = u.shape
    gdim = width // SGU_GROUPS
    tm = min(ROW_TILE, seq)
    chunk = SGU_CHUNK
    causal = jnp.tril(jnp.ones((chunk, chunk), bool))
    w_causal = jnp.where(causal[None], w_s, 0.0).astype(BF16)
    bias = jnp.broadcast_to(b_s[:, :, None], (SGU_GROUPS, chunk, LANES))
    kern = functools.partial(_sgu_mix_kernel, chunk=chunk, gdim=gdim, n_chunks=tm // chunk)
    return pl.pallas_call(
        kern,
        grid=(bsz, seq // tm),
        in_specs=[_row_spec(tm, width), _row_spec(tm, width), _row_spec(tm, width),
                  _full((SGU_GROUPS, chunk, chunk)), _full((SGU_GROUPS, chunk, LANES))],
        out_specs=_row_spec(tm, width),
        out_shape=jax.ShapeDtypeStruct((bsz, seq, width), BF16),
        name="sgu_mix",
    )(u, v, z, w_causal, bias)


def _fox_tables(n_heads, dh):
    width = n_heads * dh
    n_terms = FOX_TERMS
    assert n_terms * n_heads <= LANES and 2 * n_terms + 1 <= dh
    pq = np.zeros((LANES, width), np.float32)
    pk = np.zeros((LANES, width), np.float32)
    cq = np.zeros((1, width), np.float32)
    ck = np.zeros((1, width), np.float32)
    shift_slot = np.zeros((1, width), np.float32)
    for h in range(n_heads):
        base = (h // 2) * 2 * dh + (dh if h % 2 == 0 else 0)
        for p in range(n_terms):
            pq[p * n_heads + h, base + p] = 1.0
            cq[0, base + n_terms + p] = 1.0
            ck[0, base + p] = 1.0
            pk[p * n_heads + h, base + n_terms + p] = -1.0
        cq[0, base + 2 * n_terms] = 1.0
        shift_slot[0, base + 2 * n_terms] = 1.0
    return pq, pk, cq, ck, shift_slot


def _fox_in_kernel(*refs, has_tail, ad, dh, n_heads):
    h, (w_ref, gq_ref, gk_ref, bf_ref, tri_ref, pq_ref, pk_ref, cq_ref, ck_ref,
        qt_ref, ka_ref, vt_ref, z_ref, carry_ref) = _layer_input(refs, has_tail, n_scratch=1)

    @pl.when(pl.program_id(1) == 0)
    def _():
        carry_ref[...] = jnp.zeros_like(carry_ref)

    tm = h.shape[0]
    pair = 2 * dh
    lane = lax.broadcasted_iota(jnp.int32, (tm, pair), 1)
    low = lane < dh

    def half_rms(x):
        sq = x * x
        s_lo = jnp.sum(jnp.where(low, sq, 0.0), axis=-1, keepdims=True)
        s_hi = jnp.sum(jnp.where(low, 0.0, sq), axis=-1, keepdims=True)
        return x * lax.rsqrt(jnp.where(low, s_lo, s_hi) * (1.0 / dh) + EPS)

    f_logit = _dot(h, w_ref[:, 4 * ad:4 * ad + LANES])
    tri = tri_ref[...]
    lf_hi, lf_mid, lf_lo = _split3(_log_sigmoid(f_logit + bf_ref[...]))
    f_cum = _dot(tri, lf_hi) + _dot(tri, lf_mid) + _dot(tri, lf_lo) + carry_ref[...]
    carry_ref[...] = f_cum[tm - 1:tm, :]
    term_list = _split3(f_cum * LOG2E)
    terms = term_list[-1]
    for p in range(FOX_TERMS - 2, -1, -1):
        terms = jnp.where(lane < (p + 1) * n_heads, term_list[p], terms)
    aug_q = _dot(terms, pq_ref[...]) + cq_ref[...]
    aug_k = _dot(terms, pk_ref[...]) + ck_ref[...]

    for c0 in range(0, ad, COL_CHUNK):
        acc_q = _dot(h, w_ref[:, c0:c0 + COL_CHUNK])
        acc_k = _dot(h, w_ref[:, ad + c0:ad + c0 + COL_CHUNK])
        acc_v = _dot(h, w_ref[:, 2 * ad + c0:2 * ad + c0 + COL_CHUNK])
        z_ref[0, :, c0:c0 + COL_CHUNK] = _dot(
            h, w_ref[:, 3 * ad + c0:3 * ad + c0 + COL_CHUNK]).astype(z_ref.dtype)
        for g0 in range(0, COL_CHUNK, pair):
            cols = slice(g0, g0 + pair)
            gcols = slice(c0 + g0, c0 + g0 + pair)
            head = (c0 + g0) // dh
            qn = half_rms(acc_q[:, cols]) * gq_ref[...]
            kn = half_rms(acc_k[:, cols]) * gk_ref[...]
            aq, ak = aug_q[:, gcols], aug_k[:, gcols]
            qt_ref[0, head] = jnp.where(low, qn, aq).T.astype(qt_ref.dtype)
            qt_ref[0, head + 1] = jnp.where(low, aq, qn).T.astype(qt_ref.dtype)
            ka_ref[0, head] = jnp.where(low, kn, ak).astype(ka_ref.dtype)
            ka_ref[0, head + 1] = jnp.where(low, ak, kn).astype(ka_ref.dtype)
            v_t = acc_v[:, cols].T.astype(vt_ref.dtype)
            vt_ref[0, head] = v_t[:dh]
            vt_ref[0, head + 1] = v_t[dh:]


def _fox_in(x, g, mod, tail, w_in, b_f, g_q, g_k, shift):
    bsz, seq, d = x.shape
    nh, dh = FOX_HEADS, FOX_DH
    ad = nh * dh
    tm = min(ROW_TILE, seq)
    pad = LANES - FOX_TERMS * nh
    w = jnp.concatenate([w_in[:, :4 * ad]] + [w_in[:, 4 * ad:]] * FOX_TERMS
                        + [jnp.zeros((d, pad), w_in.dtype)], axis=1).astype(BF16)
    bf = jnp.pad(jnp.tile(b_f, FOX_TERMS), (0, pad)).reshape(1, LANES)
    gq2 = jnp.tile(g_q * (dh ** -0.5 * LOG2E), 2).reshape(1, 2 * dh)
    gk2 = jnp.tile(g_k, 2).reshape(1, 2 * dh)
    tri = jnp.asarray(np.tril(np.ones((tm, tm), np.float32)), BF16)
    pq, pk, cq, ck, shift_slot = _fox_tables(nh, dh)
    ck = jnp.asarray(ck) - shift * jnp.asarray(shift_slot)
    kern = functools.partial(_fox_in_kernel, ad=ad, dh=dh, n_heads=nh)
    return _in_proj_call(
        kern, x, g, mod, tail,
        [w, gq2, gk2, bf, tri, jnp.asarray(pq, BF16), jnp.asarray(pk, BF16), jnp.asarray(cq), ck],
        out_specs=[
            pl.BlockSpec((1, nh, LANES, tm), lambda b, s: (b, 0, 0, s)),
            pl.BlockSpec((1, nh, tm, LANES), lambda b, s: (b, 0, s, 0)),
            pl.BlockSpec((1, nh, dh, tm), lambda b, s: (b, 0, 0, s)),
            _row_spec(tm, ad)],
        out_shape=[jax.ShapeDtypeStruct((bsz, nh, LANES, seq), BF16),
                   jax.ShapeDtypeStruct((bsz, nh, seq, LANES), BF16),
                   jax.ShapeDtypeStruct((bsz, nh, dh, seq), BF16),
                   jax.ShapeDtypeStruct((bsz, seq, ad), BF16)],
        name="fox_in_proj", scratch=[pltpu.VMEM((1, LANES), F32)])


def _fox_attn_kernel(qt_ref, ka_ref, vt_ref, z_ref, o_ref, s0_ref, s1_ref, m_ref, acc_ref,
                     *, tq, tk, dh):
    qi = pl.program_id(2)
    hp = qt_ref.shape[1]
    heads = range(hp)
    sum_rows = jnp.ones((FOX_SUM_ROWS, tk), BF16)

    def scores(hh, kv):
        k0 = pl.multiple_of(kv * tk, tk)
        return _dot(ka_ref[0, hh, pl.ds(k0, tk), :], qt_ref[0, hh])

    def consume(hh, s_ref, kv, masked):
        k0 = pl.multiple_of(kv * tk, tk)
        if masked:
            visible = (lax.broadcasted_iota(jnp.int32, (tk, tq), 0)
                       <= lax.broadcasted_iota(jnp.int32, (tk, tq), 1))
            read = lambda: jnp.where(visible, s_ref[hh], -jnp.inf)
        else:
            read = lambda: s_ref[hh]
        m = m_ref[hh]
        m_new = jnp.maximum(m, jnp.max(read(), axis=0, keepdims=True))
        p = jnp.exp2(read() - m_new).astype(BF16)
        v_sum = jnp.concatenate([vt_ref[0, hh, :, pl.ds(k0, tk)], sum_rows], axis=0)
        acc_ref[hh] = acc_ref[hh] * jnp.exp2(m - m_new) + _dot(v_sum, p)
        m_ref[hh] = m_new

    m_ref[...] = jnp.full(m_ref.shape, -jnp.inf, F32)
    acc_ref[...] = jnp.zeros(acc_ref.shape, F32)
    for hh in heads:
        s0_ref[hh] = scores(hh, 0)

    def two_blocks(i, carry):
        j = 2 * i
        for hh in heads:
            s1_ref[hh] = scores(hh, j + 1)
            consume(hh, s0_ref, j, masked=False)
        for hh in heads:
            s0_ref[hh] = scores(hh, j + 2)
            consume(hh, s1_ref, j + 1, masked=False)
        return carry

    lax.fori_loop(0, qi // 2, two_blocks, 0)

    @pl.when(qi % 2 == 1)
    def _():
        for hh in heads:
            s1_ref[hh] = scores(hh, qi)
            consume(hh, s0_ref, qi - 1, masked=False)
        for hh in heads:
            consume(hh, s1_ref, qi, masked=True)

    @pl.when(qi % 2 == 0)
    def _():
        for hh in heads:
            consume(hh, s0_ref, qi, masked=True)

    out_t = jnp.concatenate([acc_ref[hh, :dh] / acc_ref[hh, dh:dh + 1] for hh in heads], axis=0)
    o_ref[0] = _silu_gated(out_t.T, z_ref[0])


def _fox_attn_shifted_kernel(qt_ref, ka_ref, vt_ref, z_ref, o_ref, p0_ref, p1_ref, acc_ref,
                             *, tq, tk, dh):
    qi = pl.program_id(2)
    hp = qt_ref.shape[1]
    bufs = (p0_ref, p1_ref)
    sum_rows = jnp.ones((FOX_SUM_ROWS, tk), BF16)

    def probs(hh, kv, masked):
        k0 = pl.multiple_of(kv * tk, tk)
        p = jnp.exp2(_dot(ka_ref[0, hh, pl.ds(k0, tk), :], qt_ref[0, hh]))
        if masked:
            visible = (lax.broadcasted_iota(jnp.int32, (tk, tq), 0)
                       <= lax.broadcasted_iota(jnp.int32, (tk, tq), 1))
            p = jnp.where(visible, p, 0.0)
        return p.astype(BF16)

    def fold(hh, kv, p_ref):
        k0 = pl.multiple_of(kv * tk, tk)
        v_sum = jnp.concatenate([vt_ref[0, hh, :, pl.ds(k0, tk)], sum_rows], axis=0)
        acc_ref[hh] += _dot(v_sum, p_ref[...])

    def section(blocks, masked, following):
        units = [(kv, hh) for kv in blocks for hh in range(hp)]
        for n, (kv, hh) in enumerate(units):
            nxt = units[n + 1] if n + 1 < len(units) else (following, 0)
            if nxt[0] is not None:
                bufs[(n + 1) % 2][...] = probs(nxt[1], nxt[0], masked and n + 1 < len(units))
            fold(hh, kv, bufs[n % 2])

    acc_ref[...] = jnp.zeros(acc_ref.shape, F32)
    p0_ref[...] = probs(0, qi, masked=True)
    section([qi], masked=True, following=0)

    def two_blocks(i, carry):
        section([2 * i, 2 * i + 1], masked=False, following=2 * i + 2)
        return carry

    lax.fori_loop(0, qi // 2, two_blocks, 0)

    @pl.when(qi % 2 == 1)
    def _():
        section([qi - 1], masked=False, following=None)

    out_t = jnp.concatenate([acc_ref[hh, :dh] / acc_ref[hh, dh:dh + 1] for hh in range(hp)],
                            axis=0)
    o_ref[0] = _silu_gated(out_t.T, z_ref[0])


def _fox_attn(qt, ka, vt, z, shifted):
    bsz, nh, _, seq = qt.shape
    dh = FOX_DH
    tq, tk = min(FOX_TQ, seq), min(FOX_TK, seq)
    hp = FOX_HEADS_PER_STEP["shifted" if shifted else "online"]
    assert tq == tk and (hp * dh) % LANES == 0
    acc = pltpu.VMEM((hp, dh + FOX_SUM_ROWS, tq), F32)
    if shifted:
        kern = _fox_attn_shifted_kernel
        scratch = [pltpu.VMEM((tk, tq), BF16), pltpu.VMEM((tk, tq), BF16), acc]
    else:
        kern = _fox_attn_kernel
        scratch = [pltpu.VMEM((hp, tk, tq), F32), pltpu.VMEM((hp, tk, tq), F32),
                   pltpu.VMEM((hp, 1, tq), F32), acc]
    return pl.pallas_call(
        functools.partial(kern, tq=tq, tk=tk, dh=dh),
        grid=(bsz, nh // hp, seq // tq),
        in_specs=[
            pl.BlockSpec((1, hp, LANES, tq), lambda b, h, s: (b, h, 0, s)),
            pl.BlockSpec((1, hp, seq, LANES), lambda b, h, s: (b, h, 0, 0)),
            pl.BlockSpec((1, hp, dh, seq), lambda b, h, s: (b, h, 0, 0)),
            pl.BlockSpec((1, tq, hp * dh), lambda b, h, s: (b, s, h)),
        ],
        out_specs=pl.BlockSpec((1, tq, hp * dh), lambda b, h, s: (b, s, h)),
        out_shape=jax.ShapeDtypeStruct((bsz, seq, nh * dh), BF16),
        scratch_shapes=scratch,
        name="fox_attn_shifted" if shifted else "fox_attn",
    )(qt, ka, vt, z)


def _out_kernel(a_ref, w_ref, g_ref, gate_ref, x_ref, y_ref):
    y_ref[0] = _layer_tail(x_ref[0], a_ref, w_ref, g_ref, gate_ref)


def _out_proj(x, tail):
    a, w_out, g_post, mod = tail
    bsz, seq, d = x.shape
    width = a.shape[-1]
    tm = min(ROW_TILE, seq)
    return pl.pallas_call(
        _out_kernel,
        grid=(bsz, seq // tm),
        in_specs=[_row_spec(tm, width), _full((width, d)), _full((1, d)), _mod_spec(d, 2),
                  _row_spec(tm, d)],
        out_specs=_row_spec(tm, d),
        out_shape=jax.ShapeDtypeStruct((bsz, seq, d), F32),
        name="out_proj",
    )(a, w_out.astype(BF16), g_post.reshape(1, d), mod, x)


def _gla_layer(x, g_pre, mod, tail, w_in, w_a2, b_a, g_head):
    q, k, v, z, gk, *x_new = _gla_in(x, g_pre, mod, tail, w_in, w_a2, b_a)
    return (x_new or [x])[0], _gla_scan(q, k, v, gk, z, g_head)


def _sgu_layer(x, g_pre, mod, tail, w_in, ln_g, ln_b, w_s, b_s):
    u, v, z, *x_new = _sgu_in(x, g_pre, mod, tail, w_in, ln_g, ln_b)
    return (x_new or [x])[0], _sgu_mix(u, v, z, w_s, b_s)


def _fox_layer(x, g_pre, mod, tail, w_in, b_f, g_q, g_k):
    bound = FOX_DH * (FOX_DH ** -0.5 * LOG2E) * jnp.max(jnp.abs(g_q)) * jnp.max(jnp.abs(g_k))
    shift = 1.02 * bound
    qt, ka, vt, z, *x_new = _fox_in(x, g_pre, mod, tail, w_in, b_f, g_q, g_k, shift)
    a = lax.cond(shift <= FOX_MAX_SHIFT,
                 functools.partial(_fox_attn, shifted=True),
                 functools.partial(_fox_attn, shifted=False), qt, ka, vt, z)
    return (x_new or [x])[0], a


def kernel(x, c, norm_pre_g, norm_post_g, w_mod, b_mod, gla_w_in, gla_w_a2, gla_b_a, gla_g_head, gla_w_out, sgu_w_in, sgu_ln_g, sgu_ln_b, sgu_w_s, sgu_b_s, sgu_w_out, fox_w_in, fox_b_f, fox_g_q, fox_g_k, fox_w_out):
    depth = w_mod.shape[0]
    bsz = x.shape[0]
    mod_all = _modulation(c, w_mod, b_mod)
    tail = None
    for i in range(depth):
        mod = mod_all[i].reshape(bsz, 1, -1)
        kind, j = i % N_MIXERS, i // N_MIXERS
        if kind == 0:
            x, a = _gla_layer(x, norm_pre_g[i], mod, tail, gla_w_in[j], gla_w_a2[j],
                                 gla_b_a[j], gla_g_head[j])
            w_out = gla_w_out[j]
        elif kind == 1:
            x, a = _sgu_layer(x, norm_pre_g[i], mod, tail, sgu_w_in[j], sgu_ln_g[j],
                                 sgu_ln_b[j], sgu_w_s[j], sgu_b_s[j])
            w_out = sgu_w_out[j]
        else:
            x, a = _fox_layer(x, norm_pre_g[i], mod, tail, fox_w_in[j], fox_b_f[j],
                                 fox_g_q[j], fox_g_k[j])
            w_out = fox_w_out[j]
        tail = (a, w_out, norm_post_g[i], mod)
    return _out_proj(x, tail)
```

```python
import functools
import math

import numpy as np
import jax
import jax.numpy as jnp
from jax import lax
from jax.experimental import pallas as pl
from jax.experimental.pallas import tpu as pltpu

EPS = 1e-6
N_MIXERS = 3

LANES = 128
ROW_TILE = 512
ROW_TILES_PER_STEP = 2
COL_CHUNK = 512

GLA_HEADS = 4
GLA_RANK = 16
GLA_NORMALIZER = 16.0
GLA_CHUNK = 64
GLA_FAST_CHUNK = 256
GLA_DECAY_GUARD = 70.0
GLA_BLOCK = 512

SGU_GROUPS = 4
SGU_CHUNK = 128

FOX_HEADS = 16
FOX_DH = 64
FOX_TQ = 512
FOX_TK = 512
FOX_HEADS_PER_STEP = {"shifted": 8, "online": 4}
FOX_SUM_ROWS = 16
FOX_TERMS = 3
FOX_MAX_SHIFT = 50.0
LOG2E = math.log2(math.e)

BF16 = jnp.bfloat16
F32 = jnp.float32


def _dot(a, b):
    return jnp.dot(a, b, preferred_element_type=F32)


def _dot_nt(a, b):
    return lax.dot_general(a, b, (((1,), (1,)), ((), ())), preferred_element_type=F32)


def _dot_tn(a, b):
    return lax.dot_general(a, b, (((0,), (0,)), ((), ())), preferred_element_type=F32)


def _split2(x):
    hi = x.astype(BF16)
    lo = (x - hi.astype(F32)).astype(BF16)
    return hi, lo


def _split3(x):
    hi = x.astype(BF16)
    r = x - hi.astype(F32)
    mid = r.astype(BF16)
    lo = (r - mid.astype(F32)).astype(BF16)
    return hi, mid, lo


def _log_sigmoid(x):
    return jnp.minimum(x, 0.0) - jnp.log1p(jnp.exp(-jnp.abs(x)))


def _rms(x):
    return x * lax.rsqrt(jnp.mean(x * x, axis=-1, keepdims=True) + EPS)


def _mod_kernel(c_ref, w_ref, b_ref, o_ref):
    cond = jax.nn.silu(c_ref[...])
    o_ref[0] = _dot(cond, w_ref[0]) + b_ref[0]


def _modulation(c, w_mod, b_mod):
    depth, d, d3 = w_mod.shape
    bsz = c.shape[0]
    nblk = d3 // d
    return pl.pallas_call(
        _mod_kernel,
        grid=(depth, nblk),
        in_specs=[
            pl.BlockSpec((bsz, d), lambda i, j: (0, 0)),
            pl.BlockSpec((1, d, d), lambda i, j: (i, 0, j)),
            pl.BlockSpec((1, 1, d), lambda i, j: (i, 0, j)),
        ],
        out_specs=pl.BlockSpec((1, bsz, d), lambda i, j: (i, 0, j)),
        out_shape=jax.ShapeDtypeStruct((depth, bsz, d3), F32),
        name="adaln_mod",
    )(c, w_mod, b_mod.reshape(depth, 1, d3))


N_HEAD_REFS = 4
N_TAIL_REFS = 5


def _layer_tail(x, rows, o_ref, z_ref, w_ref, g_ref, gate_ref):
    gated = o_ref[0, rows, :].astype(F32) * jax.nn.silu(z_ref[0, rows, :].astype(F32))
    y = _dot(gated.astype(BF16), w_ref[...])
    return x + gate_ref[0] * (_rms(y) * g_ref[...])


def _kernel_refs(refs, has_tail, n_scratch=0):
    rest = list(refs[N_HEAD_REFS:])
    if has_tail:
        del rest[len(rest) - n_scratch - 1]
        rest = rest[N_TAIL_REFS:]
    return rest


def _row_tiles(refs):
    rows = refs[0].shape[1]
    return [slice(r, r + ROW_TILE) for r in range(0, rows, min(ROW_TILE, rows))]


def _layer_input(refs, has_tail, rows, n_scratch=0):
    x_ref, g_ref, shift_ref, scale_ref = refs[:N_HEAD_REFS]
    x = x_ref[0, rows, :]
    if has_tail:
        x = _layer_tail(x, rows, *refs[N_HEAD_REFS:N_HEAD_REFS + N_TAIL_REFS])
        refs[len(refs) - n_scratch - 1][0, rows, :] = x
    h = _rms(x) * g_ref[...]
    h = h * (1.0 + scale_ref[0]) + shift_ref[0]
    return h.astype(BF16)


def _full(shape):
    nd = len(shape)
    return pl.BlockSpec(shape, lambda b, s: (0,) * nd, pipeline_mode=pl.Buffered(1))


def _row_spec(tm, width, col=0):
    return pl.BlockSpec((1, tm, width), lambda b, s: (b, s, col))


def _mod_spec(d, part):
    return pl.BlockSpec((1, 1, d), lambda b, s: (b, 0, part))


def _in_proj_call(kern, x, g, mod, tail, consts, out_specs, out_shape, name, tm, scratch=()):
    bsz, seq, d = x.shape
    operands = [x, g.reshape(1, d), mod, mod]
    in_specs = [_row_spec(tm, d), _full((1, d)), _mod_spec(d, 0), _mod_spec(d, 1)]
    if tail is not None:
        o, z, w_out, g_post, mod_prev = tail
        width = o.shape[-1]
        operands += [o, z, w_out.astype(BF16), g_post.reshape(1, d), mod_prev]
        in_specs += [_row_spec(tm, width), _row_spec(tm, width), _full((width, d)),
                     _full((1, d)), _mod_spec(d, 2)]
        out_specs = list(out_specs) + [_row_spec(tm, d)]
        out_shape = list(out_shape) + [jax.ShapeDtypeStruct((bsz, seq, d), F32)]
    operands += list(consts)
    in_specs += [_full(c.shape) for c in consts]
    return pl.pallas_call(
        functools.partial(kern, has_tail=tail is not None),
        grid=(bsz, seq // tm),
        in_specs=in_specs, out_specs=out_specs, out_shape=out_shape,
        scratch_shapes=list(scratch),
        compiler_params=pltpu.CompilerParams(dimension_semantics=("arbitrary", "arbitrary")),
        name=name,
    )(*operands)


def _gla_in_kernel(*refs, has_tail, kd, vd, q_scale):
    w_ref, wa2_ref, ba_ref, q_ref, k_ref, v_ref, z_ref, gk_ref = _kernel_refs(refs, has_tail)
    n_main = 2 * kd + 2 * vd
    tiles = _row_tiles(refs)
    inputs = [_layer_input(refs, has_tail, rows) for rows in tiles]
    for rows, h in zip(tiles, inputs):
        a_low = _dot(h, w_ref[:, n_main:n_main + LANES])
        logit = _dot(a_low.astype(BF16), wa2_ref[...]) + ba_ref[...]
        gk_ref[0, rows, :] = _log_sigmoid(logit) * (1.0 / GLA_NORMALIZER)
        col = 0
        for out_ref, width, mul in ((q_ref, kd, q_scale), (k_ref, kd, None),
                                    (v_ref, vd, None), (z_ref, vd, None)):
            for c0 in range(0, width, COL_CHUNK):
                acc = _dot(h, w_ref[:, col + c0:col + c0 + COL_CHUNK])
                if mul is not None:
                    acc = acc * mul
                out_ref[0, rows, c0:c0 + COL_CHUNK] = acc.astype(out_ref.dtype)
            col += width


def _gla_in(x, g, mod, tail, w_in, w_a2, b_a):
    bsz, seq, d = x.shape
    kd = w_a2.shape[1]
    vd = (w_in.shape[1] - 2 * kd - GLA_RANK) // 2
    tm = min(ROW_TILE * ROW_TILES_PER_STEP, seq)
    w = jnp.pad(w_in, ((0, 0), (0, LANES - GLA_RANK))).astype(BF16)
    wa2 = jnp.pad(w_a2, ((0, LANES - GLA_RANK), (0, 0))).astype(BF16)
    dk = kd // GLA_HEADS
    kern = functools.partial(_gla_in_kernel, kd=kd, vd=vd, q_scale=dk ** -0.5)
    act = lambda width, dtype: jax.ShapeDtypeStruct((bsz, seq, width), dtype)
    return _in_proj_call(
        kern, x, g, mod, tail, [w, wa2, b_a.reshape(1, kd)],
        out_specs=[_row_spec(tm, kd), _row_spec(tm, kd), _row_spec(tm, vd),
                   _row_spec(tm, vd), _row_spec(tm, kd)],
        out_shape=[act(kd, BF16), act(kd, BF16), act(vd, BF16), act(vd, BF16), act(kd, F32)],
        name="gla_in_proj", tm=tm)


def _gla_tables(c):
    tri = np.tril(np.ones((c, c), np.float32))
    rows = [tri, 1.0 - tri]
    n_levels = int(math.log2(c))
    level = np.full((c, c), -1, np.int32)
    idx = np.arange(c)
    for l in range(n_levels):
        blk = c >> l
        half = blk // 2
        mid = (idx // blk) * blk + half - 1
        rows.append(tri - tri[mid])
        same = (idx[:, None] // blk) == (idx[None, :] // blk)
        upper = (idx[:, None] % blk) >= half
        lower = (idx[None, :] % blk) < half
        level[same & upper & lower] = l
    level[idx, idx] = n_levels
    return np.concatenate(rows, axis=0), level, n_levels


def _gla_scan_kernel(q_ref, k_ref, v_ref, gk_ref, wf_ref, ws_ref, lvl_ref, gh_ref, o_ref, st_ref,
                     *, fast_chunk, safe_chunk, n_levels, dk, dv):
    tb, kd = gk_ref.shape[1], gk_ref.shape[2]

    @pl.when(pl.program_id(1) == 0)
    def _():
        st_ref[...] = jnp.zeros_like(st_ref)

    def head_chunk(hd, rows, chunk, safe):
        kc = slice(hd * dk, (hd + 1) * dk)
        vc = slice(hd * dv, (hd + 1) * dv)
        w = ws_ref[...] if safe else wf_ref[...]
        r = _dot(w, jnp.concatenate(_split2(gk_ref[0, rows, kc]), axis=0))
        b = r[0:chunk]
        rev = r[chunk:2 * chunk]
        qf = q_ref[0, rows, kc].astype(F32)
        kf = k_ref[0, rows, kc].astype(F32)
        v = v_ref[0, rows, vc]
        st = st_ref[hd]
        q_dec = (qf * jnp.exp(b)).astype(BF16)

        if safe:
            lvl = lvl_ref[...]
            att = jnp.zeros((chunk, chunk), F32)
            for l in range(n_levels + 1):
                if l < n_levels:
                    e = jnp.exp(-jnp.abs(r[(2 + l) * chunk:(3 + l) * chunk]))
                    ql, kl = (qf * e).astype(BF16), (kf * e).astype(BF16)
                else:
                    ql, kl = qf.astype(BF16), kf.astype(BF16)
                att = jnp.where(lvl == l, _dot_nt(ql, kl), att)
        else:
            causal = (lax.broadcasted_iota(jnp.int32, (chunk, chunk), 0)
                      >= lax.broadcasted_iota(jnp.int32, (chunk, chunk), 1))
            att = jnp.where(causal, _dot_nt(q_dec, (kf * jnp.exp(-b)).astype(BF16)), 0.0)

        o = _dot_nt(q_dec, st.astype(BF16)) + _dot(att.astype(BF16), v)
        o_ref[0, rows, vc] = (_rms(o) * gh_ref[:, vc]).astype(o_ref.dtype)
        k_rev = (kf * jnp.exp(rev)).astype(BF16)
        st_ref[hd] = st * jnp.exp(b[chunk - 1:chunk, :]) + _dot_tn(v, k_rev)

    def run(chunk, safe):
        def body(ci, carry):
            rows = pl.ds(pl.multiple_of(ci * chunk, chunk), chunk)
            for hd in range(GLA_HEADS):
                head_chunk(hd, rows, chunk, safe)
            return carry
        lax.fori_loop(0, tb // chunk, body, 0, unroll=2)

    chunk_sums = jnp.sum(gk_ref[0].reshape(tb // fast_chunk, fast_chunk, kd), axis=1)
    fast_ok = jnp.min(chunk_sums) >= -GLA_DECAY_GUARD

    @pl.when(fast_ok)
    def _():
        run(fast_chunk, safe=False)

    @pl.when(jnp.logical_not(fast_ok))
    def _():
        run(safe_chunk, safe=True)


def _gla_scan(q, k, v, gk, g_head):
    bsz, seq, kd = q.shape
    vd = v.shape[-1]
    dk, dv = kd // GLA_HEADS, vd // GLA_HEADS
    tb = min(GLA_BLOCK, seq)
    ws_np, lvl_np, n_levels = _gla_tables(GLA_CHUNK)
    wf_np = _gla_tables(GLA_FAST_CHUNK)[0][:2 * GLA_FAST_CHUNK]
    stack2 = lambda t: jnp.asarray(np.concatenate([t, t], axis=1), BF16)
    wf, ws = stack2(wf_np), stack2(ws_np)
    kern = functools.partial(_gla_scan_kernel, fast_chunk=GLA_FAST_CHUNK, safe_chunk=GLA_CHUNK,
                             n_levels=n_levels, dk=dk, dv=dv)
    return pl.pallas_call(
        kern,
        grid=(bsz, seq // tb),
        in_specs=[_row_spec(tb, kd), _row_spec(tb, kd), _row_spec(tb, vd), _row_spec(tb, kd),
                  _full(wf.shape), _full(ws.shape), _full(lvl_np.shape), _full((1, vd))],
        out_specs=_row_spec(tb, vd),
        out_shape=jax.ShapeDtypeStruct((bsz, seq, vd), BF16),
        scratch_shapes=[pltpu.VMEM((GLA_HEADS, dv, dk), F32)],
        compiler_params=pltpu.CompilerParams(dimension_semantics=("arbitrary", "arbitrary")),
        name="gla_scan",
    )(q, k, v, gk, wf, ws, jnp.asarray(lvl_np), g_head.reshape(1, vd))


def _sgu_in_kernel(*refs, has_tail, width):
    w_ref, lng_ref, lnb_ref, u_ref, v_ref, z_ref, vtmp_ref = _kernel_refs(refs, has_tail, 1)
    tiles = _row_tiles(refs)
    inputs = [_layer_input(refs, has_tail, rows, n_scratch=1) for rows in tiles]
    for rows, h in zip(tiles, inputs):
        for c0 in range(0, width, COL_CHUNK):
            vtmp_ref[rows, c0:c0 + COL_CHUNK] = jax.nn.gelu(
                _dot(h, w_ref[:, width + c0:width + c0 + COL_CHUNK]))
        v = vtmp_ref[rows, :]
        mu = jnp.mean(v, axis=-1, keepdims=True)
        vc = v - mu
        var = jnp.mean(vc * vc, axis=-1, keepdims=True)
        v_ref[0, rows, :] = (vc * lax.rsqrt(var + EPS) * lng_ref[...] + lnb_ref[...]
                             ).astype(v_ref.dtype)
        for c0 in range(0, width, COL_CHUNK):
            cols = slice(c0, c0 + COL_CHUNK)
            u_ref[0, rows, cols] = jax.nn.gelu(_dot(h, w_ref[:, cols])).astype(u_ref.dtype)
            z_ref[0, rows, cols] = _dot(h, w_ref[:, 2 * width + c0:2 * width + c0 + COL_CHUNK]
                                        ).astype(z_ref.dtype)


def _sgu_in(x, g, mod, tail, w_in, ln_g, ln_b):
    bsz, seq, d = x.shape
    width = w_in.shape[1] // 3
    tm = min(ROW_TILE * ROW_TILES_PER_STEP, seq)
    kern = functools.partial(_sgu_in_kernel, width=width)
    act = jax.ShapeDtypeStruct((bsz, seq, width), BF16)
    return _in_proj_call(
        kern, x, g, mod, tail,
        [w_in.astype(BF16), ln_g.reshape(1, width), ln_b.reshape(1, width)],
        out_specs=[_row_spec(tm, width)] * 3, out_shape=[act, act, act],
        name="sgu_in_proj", tm=tm, scratch=[pltpu.VMEM((tm, width), F32)])


def _sgu_mix_kernel(u_ref, v_ref, ws_ref, bs_ref, o_ref, *, chunk, gdim, n_chunks):
    for g in range(SGU_GROUPS):
        cols = slice(g * gdim, (g + 1) * gdim)
        w = ws_ref[g]
        bias = bs_ref[g]
        bias = jnp.concatenate([bias] * (gdim // LANES), axis=1)
        for c in range(n_chunks):
            rows = slice(c * chunk, (c + 1) * chunk)
            mixed = _dot(w, v_ref[0, rows, cols]) + bias
            o_ref[0, rows, cols] = (u_ref[0, rows, cols].astype(F32) * mixed).astype(o_ref.dtype)


def _sgu_mix(u, v, w_s, b_s):
    bsz, seq, width = u.shape
    gdim = width // SGU_GROUPS
    tm = min(ROW_TILE, seq)
    chunk = SGU_CHUNK
    causal = jnp.tril(jnp.ones((chunk, chunk), bool))
    w_causal = jnp.where(causal[None], w_s, 0.0).astype(BF16)
    bias = jnp.broadcast_to(b_s[:, :, None], (SGU_GROUPS, chunk, LANES))
    kern = functools.partial(_sgu_mix_kernel, chunk=chunk, gdim=gdim, n_chunks=tm // chunk)
    return pl.pallas_call(
        kern,
        grid=(bsz, seq // tm),
        in_specs=[_row_spec(tm, width), _row_spec(tm, width),
                  _full((SGU_GROUPS, chunk, chunk)), _full((SGU_GROUPS, chunk, LANES))],
        out_specs=_row_spec(tm, width),
        out_shape=jax.ShapeDtypeStruct((bsz, seq, width), BF16),
        name="sgu_mix",
    )(u, v, w_causal, bias)


def _fox_tables(n_heads, dh):
    width = n_heads * dh
    n_terms = FOX_TERMS
    assert n_terms * n_heads <= LANES and 2 * n_terms + 1 <= dh
    pq = np.zeros((LANES, width), np.float32)
    pk = np.zeros((LANES, width), np.float32)
    cq = np.zeros((1, width), np.float32)
    ck = np.zeros((1, width), np.float32)
    shift_slot = np.zeros((1, width), np.float32)
    for h in range(n_heads):
        base = (h // 2) * 2 * dh + (dh if h % 2 == 0 else 0)
        for p in range(n_terms):
            pq[p * n_heads + h, base + p] = 1.0
            cq[0, base + n_terms + p] = 1.0
            ck[0, base + p] = 1.0
            pk[p * n_heads + h, base + n_terms + p] = -1.0
        cq[0, base + 2 * n_terms] = 1.0
        shift_slot[0, base + 2 * n_terms] = 1.0
    return pq, pk, cq, ck, shift_slot


def _fox_in_kernel(*refs, has_tail, ad, dh, n_heads):
    (w_ref, gq_ref, gk_ref, bf_ref, tri_ref, pq_ref, pk_ref, cq_ref, ck_ref,
     qt_ref, ka_ref, vt_ref, z_ref, carry_ref) = _kernel_refs(refs, has_tail, 1)
    h = _layer_input(refs, has_tail, slice(None), n_scratch=1)

    @pl.when(pl.program_id(1) == 0)
    def _():
        carry_ref[...] = jnp.zeros_like(carry_ref)

    tm = h.shape[0]
    pair = 2 * dh
    lane = lax.broadcasted_iota(jnp.int32, (tm, pair), 1)
    low = lane < dh

    def half_rms(x):
        sq = x * x
        s_lo = jnp.sum(jnp.where(low, sq, 0.0), axis=-1, keepdims=True)
        s_hi = jnp.sum(jnp.where(low, 0.0, sq), axis=-1, keepdims=True)
        return x * lax.rsqrt(jnp.where(low, s_lo, s_hi) * (1.0 / dh) + EPS)

    f_logit = _dot(h, w_ref[:, 4 * ad:4 * ad + LANES])
    tri = tri_ref[...]
    lf_hi, lf_mid, lf_lo = _split3(_log_sigmoid(f_logit + bf_ref[...]))
    f_cum = _dot(tri, lf_hi) + _dot(tri, lf_mid) + _dot(tri, lf_lo) + carry_ref[...]
    carry_ref[...] = f_cum[tm - 1:tm, :]
    term_list = _split3(f_cum * LOG2E)
    terms = term_list[-1]
    for p in range(FOX_TERMS - 2, -1, -1):
        terms = jnp.where(lane < (p + 1) * n_heads, term_list[p], terms)
    aug_q = _dot(terms, pq_ref[...]) + cq_ref[...]
    aug_k = _dot(terms, pk_ref[...]) + ck_ref[...]

    for c0 in range(0, ad, COL_CHUNK):
        acc_q = _dot(h, w_ref[:, c0:c0 + COL_CHUNK])
        acc_k = _dot(h, w_ref[:, ad + c0:ad + c0 + COL_CHUNK])
        acc_v = _dot(h, w_ref[:, 2 * ad + c0:2 * ad + c0 + COL_CHUNK])
        z_ref[0, :, c0:c0 + COL_CHUNK] = _dot(
            h, w_ref[:, 3 * ad + c0:3 * ad + c0 + COL_CHUNK]).astype(z_ref.dtype)
        for g0 in range(0, COL_CHUNK, pair):
            cols = slice(g0, g0 + pair)
            gcols = slice(c0 + g0, c0 + g0 + pair)
            head = (c0 + g0) // dh
            qn = half_rms(acc_q[:, cols]) * gq_ref[...]
            kn = half_rms(acc_k[:, cols]) * gk_ref[...]
            aq, ak = aug_q[:, gcols], aug_k[:, gcols]
            qt_ref[0, head] = jnp.where(low, qn, aq).T.astype(qt_ref.dtype)
            qt_ref[0, head + 1] = jnp.where(low, aq, qn).T.astype(qt_ref.dtype)
            ka_ref[0, head] = jnp.where(low, kn, ak).astype(ka_ref.dtype)
            ka_ref[0, head + 1] = jnp.where(low, ak, kn).astype(ka_ref.dtype)
            v_t = acc_v[:, cols].T.astype(vt_ref.dtype)
            vt_ref[0, head] = v_t[:dh]
            vt_ref[0, head + 1] = v_t[dh:]


def _fox_in(x, g, mod, tail, w_in, b_f, g_q, g_k, shift):
    bsz, seq, d = x.shape
    nh, dh = FOX_HEADS, FOX_DH
    ad = nh * dh
    tm = min(ROW_TILE, seq)
    pad = LANES - FOX_TERMS * nh
    w = jnp.concatenate([w_in[:, :4 * ad]] + [w_in[:, 4 * ad:]] * FOX_TERMS
                        + [jnp.zeros((d, pad), w_in.dtype)], axis=1).astype(BF16)
    bf = jnp.pad(jnp.tile(b_f, FOX_TERMS), (0, pad)).reshape(1, LANES)
    gq2 = jnp.tile(g_q * (dh ** -0.5 * LOG2E), 2).reshape(1, 2 * dh)
    gk2 = jnp.tile(g_k, 2).reshape(1, 2 * dh)
    tri = jnp.asarray(np.tril(np.ones((tm, tm), np.float32)), BF16)
    pq, pk, cq, ck, shift_slot = _fox_tables(nh, dh)
    ck = jnp.asarray(ck) - shift * jnp.asarray(shift_slot)
    kern = functools.partial(_fox_in_kernel, ad=ad, dh=dh, n_heads=nh)
    return _in_proj_call(
        kern, x, g, mod, tail,
        [w, gq2, gk2, bf, tri, jnp.asarray(pq, BF16), jnp.asarray(pk, BF16), jnp.asarray(cq), ck],
        out_specs=[
            pl.BlockSpec((1, nh, LANES, tm), lambda b, s: (b, 0, 0, s)),
            pl.BlockSpec((1, nh, tm, LANES), lambda b, s: (b, 0, s, 0)),
            pl.BlockSpec((1, nh, dh, tm), lambda b, s: (b, 0, 0, s)),
            _row_spec(tm, ad)],
        out_shape=[jax.ShapeDtypeStruct((bsz, nh, LANES, seq), BF16),
                   jax.ShapeDtypeStruct((bsz, nh, seq, LANES), BF16),
                   jax.ShapeDtypeStruct((bsz, nh, dh, seq), BF16),
                   jax.ShapeDtypeStruct((bsz, seq, ad), BF16)],
        name="fox_in_proj", tm=tm, scratch=[pltpu.VMEM((1, LANES), F32)])


def _fox_attn_kernel(qt_ref, ka_ref, vt_ref, o_ref, s0_ref, s1_ref, m_ref, acc_ref,
                     *, tq, tk, dh):
    qi = pl.program_id(2)
    hp = qt_ref.shape[1]
    heads = range(hp)
    sum_rows = jnp.ones((FOX_SUM_ROWS, tk), BF16)

    def scores(hh, kv):
        k0 = pl.multiple_of(kv * tk, tk)
        return _dot(ka_ref[0, hh, pl.ds(k0, tk), :], qt_ref[0, hh])

    def consume(hh, s_ref, kv, masked):
        k0 = pl.multiple_of(kv * tk, tk)
        if masked:
            visible = (lax.broadcasted_iota(jnp.int32, (tk, tq), 0)
                       <= lax.broadcasted_iota(jnp.int32, (tk, tq), 1))
            read = lambda: jnp.where(visible, s_ref[hh], -jnp.inf)
        else:
            read = lambda: s_ref[hh]
        m = m_ref[hh]
        m_new = jnp.maximum(m, jnp.max(read(), axis=0, keepdims=True))
        p = jnp.exp2(read() - m_new).astype(BF16)
        v_sum = jnp.concatenate([vt_ref[0, hh, :, pl.ds(k0, tk)], sum_rows], axis=0)
        acc_ref[hh] = acc_ref[hh] * jnp.exp2(m - m_new) + _dot(v_sum, p)
        m_ref[hh] = m_new

    m_ref[...] = jnp.full(m_ref.shape, -jnp.inf, F32)
    acc_ref[...] = jnp.zeros(acc_ref.shape, F32)
    for hh in heads:
        s0_ref[hh] = scores(hh, 0)

    def two_blocks(i, carry):
        j = 2 * i
        for hh in heads:
            s1_ref[hh] = scores(hh, j + 1)
            consume(hh, s0_ref, j, masked=False)
        for hh in heads:
            s0_ref[hh] = scores(hh, j + 2)
            consume(hh, s1_ref, j + 1, masked=False)
        return carry

    lax.fori_loop(0, qi // 2, two_blocks, 0)

    @pl.when(qi % 2 == 1)
    def _():
        for hh in heads:
            s1_ref[hh] = scores(hh, qi)
            consume(hh, s0_ref, qi - 1, masked=False)
        for hh in heads:
            consume(hh, s1_ref, qi, masked=True)

    @pl.when(qi % 2 == 0)
    def _():
        for hh in heads:
            consume(hh, s0_ref, qi, masked=True)

    out_t = jnp.concatenate([acc_ref[hh, :dh] / acc_ref[hh, dh:dh + 1] for hh in heads], axis=0)
    o_ref[0] = out_t.T.astype(o_ref.dtype)


def _fox_attn_shifted_kernel(qt_ref, ka_ref, vt_ref, o_ref, p0_ref, p1_ref, acc_ref,
                             *, tq, tk, dh):
    qi = pl.program_id(2)
    hp = qt_ref.shape[1]
    bufs = (p0_ref, p1_ref)
    sum_rows = jnp.ones((FOX_SUM_ROWS, tk), BF16)

    def probs(hh, kv, masked):
        k0 = pl.multiple_of(kv * tk, tk)
        p = jnp.exp2(_dot(ka_ref[0, hh, pl.ds(k0, tk), :], qt_ref[0, hh]))
        if masked:
            visible = (lax.broadcasted_iota(jnp.int32, (tk, tq), 0)
                       <= lax.broadcasted_iota(jnp.int32, (tk, tq), 1))
            p = jnp.where(visible, p, 0.0)
        return p.astype(BF16)

    def fold(hh, kv, p_ref):
        k0 = pl.multiple_of(kv * tk, tk)
        v_sum = jnp.concatenate([vt_ref[0, hh, :, pl.ds(k0, tk)], sum_rows], axis=0)
        acc_ref[hh] += _dot(v_sum, p_ref[...])

    def section(blocks, masked, following):
        units = [(kv, hh) for kv in blocks for hh in range(hp)]
        for n, (kv, hh) in enumerate(units):
            nxt = units[n + 1] if n + 1 < len(units) else (following, 0)
            if nxt[0] is not None:
                bufs[(n + 1) % 2][...] = probs(nxt[1], nxt[0], masked and n + 1 < len(units))
            fold(hh, kv, bufs[n % 2])

    acc_ref[...] = jnp.zeros(acc_ref.shape, F32)
    p0_ref[...] = probs(0, qi, masked=True)
    section([qi], masked=True, following=0)

    def two_blocks(i, carry):
        section([2 * i, 2 * i + 1], masked=False, following=2 * i + 2)
        return carry

    lax.fori_loop(0, qi // 2, two_blocks, 0)

    @pl.when(qi % 2 == 1)
    def _():
        section([qi - 1], masked=False, following=None)

    out_t = jnp.concatenate([acc_ref[hh, :dh] / acc_ref[hh, dh:dh + 1] for hh in range(hp)],
                            axis=0)
    o_ref[0] = out_t.T.astype(o_ref.dtype)


def _fox_attn(qt, ka, vt, shifted):
    bsz, nh, _, seq = qt.shape
    dh = FOX_DH
    tq, tk = min(FOX_TQ, seq), min(FOX_TK, seq)
    hp = FOX_HEADS_PER_STEP["shifted" if shifted else "online"]
    assert tq == tk and (hp * dh) % LANES == 0
    acc = pltpu.VMEM((hp, dh + FOX_SUM_ROWS, tq), F32)
    if shifted:
        kern = _fox_attn_shifted_kernel
        scratch = [pltpu.VMEM((tk, tq), BF16), pltpu.VMEM((tk, tq), BF16), acc]
    else:
        kern = _fox_attn_kernel
        scratch = [pltpu.VMEM((hp, tk, tq), F32), pltpu.VMEM((hp, tk, tq), F32),
                   pltpu.VMEM((hp, 1, tq), F32), acc]
    return pl.pallas_call(
        functools.partial(kern, tq=tq, tk=tk, dh=dh),
        grid=(bsz, nh // hp, seq // tq),
        in_specs=[
            pl.BlockSpec((1, hp, LANES, tq), lambda b, h, s: (b, h, 0, s)),
            pl.BlockSpec((1, hp, seq, LANES), lambda b, h, s: (b, h, 0, 0)),
            pl.BlockSpec((1, hp, dh, seq), lambda b, h, s: (b, h, 0, 0)),
        ],
        out_specs=pl.BlockSpec((1, tq, hp * dh), lambda b, h, s: (b, s, h)),
        out_shape=jax.ShapeDtypeStruct((bsz, seq, nh * dh), BF16),
        scratch_shapes=scratch,
        name="fox_attn_shifted" if shifted else "fox_attn",
    )(qt, ka, vt)


def _out_kernel(o_ref, z_ref, w_ref, g_ref, gate_ref, x_ref, y_ref):
    y_ref[0] = _layer_tail(x_ref[0], slice(None), o_ref, z_ref, w_ref, g_ref, gate_ref)


def _out_proj(x, tail):
    o, z, w_out, g_post, mod = tail
    bsz, seq, d = x.shape
    width = o.shape[-1]
    tm = min(ROW_TILE, seq)
    return pl.pallas_call(
        _out_kernel,
        grid=(bsz, seq // tm),
        in_specs=[_row_spec(tm, width), _row_spec(tm, width), _full((width, d)), _full((1, d)),
                  _mod_spec(d, 2), _row_spec(tm, d)],
        out_specs=_row_spec(tm, d),
        out_shape=jax.ShapeDtypeStruct((bsz, seq, d), F32),
        name="out_proj",
    )(o, z, w_out.astype(BF16), g_post.reshape(1, d), mod, x)


def _gla_layer(x, g_pre, mod, tail, w_in, w_a2, b_a, g_head):
    q, k, v, z, gk, *x_new = _gla_in(x, g_pre, mod, tail, w_in, w_a2, b_a)
    return (x_new or [x])[0], _gla_scan(q, k, v, gk, g_head), z


def _sgu_layer(x, g_pre, mod, tail, w_in, ln_g, ln_b, w_s, b_s):
    u, v, z, *x_new = _sgu_in(x, g_pre, mod, tail, w_in, ln_g, ln_b)
    return (x_new or [x])[0], _sgu_mix(u, v, w_s, b_s), z


def _fox_layer(x, g_pre, mod, tail, w_in, b_f, g_q, g_k):
    bound = FOX_DH * (FOX_DH ** -0.5 * LOG2E) * jnp.max(jnp.abs(g_q)) * jnp.max(jnp.abs(g_k))
    shift = 1.02 * bound
    qt, ka, vt, z, *x_new = _fox_in(x, g_pre, mod, tail, w_in, b_f, g_q, g_k, shift)
    o = lax.cond(shift <= FOX_MAX_SHIFT,
                 functools.partial(_fox_attn, shifted=True),
                 functools.partial(_fox_attn, shifted=False), qt, ka, vt)
    return (x_new or [x])[0], o, z


def kernel(x, c, norm_pre_g, norm_post_g, w_mod, b_mod, gla_w_in, gla_w_a2, gla_b_a, gla_g_head, gla_w_out, sgu_w_in, sgu_ln_g, sgu_ln_b, sgu_w_s, sgu_b_s, sgu_w_out, fox_w_in, fox_b_f, fox_g_q, fox_g_k, fox_w_out):
    depth = w_mod.shape[0]
    bsz = x.shape[0]
    mod_all = _modulation(c, w_mod, b_mod)
    tail = None
    for i in range(depth):
        mod = mod_all[i].reshape(bsz, 1, -1)
        kind, j = i % N_MIXERS, i // N_MIXERS
        if kind == 0:
            x, o, z = _gla_layer(x, norm_pre_g[i], mod, tail, gla_w_in[j], gla_w_a2[j],
                                 gla_b_a[j], gla_g_head[j])
            w_out = gla_w_out[j]
        elif kind == 1:
            x, o, z = _sgu_layer(x, norm_pre_g[i], mod, tail, sgu_w_in[j], sgu_ln_g[j],
                                 sgu_ln_b[j], sgu_w_s[j], sgu_b_s[j])
            w_out = sgu_w_out[j]
        else:
            x, o, z = _fox_layer(x, norm_pre_g[i], mod, tail, fox_w_in[j], fox_b_f[j],
                                 fox_g_q[j], fox_g_k[j])
            w_out = fox_w_out[j]
        tail = (o, z, w_out, norm_post_g[i], mod)
    return _out_proj(x, tail)
```

```python
import functools
import math

import numpy as np
import jax
import jax.numpy as jnp
from jax import lax
from jax.experimental import pallas as pl
from jax.experimental.pallas import tpu as pltpu

EPS = 1e-6
N_MIXERS = 3

LANES = 128
ROW_TILE = 512
ROW_TILES_PER_STEP = 2
COL_CHUNK = 512

GLA_HEADS = 4
GLA_RANK = 16
GLA_NORMALIZER = 16.0
GLA_CHUNK = 64
GLA_FAST_CHUNK = 256
GLA_DECAY_GUARD = 70.0
GLA_BLOCK = 1024

SGU_GROUPS = 4
SGU_CHUNK = 128

FOX_HEADS = 16
FOX_DH = 64
FOX_TQ = 512
FOX_TK = 512
FOX_HEADS_PER_STEP = {"shifted": 8, "online": 4}
FOX_SUM_ROWS = 16
FOX_TERMS = 3
FOX_MAX_SHIFT = 50.0
LOG2E = math.log2(math.e)

BF16 = jnp.bfloat16
F32 = jnp.float32


def _dot(a, b):
    return jnp.dot(a, b, preferred_element_type=F32)


def _dot_nt(a, b):
    return lax.dot_general(a, b, (((1,), (1,)), ((), ())), preferred_element_type=F32)


def _dot_tn(a, b):
    return lax.dot_general(a, b, (((0,), (0,)), ((), ())), preferred_element_type=F32)


def _split2(x):
    hi = x.astype(BF16)
    lo = (x - hi.astype(F32)).astype(BF16)
    return hi, lo


def _split3(x):
    hi = x.astype(BF16)
    r = x - hi.astype(F32)
    mid = r.astype(BF16)
    lo = (r - mid.astype(F32)).astype(BF16)
    return hi, mid, lo


def _log_sigmoid(x):
    return jnp.minimum(x, 0.0) - jnp.log1p(jnp.exp(-jnp.abs(x)))


def _rms(x):
    return x * lax.rsqrt(jnp.mean(x * x, axis=-1, keepdims=True) + EPS)


def _mod_kernel(c_ref, w_ref, b_ref, o_ref):
    cond = jax.nn.silu(c_ref[...])
    o_ref[0] = _dot(cond, w_ref[0]) + b_ref[0]


def _modulation(c, w_mod, b_mod):
    depth, d, d3 = w_mod.shape
    bsz = c.shape[0]
    nblk = d3 // d
    return pl.pallas_call(
        _mod_kernel,
        grid=(depth, nblk),
        in_specs=[
            pl.BlockSpec((bsz, d), lambda i, j: (0, 0)),
            pl.BlockSpec((1, d, d), lambda i, j: (i, 0, j)),
            pl.BlockSpec((1, 1, d), lambda i, j: (i, 0, j)),
        ],
        out_specs=pl.BlockSpec((1, bsz, d), lambda i, j: (i, 0, j)),
        out_shape=jax.ShapeDtypeStruct((depth, bsz, d3), F32),
        name="adaln_mod",
    )(c, w_mod, b_mod.reshape(depth, 1, d3))


N_HEAD_REFS = 4
N_TAIL_REFS = 5


def _layer_tail(x, rows, o_ref, z_ref, w_ref, g_ref, gate_ref):
    gated = o_ref[0, rows, :].astype(F32) * jax.nn.silu(z_ref[0, rows, :].astype(F32))
    y = _dot(gated.astype(BF16), w_ref[...])
    return x + gate_ref[0] * (_rms(y) * g_ref[...])


def _kernel_refs(refs, has_tail, n_scratch=0):
    rest = list(refs[N_HEAD_REFS:])
    if has_tail:
        del rest[len(rest) - n_scratch - 1]
        rest = rest[N_TAIL_REFS:]
    return rest


def _row_tiles(refs):
    rows = refs[0].shape[1]
    return [slice(r, r + ROW_TILE) for r in range(0, rows, min(ROW_TILE, rows))]


def _layer_input(refs, has_tail, rows, n_scratch=0):
    x_ref, g_ref, shift_ref, scale_ref = refs[:N_HEAD_REFS]
    x = x_ref[0, rows, :]
    if has_tail:
        x = _layer_tail(x, rows, *refs[N_HEAD_REFS:N_HEAD_REFS + N_TAIL_REFS])
        refs[len(refs) - n_scratch - 1][0, rows, :] = x
    h = _rms(x) * g_ref[...]
    h = h * (1.0 + scale_ref[0]) + shift_ref[0]
    return h.astype(BF16)


def _full(shape):
    nd = len(shape)
    return pl.BlockSpec(shape, lambda b, s: (0,) * nd, pipeline_mode=pl.Buffered(1))


def _row_spec(tm, width, col=0):
    return pl.BlockSpec((1, tm, width), lambda b, s: (b, s, col))


def _mod_spec(d, part):
    return pl.BlockSpec((1, 1, d), lambda b, s: (b, 0, part))


def _in_proj_call(kern, x, g, mod, tail, consts, out_specs, out_shape, name, tm, scratch=()):
    bsz, seq, d = x.shape
    operands = [x, g.reshape(1, d), mod, mod]
    in_specs = [_row_spec(tm, d), _full((1, d)), _mod_spec(d, 0), _mod_spec(d, 1)]
    if tail is not None:
        o, z, w_out, g_post, mod_prev = tail
        width = o.shape[-1]
        operands += [o, z, w_out.astype(BF16), g_post.reshape(1, d), mod_prev]
        in_specs += [_row_spec(tm, width), _row_spec(tm, width), _full((width, d)),
                     _full((1, d)), _mod_spec(d, 2)]
        out_specs = list(out_specs) + [_row_spec(tm, d)]
        out_shape = list(out_shape) + [jax.ShapeDtypeStruct((bsz, seq, d), F32)]
    operands += list(consts)
    in_specs += [_full(c.shape) for c in consts]
    return pl.pallas_call(
        functools.partial(kern, has_tail=tail is not None),
        grid=(bsz, seq // tm),
        in_specs=in_specs, out_specs=out_specs, out_shape=out_shape,
        scratch_shapes=list(scratch),
        compiler_params=pltpu.CompilerParams(dimension_semantics=("arbitrary", "arbitrary")),
        name=name,
    )(*operands)


def _gla_in_kernel(*refs, has_tail, kd, vd, q_scale):
    w_ref, wa2_ref, ba_ref, q_ref, k_ref, v_ref, z_ref, gk_ref = _kernel_refs(refs, has_tail)
    n_main = 2 * kd + 2 * vd
    tiles = _row_tiles(refs)
    inputs = [_layer_input(refs, has_tail, rows) for rows in tiles]
    for rows, h in zip(tiles, inputs):
        a_low = _dot(h, w_ref[:, n_main:n_main + LANES])
        logit = _dot(a_low.astype(BF16), wa2_ref[...]) + ba_ref[...]
        gk_ref[0, rows, :] = _log_sigmoid(logit) * (1.0 / GLA_NORMALIZER)
        col = 0
        for out_ref, width, mul in ((q_ref, kd, q_scale), (k_ref, kd, None),
                                    (v_ref, vd, None), (z_ref, vd, None)):
            for c0 in range(0, width, COL_CHUNK):
                acc = _dot(h, w_ref[:, col + c0:col + c0 + COL_CHUNK])
                if mul is not None:
                    acc = acc * mul
                out_ref[0, rows, c0:c0 + COL_CHUNK] = acc.astype(out_ref.dtype)
            col += width


def _gla_in(x, g, mod, tail, w_in, w_a2, b_a):
    bsz, seq, d = x.shape
    kd = w_a2.shape[1]
    vd = (w_in.shape[1] - 2 * kd - GLA_RANK) // 2
    tm = min(ROW_TILE * ROW_TILES_PER_STEP, seq)
    w = jnp.pad(w_in, ((0, 0), (0, LANES - GLA_RANK))).astype(BF16)
    wa2 = jnp.pad(w_a2, ((0, LANES - GLA_RANK), (0, 0))).astype(BF16)
    dk = kd // GLA_HEADS
    kern = functools.partial(_gla_in_kernel, kd=kd, vd=vd, q_scale=dk ** -0.5)
    act = lambda width, dtype: jax.ShapeDtypeStruct((bsz, seq, width), dtype)
    return _in_proj_call(
        kern, x, g, mod, tail, [w, wa2, b_a.reshape(1, kd)],
        out_specs=[_row_spec(tm, kd), _row_spec(tm, kd), _row_spec(tm, vd),
                   _row_spec(tm, vd), _row_spec(tm, kd)],
        out_shape=[act(kd, BF16), act(kd, BF16), act(vd, BF16), act(vd, BF16), act(kd, F32)],
        name="gla_in_proj", tm=tm)


def _gla_tables(c):
    tri = np.tril(np.ones((c, c), np.float32))
    rows = [tri, 1.0 - tri]
    n_levels = int(math.log2(c))
    level = np.full((c, c), -1, np.int32)
    idx = np.arange(c)
    for l in range(n_levels):
        blk = c >> l
        half = blk // 2
        mid = (idx // blk) * blk + half - 1
        rows.append(tri - tri[mid])
        same = (idx[:, None] // blk) == (idx[None, :] // blk)
        upper = (idx[:, None] % blk) >= half
        lower = (idx[None, :] % blk) < half
        level[same & upper & lower] = l
    level[idx, idx] = n_levels
    return np.concatenate(rows, axis=0), level, n_levels


def _gla_scan_kernel(q_ref, k_ref, v_ref, gk_ref, wf_ref, ws_ref, lvl_ref, gh_ref, o_ref, st_ref,
                     *, fast_chunk, safe_chunk, n_levels, dk, dv):
    tb, kd = gk_ref.shape[1], gk_ref.shape[2]

    @pl.when(pl.program_id(1) == 0)
    def _():
        st_ref[...] = jnp.zeros_like(st_ref)

    def head_chunk(hd, rows, chunk, safe):
        kc = slice(hd * dk, (hd + 1) * dk)
        vc = slice(hd * dv, (hd + 1) * dv)
        w = ws_ref[...] if safe else wf_ref[...]
        r = _dot(w, jnp.concatenate(_split2(gk_ref[0, rows, kc]), axis=0))
        b = r[0:chunk]
        rev = r[chunk:2 * chunk]
        qf = q_ref[0, rows, kc].astype(F32)
        kf = k_ref[0, rows, kc].astype(F32)
        v = v_ref[0, rows, vc]
        st = st_ref[hd]
        q_dec = (qf * jnp.exp(b)).astype(BF16)

        if safe:
            lvl = lvl_ref[...]
            att = jnp.zeros((chunk, chunk), F32)
            for l in range(n_levels + 1):
                if l < n_levels:
                    e = jnp.exp(-jnp.abs(r[(2 + l) * chunk:(3 + l) * chunk]))
                    ql, kl = (qf * e).astype(BF16), (kf * e).astype(BF16)
                else:
                    ql, kl = qf.astype(BF16), kf.astype(BF16)
                att = jnp.where(lvl == l, _dot_nt(ql, kl), att)
        else:
            causal = (lax.broadcasted_iota(jnp.int32, (chunk, chunk), 0)
                      >= lax.broadcasted_iota(jnp.int32, (chunk, chunk), 1))
            att = jnp.where(causal, _dot_nt(q_dec, (kf * jnp.exp(-b)).astype(BF16)), 0.0)

        o = _dot_nt(q_dec, st.astype(BF16)) + _dot(att.astype(BF16), v)
        o_ref[0, rows, vc] = (_rms(o) * gh_ref[:, vc]).astype(o_ref.dtype)
        k_rev = (kf * jnp.exp(rev)).astype(BF16)
        st_ref[hd] = st * jnp.exp(b[chunk - 1:chunk, :]) + _dot_tn(v, k_rev)

    def run(chunk, safe):
        def body(ci, carry):
            rows = pl.ds(pl.multiple_of(ci * chunk, chunk), chunk)
            for hd in range(GLA_HEADS):
                head_chunk(hd, rows, chunk, safe)
            return carry
        lax.fori_loop(0, tb // chunk, body, 0, unroll=2)

    chunk_sums = jnp.sum(gk_ref[0].reshape(tb // fast_chunk, fast_chunk, kd), axis=1)
    fast_ok = jnp.min(chunk_sums) >= -GLA_DECAY_GUARD

    @pl.when(fast_ok)
    def _():
        run(fast_chunk, safe=False)

    @pl.when(jnp.logical_not(fast_ok))
    def _():
        run(safe_chunk, safe=True)


def _gla_scan(q, k, v, gk, g_head):
    bsz, seq, kd = q.shape
    vd = v.shape[-1]
    dk, dv = kd // GLA_HEADS, vd // GLA_HEADS
    tb = min(GLA_BLOCK, seq)
    ws_np, lvl_np, n_levels = _gla_tables(GLA_CHUNK)
    wf_np = _gla_tables(GLA_FAST_CHUNK)[0][:2 * GLA_FAST_CHUNK]
    stack2 = lambda t: jnp.asarray(np.concatenate([t, t], axis=1), BF16)
    wf, ws = stack2(wf_np), stack2(ws_np)
    kern = functools.partial(_gla_scan_kernel, fast_chunk=GLA_FAST_CHUNK, safe_chunk=GLA_CHUNK,
                             n_levels=n_levels, dk=dk, dv=dv)
    return pl.pallas_call(
        kern,
        grid=(bsz, seq // tb),
        in_specs=[_row_spec(tb, kd), _row_spec(tb, kd), _row_spec(tb, vd), _row_spec(tb, kd),
                  _full(wf.shape), _full(ws.shape), _full(lvl_np.shape), _full((1, vd))],
        out_specs=_row_spec(tb, vd),
        out_shape=jax.ShapeDtypeStruct((bsz, seq, vd), BF16),
        scratch_shapes=[pltpu.VMEM((GLA_HEADS, dv, dk), F32)],
        compiler_params=pltpu.CompilerParams(dimension_semantics=("arbitrary", "arbitrary")),
        name="gla_scan",
    )(q, k, v, gk, wf, ws, jnp.asarray(lvl_np), g_head.reshape(1, vd))


def _sgu_in_kernel(*refs, has_tail, width):
    w_ref, lng_ref, lnb_ref, u_ref, v_ref, z_ref, vtmp_ref = _kernel_refs(refs, has_tail, 1)
    tiles = _row_tiles(refs)
    inputs = [_layer_input(refs, has_tail, rows, n_scratch=1) for rows in tiles]
    for rows, h in zip(tiles, inputs):
        for c0 in range(0, width, COL_CHUNK):
            vtmp_ref[rows, c0:c0 + COL_CHUNK] = jax.nn.gelu(
                _dot(h, w_ref[:, width + c0:width + c0 + COL_CHUNK]))
        v = vtmp_ref[rows, :]
        mu = jnp.mean(v, axis=-1, keepdims=True)
        vc = v - mu
        var = jnp.mean(vc * vc, axis=-1, keepdims=True)
        v_ref[0, rows, :] = (vc * lax.rsqrt(var + EPS) * lng_ref[...] + lnb_ref[...]
                             ).astype(v_ref.dtype)
        for c0 in range(0, width, COL_CHUNK):
            cols = slice(c0, c0 + COL_CHUNK)
            u_ref[0, rows, cols] = jax.nn.gelu(_dot(h, w_ref[:, cols])).astype(u_ref.dtype)
            z_ref[0, rows, cols] = _dot(h, w_ref[:, 2 * width + c0:2 * width + c0 + COL_CHUNK]
                                        ).astype(z_ref.dtype)


def _sgu_in(x, g, mod, tail, w_in, ln_g, ln_b):
    bsz, seq, d = x.shape
    width = w_in.shape[1] // 3
    tm = min(ROW_TILE * ROW_TILES_PER_STEP, seq)
    kern = functools.partial(_sgu_in_kernel, width=width)
    act = jax.ShapeDtypeStruct((bsz, seq, width), BF16)
    return _in_proj_call(
        kern, x, g, mod, tail,
        [w_in.astype(BF16), ln_g.reshape(1, width), ln_b.reshape(1, width)],
        out_specs=[_row_spec(tm, width)] * 3, out_shape=[act, act, act],
        name="sgu_in_proj", tm=tm, scratch=[pltpu.VMEM((tm, width), F32)])


def _sgu_mix_kernel(u_ref, v_ref, ws_ref, bs_ref, o_ref, *, chunk, gdim, n_chunks):
    for g in range(SGU_GROUPS):
        cols = slice(g * gdim, (g + 1) * gdim)
        w = ws_ref[g]
        bias = bs_ref[g]
        bias = jnp.concatenate([bias] * (gdim // LANES), axis=1)
        for c in range(n_chunks):
            rows = slice(c * chunk, (c + 1) * chunk)
            mixed = _dot(w, v_ref[0, rows, cols]) + bias
            o_ref[0, rows, cols] = (u_ref[0, rows, cols].astype(F32) * mixed).astype(o_ref.dtype)


def _sgu_mix(u, v, w_s, b_s):
    bsz, seq, width = u.shape
    gdim = width // SGU_GROUPS
    tm = min(ROW_TILE, seq)
    chunk = SGU_CHUNK
    causal = jnp.tril(jnp.ones((chunk, chunk), bool))
    w_causal = jnp.where(causal[None], w_s, 0.0).astype(BF16)
    bias = jnp.broadcast_to(b_s[:, :, None], (SGU_GROUPS, chunk, LANES))
    kern = functools.partial(_sgu_mix_kernel, chunk=chunk, gdim=gdim, n_chunks=tm // chunk)
    return pl.pallas_call(
        kern,
        grid=(bsz, seq // tm),
        in_specs=[_row_spec(tm, width), _row_spec(tm, width),
                  _full((SGU_GROUPS, chunk, chunk)), _full((SGU_GROUPS, chunk, LANES))],
        out_specs=_row_spec(tm, width),
        out_shape=jax.ShapeDtypeStruct((bsz, seq, width), BF16),
        name="sgu_mix",
    )(u, v, w_causal, bias)


def _fox_tables(n_heads, dh):
    width = n_heads * dh
    n_terms = FOX_TERMS
    assert n_terms * n_heads <= LANES and 2 * n_terms + 1 <= dh
    pq = np.zeros((LANES, width), np.float32)
    pk = np.zeros((LANES, width), np.float32)
    cq = np.zeros((1, width), np.float32)
    ck = np.zeros((1, width), np.float32)
    shift_slot = np.zeros((1, width), np.float32)
    for h in range(n_heads):
        base = (h // 2) * 2 * dh + (dh if h % 2 == 0 else 0)
        for p in range(n_terms):
            pq[p * n_heads + h, base + p] = 1.0
            cq[0, base + n_terms + p] = 1.0
            ck[0, base + p] = 1.0
            pk[p * n_heads + h, base + n_terms + p] = -1.0
        cq[0, base + 2 * n_terms] = 1.0
        shift_slot[0, base + 2 * n_terms] = 1.0
    return pq, pk, cq, ck, shift_slot


def _fox_in_kernel(*refs, has_tail, ad, dh, n_heads):
    (w_ref, gq_ref, gk_ref, bf_ref, tri_ref, pq_ref, pk_ref, cq_ref, ck_ref,
     qt_ref, ka_ref, vt_ref, z_ref, carry_ref) = _kernel_refs(refs, has_tail, 1)
    h = _layer_input(refs, has_tail, slice(None), n_scratch=1)

    @pl.when(pl.program_id(1) == 0)
    def _():
        carry_ref[...] = jnp.zeros_like(carry_ref)

    tm = h.shape[0]
    pair = 2 * dh
    lane = lax.broadcasted_iota(jnp.int32, (tm, pair), 1)
    low = lane < dh

    def half_rms(x):
        sq = x * x
        s_lo = jnp.sum(jnp.where(low, sq, 0.0), axis=-1, keepdims=True)
        s_hi = jnp.sum(jnp.where(low, 0.0, sq), axis=-1, keepdims=True)
        return x * lax.rsqrt(jnp.where(low, s_lo, s_hi) * (1.0 / dh) + EPS)

    f_logit = _dot(h, w_ref[:, 4 * ad:4 * ad + LANES])
    tri = tri_ref[...]
    lf_hi, lf_mid, lf_lo = _split3(_log_sigmoid(f_logit + bf_ref[...]))
    f_cum = _dot(tri, lf_hi) + _dot(tri, lf_mid) + _dot(tri, lf_lo) + carry_ref[...]
    carry_ref[...] = f_cum[tm - 1:tm, :]
    term_list = _split3(f_cum * LOG2E)
    terms = term_list[-1]
    for p in range(FOX_TERMS - 2, -1, -1):
        terms = jnp.where(lane < (p + 1) * n_heads, term_list[p], terms)
    aug_q = _dot(terms, pq_ref[...]) + cq_ref[...]
    aug_k = _dot(terms, pk_ref[...]) + ck_ref[...]

    for c0 in range(0, ad, COL_CHUNK):
        acc_q = _dot(h, w_ref[:, c0:c0 + COL_CHUNK])
        acc_k = _dot(h, w_ref[:, ad + c0:ad + c0 + COL_CHUNK])
        acc_v = _dot(h, w_ref[:, 2 * ad + c0:2 * ad + c0 + COL_CHUNK])
        z_ref[0, :, c0:c0 + COL_CHUNK] = _dot(
            h, w_ref[:, 3 * ad + c0:3 * ad + c0 + COL_CHUNK]).astype(z_ref.dtype)
        for g0 in range(0, COL_CHUNK, pair):
            cols = slice(g0, g0 + pair)
            gcols = slice(c0 + g0, c0 + g0 + pair)
            head = (c0 + g0) // dh
            qn = half_rms(acc_q[:, cols]) * gq_ref[...]
            kn = half_rms(acc_k[:, cols]) * gk_ref[...]
            aq, ak = aug_q[:, gcols], aug_k[:, gcols]
            qt_ref[0, head] = jnp.where(low, qn, aq).T.astype(qt_ref.dtype)
            qt_ref[0, head + 1] = jnp.where(low, aq, qn).T.astype(qt_ref.dtype)
            ka_ref[0, head] = jnp.where(low, kn, ak).astype(ka_ref.dtype)
            ka_ref[0, head + 1] = jnp.where(low, ak, kn).astype(ka_ref.dtype)
            v_t = acc_v[:, cols].T.astype(vt_ref.dtype)
            vt_ref[0, head] = v_t[:dh]
            vt_ref[0, head + 1] = v_t[dh:]


def _fox_in(x, g, mod, tail, w_in, b_f, g_q, g_k, shift):
    bsz, seq, d = x.shape
    nh, dh = FOX_HEADS, FOX_DH
    ad = nh * dh
    tm = min(ROW_TILE, seq)
    pad = LANES - FOX_TERMS * nh
    w = jnp.concatenate([w_in[:, :4 * ad]] + [w_in[:, 4 * ad:]] * FOX_TERMS
                        + [jnp.zeros((d, pad), w_in.dtype)], axis=1).astype(BF16)
    bf = jnp.pad(jnp.tile(b_f, FOX_TERMS), (0, pad)).reshape(1, LANES)
    gq2 = jnp.tile(g_q * (dh ** -0.5 * LOG2E), 2).reshape(1, 2 * dh)
    gk2 = jnp.tile(g_k, 2).reshape(1, 2 * dh)
    tri = jnp.asarray(np.tril(np.ones((tm, tm), np.float32)), BF16)
    pq, pk, cq, ck, shift_slot = _fox_tables(nh, dh)
    ck = jnp.asarray(ck) - shift * jnp.asarray(shift_slot)
    kern = functools.partial(_fox_in_kernel, ad=ad, dh=dh, n_heads=nh)
    return _in_proj_call(
        kern, x, g, mod, tail,
        [w, gq2, gk2, bf, tri, jnp.asarray(pq, BF16), jnp.asarray(pk, BF16), jnp.asarray(cq), ck],
        out_specs=[
            pl.BlockSpec((1, nh, LANES, tm), lambda b, s: (b, 0, 0, s)),
            pl.BlockSpec((1, nh, tm, LANES), lambda b, s: (b, 0, s, 0)),
            pl.BlockSpec((1, nh, dh, tm), lambda b, s: (b, 0, 0, s)),
            _row_spec(tm, ad)],
        out_shape=[jax.ShapeDtypeStruct((bsz, nh, LANES, seq), BF16),
                   jax.ShapeDtypeStruct((bsz, nh, seq, LANES), BF16),
                   jax.ShapeDtypeStruct((bsz, nh, dh, seq), BF16),
                   jax.ShapeDtypeStruct((bsz, seq, ad), BF16)],
        name="fox_in_proj", tm=tm, scratch=[pltpu.VMEM((1, LANES), F32)])


def _fox_attn_kernel(qt_ref, ka_ref, vt_ref, o_ref, s0_ref, s1_ref, m_ref, acc_ref,
                     *, tq, tk, dh):
    qi = pl.program_id(2)
    hp = qt_ref.shape[1]
    heads = range(hp)
    sum_rows = jnp.ones((FOX_SUM_ROWS, tk), BF16)

    def scores(hh, kv):
        k0 = pl.multiple_of(kv * tk, tk)
        return _dot(ka_ref[0, hh, pl.ds(k0, tk), :], qt_ref[0, hh])

    def consume(hh, s_ref, kv, masked):
        k0 = pl.multiple_of(kv * tk, tk)
        if masked:
            visible = (lax.broadcasted_iota(jnp.int32, (tk, tq), 0)
                       <= lax.broadcasted_iota(jnp.int32, (tk, tq), 1))
            read = lambda: jnp.where(visible, s_ref[hh], -jnp.inf)
        else:
            read = lambda: s_ref[hh]
        m = m_ref[hh]
        m_new = jnp.maximum(m, jnp.max(read(), axis=0, keepdims=True))
        p = jnp.exp2(read() - m_new).astype(BF16)
        v_sum = jnp.concatenate([vt_ref[0, hh, :, pl.ds(k0, tk)], sum_rows], axis=0)
        acc_ref[hh] = acc_ref[hh] * jnp.exp2(m - m_new) + _dot(v_sum, p)
        m_ref[hh] = m_new

    m_ref[...] = jnp.full(m_ref.shape, -jnp.inf, F32)
    acc_ref[...] = jnp.zeros(acc_ref.shape, F32)
    for hh in heads:
        s0_ref[hh] = scores(hh, 0)

    def two_blocks(i, carry):
        j = 2 * i
        for hh in heads:
            s1_ref[hh] = scores(hh, j + 1)
            consume(hh, s0_ref, j, masked=False)
        for hh in heads:
            s0_ref[hh] = scores(hh, j + 2)
            consume(hh, s1_ref, j + 1, masked=False)
        return carry

    lax.fori_loop(0, qi // 2, two_blocks, 0)

    @pl.when(qi % 2 == 1)
    def _():
        for hh in heads:
            s1_ref[hh] = scores(hh, qi)
            consume(hh, s0_ref, qi - 1, masked=False)
        for hh in heads:
            consume(hh, s1_ref, qi, masked=True)

    @pl.when(qi % 2 == 0)
    def _():
        for hh in heads:
            consume(hh, s0_ref, qi, masked=True)

    out_t = jnp.concatenate([acc_ref[hh, :dh] / acc_ref[hh, dh:dh + 1] for hh in heads], axis=0)
    o_ref[0] = out_t.T.astype(o_ref.dtype)


def _fox_attn_shifted_kernel(qt_ref, ka_ref, vt_ref, o_ref, p0_ref, p1_ref, acc_ref,
                             *, tq, tk, dh):
    qi = pl.program_id(2)
    hp = qt_ref.shape[1]
    bufs = (p0_ref, p1_ref)
    sum_rows = jnp.ones((FOX_SUM_ROWS, tk), BF16)

    def probs(hh, kv, masked):
        k0 = pl.multiple_of(kv * tk, tk)
        p = jnp.exp2(_dot(ka_ref[0, hh, pl.ds(k0, tk), :], qt_ref[0, hh]))
        if masked:
            visible = (lax.broadcasted_iota(jnp.int32, (tk, tq), 0)
                       <= lax.broadcasted_iota(jnp.int32, (tk, tq), 1))
            p = jnp.where(visible, p, 0.0)
        return p.astype(BF16)

    def fold(hh, kv, p_ref):
        k0 = pl.multiple_of(kv * tk, tk)
        v_sum = jnp.concatenate([vt_ref[0, hh, :, pl.ds(k0, tk)], sum_rows], axis=0)
        acc_ref[hh] += _dot(v_sum, p_ref[...])

    def section(blocks, masked, following):
        units = [(kv, hh) for kv in blocks for hh in range(hp)]
        for n, (kv, hh) in enumerate(units):
            nxt = units[n + 1] if n + 1 < len(units) else (following, 0)
            if nxt[0] is not None:
                bufs[(n + 1) % 2][...] = probs(nxt[1], nxt[0], masked and n + 1 < len(units))
            fold(hh, kv, bufs[n % 2])

    acc_ref[...] = jnp.zeros(acc_ref.shape, F32)
    p0_ref[...] = probs(0, qi, masked=True)
    section([qi], masked=True, following=0)

    def two_blocks(i, carry):
        section([2 * i, 2 * i + 1], masked=False, following=2 * i + 2)
        return carry

    lax.fori_loop(0, qi // 2, two_blocks, 0)

    @pl.when(qi % 2 == 1)
    def _():
        section([qi - 1], masked=False, following=None)

    out_t = jnp.concatenate([acc_ref[hh, :dh] / acc_ref[hh, dh:dh + 1] for hh in range(hp)],
                            axis=0)
    o_ref[0] = out_t.T.astype(o_ref.dtype)


def _fox_attn(qt, ka, vt, shifted):
    bsz, nh, _, seq = qt.shape
    dh = FOX_DH
    tq, tk = min(FOX_TQ, seq), min(FOX_TK, seq)
    hp = FOX_HEADS_PER_STEP["shifted" if shifted else "online"]
    assert tq == tk and (hp * dh) % LANES == 0
    acc = pltpu.VMEM((hp, dh + FOX_SUM_ROWS, tq), F32)
    if shifted:
        kern = _fox_attn_shifted_kernel
        scratch = [pltpu.VMEM((tk, tq), BF16), pltpu.VMEM((tk, tq), BF16), acc]
    else:
        kern = _fox_attn_kernel
        scratch = [pltpu.VMEM((hp, tk, tq), F32), pltpu.VMEM((hp, tk, tq), F32),
                   pltpu.VMEM((hp, 1, tq), F32), acc]
    return pl.pallas_call(
        functools.partial(kern, tq=tq, tk=tk, dh=dh),
        grid=(bsz, nh // hp, seq // tq),
        in_specs=[
            pl.BlockSpec((1, hp, LANES, tq), lambda b, h, s: (b, h, 0, s)),
            pl.BlockSpec((1, hp, seq, LANES), lambda b, h, s: (b, h, 0, 0)),
            pl.BlockSpec((1, hp, dh, seq), lambda b, h, s: (b, h, 0, 0)),
        ],
        out_specs=pl.BlockSpec((1, tq, hp * dh), lambda b, h, s: (b, s, h)),
        out_shape=jax.ShapeDtypeStruct((bsz, seq, nh * dh), BF16),
        scratch_shapes=scratch,
        name="fox_attn_shifted" if shifted else "fox_attn",
    )(qt, ka, vt)


def _out_kernel(o_ref, z_ref, w_ref, g_ref, gate_ref, x_ref, y_ref):
    y_ref[0] = _layer_tail(x_ref[0], slice(None), o_ref, z_ref, w_ref, g_ref, gate_ref)


def _out_proj(x, tail):
    o, z, w_out, g_post, mod = tail
    bsz, seq, d = x.shape
    width = o.shape[-1]
    tm = min(ROW_TILE, seq)
    return pl.pallas_call(
        _out_kernel,
        grid=(bsz, seq // tm),
        in_specs=[_row_spec(tm, width), _row_spec(tm, width), _full((width, d)), _full((1, d)),
                  _mod_spec(d, 2), _row_spec(tm, d)],
        out_specs=_row_spec(tm, d),
        out_shape=jax.ShapeDtypeStruct((bsz, seq, d), F32),
        name="out_proj",
    )(o, z, w_out.astype(BF16), g_post.reshape(1, d), mod, x)


def _gla_layer(x, g_pre, mod, tail, w_in, w_a2, b_a, g_head):
    q, k, v, z, gk, *x_new = _gla_in(x, g_pre, mod, tail, w_in, w_a2, b_a)
    return (x_new or [x])[0], _gla_scan(q, k, v, gk, g_head), z


def _sgu_layer(x, g_pre, mod, tail, w_in, ln_g, ln_b, w_s, b_s):
    u, v, z, *x_new = _sgu_in(x, g_pre, mod, tail, w_in, ln_g, ln_b)
    return (x_new or [x])[0], _sgu_mix(u, v, w_s, b_s), z


def _fox_layer(x, g_pre, mod, tail, w_in, b_f, g_q, g_k):
    bound = FOX_DH * (FOX_DH ** -0.5 * LOG2E) * jnp.max(jnp.abs(g_q)) * jnp.max(jnp.abs(g_k))
    shift = 1.02 * bound
    qt, ka, vt, z, *x_new = _fox_in(x, g_pre, mod, tail, w_in, b_f, g_q, g_k, shift)
    o = lax.cond(shift <= FOX_MAX_SHIFT,
                 functools.partial(_fox_attn, shifted=True),
                 functools.partial(_fox_attn, shifted=False), qt, ka, vt)
    return (x_new or [x])[0], o, z


def kernel(x, c, norm_pre_g, norm_post_g, w_mod, b_mod, gla_w_in, gla_w_a2, gla_b_a, gla_g_head, gla_w_out, sgu_w_in, sgu_ln_g, sgu_ln_b, sgu_w_s, sgu_b_s, sgu_w_out, fox_w_in, fox_b_f, fox_g_q, fox_g_k, fox_w_out):
    depth = w_mod.shape[0]
    bsz = x.shape[0]
    mod_all = _modulation(c, w_mod, b_mod)
    tail = None
    for i in range(depth):
        mod = mod_all[i].reshape(bsz, 1, -1)
        kind, j = i % N_MIXERS, i // N_MIXERS
        if kind == 0:
            x, o, z = _gla_layer(x, norm_pre_g[i], mod, tail, gla_w_in[j], gla_w_a2[j],
                                 gla_b_a[j], gla_g_head[j])
            w_out = gla_w_out[j]
        elif kind == 1:
            x, o, z = _sgu_layer(x, norm_pre_g[i], mod, tail, sgu_w_in[j], sgu_ln_g[j],
                                 sgu_ln_b[j], sgu_w_s[j], sgu_b_s[j])
            w_out = sgu_w_out[j]
        else:
            x, o, z = _fox_layer(x, norm_pre_g[i], mod, tail, fox_w_in[j], fox_b_f[j],
                                 fox_g_q[j], fox_g_k[j])
            w_out = fox_w_out[j]
        tail = (o, z, w_out, norm_post_g[i], mod)
    return _out_proj(x, tail)
```

```python
import functools
import math

import numpy as np
import jax
import jax.numpy as jnp
from jax import lax
from jax.experimental import pallas as pl
from jax.experimental.pallas import tpu as pltpu

EPS = 1e-6
N_MIXERS = 3

LANES = 128
ROW_TILE = 512
ROW_TILES_PER_STEP = 2
COL_CHUNK = 512

GLA_HEADS = 4
GLA_RANK = 16
GLA_NORMALIZER = 16.0
GLA_CHUNK = 64
GLA_FAST_CHUNK = 256
GLA_DECAY_GUARD = 70.0
GLA_BLOCK = 1024

SGU_GROUPS = 4
SGU_CHUNK = 128

FOX_HEADS = 16
FOX_DH = 64
FOX_TQ = 512
FOX_TK = 512
FOX_HEADS_PER_STEP = {"shifted": 8, "online": 4}
FOX_SUM_ROWS = 16
FOX_TERMS = 3
FOX_MAX_SHIFT = 50.0
LOG2E = math.log2(math.e)

BF16 = jnp.bfloat16
F32 = jnp.float32


def _dot(a, b):
    return jnp.dot(a, b, preferred_element_type=F32)


def _dot_nt(a, b):
    return lax.dot_general(a, b, (((1,), (1,)), ((), ())), preferred_element_type=F32)


def _dot_tn(a, b):
    return lax.dot_general(a, b, (((0,), (0,)), ((), ())), preferred_element_type=F32)


def _split2(x):
    hi = x.astype(BF16)
    lo = (x - hi.astype(F32)).astype(BF16)
    return hi, lo


def _split3(x):
    hi = x.astype(BF16)
    r = x - hi.astype(F32)
    mid = r.astype(BF16)
    lo = (r - mid.astype(F32)).astype(BF16)
    return hi, mid, lo


def _log_sigmoid(x):
    return jnp.minimum(x, 0.0) - jnp.log1p(jnp.exp(-jnp.abs(x)))


def _rms(x):
    return x * lax.rsqrt(jnp.mean(x * x, axis=-1, keepdims=True) + EPS)


def _mod_kernel(c_ref, w_ref, b_ref, o_ref):
    cond = jax.nn.silu(c_ref[...])
    o_ref[0] = _dot(cond, w_ref[0]) + b_ref[0]


def _modulation(c, w_mod, b_mod):
    depth, d, d3 = w_mod.shape
    bsz = c.shape[0]
    nblk = d3 // d
    return pl.pallas_call(
        _mod_kernel,
        grid=(depth, nblk),
        in_specs=[
            pl.BlockSpec((bsz, d), lambda i, j: (0, 0)),
            pl.BlockSpec((1, d, d), lambda i, j: (i, 0, j)),
            pl.BlockSpec((1, 1, d), lambda i, j: (i, 0, j)),
        ],
        out_specs=pl.BlockSpec((1, bsz, d), lambda i, j: (i, 0, j)),
        out_shape=jax.ShapeDtypeStruct((depth, bsz, d3), F32),
        name="adaln_mod",
    )(c, w_mod, b_mod.reshape(depth, 1, d3))


N_HEAD_REFS = 4
N_TAIL_REFS = 5


def _layer_tail(x, rows, o_ref, z_ref, w_ref, g_ref, gate_ref):
    gated = o_ref[0, rows, :].astype(F32) * jax.nn.silu(z_ref[0, rows, :].astype(F32))
    y = _dot(gated.astype(BF16), w_ref[...])
    return x + gate_ref[0] * (_rms(y) * g_ref[...])


def _kernel_refs(refs, has_tail, n_scratch=0):
    rest = list(refs[N_HEAD_REFS:])
    if has_tail:
        del rest[len(rest) - n_scratch - 1]
        rest = rest[N_TAIL_REFS:]
    return rest


def _row_tiles(refs):
    rows = refs[0].shape[1]
    return [slice(r, r + ROW_TILE) for r in range(0, rows, min(ROW_TILE, rows))]


def _layer_input(refs, has_tail, rows, n_scratch=0):
    x_ref, g_ref, shift_ref, scale_ref = refs[:N_HEAD_REFS]
    x = x_ref[0, rows, :]
    if has_tail:
        x = _layer_tail(x, rows, *refs[N_HEAD_REFS:N_HEAD_REFS + N_TAIL_REFS])
        refs[len(refs) - n_scratch - 1][0, rows, :] = x
    h = _rms(x) * g_ref[...]
    h = h * (1.0 + scale_ref[0]) + shift_ref[0]
    return h.astype(BF16)


def _full(shape):
    nd = len(shape)
    return pl.BlockSpec(shape, lambda b, s: (0,) * nd, pipeline_mode=pl.Buffered(1))


def _row_spec(tm, width, col=0):
    return pl.BlockSpec((1, tm, width), lambda b, s: (b, s, col))


def _mod_spec(d, part):
    return pl.BlockSpec((1, 1, d), lambda b, s: (b, 0, part))


def _in_proj_call(kern, x, g, mod, tail, consts, out_specs, out_shape, name, tm, scratch=()):
    bsz, seq, d = x.shape
    operands = [x, g.reshape(1, d), mod, mod]
    in_specs = [_row_spec(tm, d), _full((1, d)), _mod_spec(d, 0), _mod_spec(d, 1)]
    if tail is not None:
        o, z, w_out, g_post, mod_prev = tail
        width = o.shape[-1]
        operands += [o, z, w_out.astype(BF16), g_post.reshape(1, d), mod_prev]
        in_specs += [_row_spec(tm, width), _row_spec(tm, width), _full((width, d)),
                     _full((1, d)), _mod_spec(d, 2)]
        out_specs = list(out_specs) + [_row_spec(tm, d)]
        out_shape = list(out_shape) + [jax.ShapeDtypeStruct((bsz, seq, d), F32)]
    operands += list(consts)
    in_specs += [_full(c.shape) for c in consts]
    return pl.pallas_call(
        functools.partial(kern, has_tail=tail is not None),
        grid=(bsz, seq // tm),
        in_specs=in_specs, out_specs=out_specs, out_shape=out_shape,
        scratch_shapes=list(scratch),
        compiler_params=pltpu.CompilerParams(dimension_semantics=("arbitrary", "arbitrary")),
        name=name,
    )(*operands)


def _gla_in_kernel(*refs, has_tail, kd, vd, q_scale):
    w_ref, wa2_ref, ba_ref, q_ref, k_ref, v_ref, z_ref, gk_ref = _kernel_refs(refs, has_tail)
    n_main = 2 * kd + 2 * vd
    tiles = _row_tiles(refs)
    inputs = [_layer_input(refs, has_tail, rows) for rows in tiles]
    for rows, h in zip(tiles, inputs):
        a_low = _dot(h, w_ref[:, n_main:n_main + LANES])
        logit = _dot(a_low.astype(BF16), wa2_ref[...]) + ba_ref[...]
        gk_ref[0, rows, :] = _log_sigmoid(logit) * (1.0 / GLA_NORMALIZER)
        col = 0
        for out_ref, width, mul in ((q_ref, kd, q_scale), (k_ref, kd, None),
                                    (v_ref, vd, None), (z_ref, vd, None)):
            for c0 in range(0, width, COL_CHUNK):
                acc = _dot(h, w_ref[:, col + c0:col + c0 + COL_CHUNK])
                if mul is not None:
                    acc = acc * mul
                out_ref[0, rows, c0:c0 + COL_CHUNK] = acc.astype(out_ref.dtype)
            col += width


def _gla_in(x, g, mod, tail, w_in, w_a2, b_a):
    bsz, seq, d = x.shape
    kd = w_a2.shape[1]
    vd = (w_in.shape[1] - 2 * kd - GLA_RANK) // 2
    tm = min(ROW_TILE * ROW_TILES_PER_STEP, seq)
    w = jnp.pad(w_in, ((0, 0), (0, LANES - GLA_RANK))).astype(BF16)
    wa2 = jnp.pad(w_a2, ((0, LANES - GLA_RANK), (0, 0))).astype(BF16)
    dk = kd // GLA_HEADS
    kern = functools.partial(_gla_in_kernel, kd=kd, vd=vd, q_scale=dk ** -0.5)
    act = lambda width, dtype: jax.ShapeDtypeStruct((bsz, seq, width), dtype)
    return _in_proj_call(
        kern, x, g, mod, tail, [w, wa2, b_a.reshape(1, kd)],
        out_specs=[_row_spec(tm, kd), _row_spec(tm, kd), _row_spec(tm, vd),
                   _row_spec(tm, vd), _row_spec(tm, kd)],
        out_shape=[act(kd, BF16), act(kd, BF16), act(vd, BF16), act(vd, BF16), act(kd, F32)],
        name="gla_in_proj", tm=tm)


def _gla_tables(c):
    tri = np.tril(np.ones((c, c), np.float32))
    rows = [tri, 1.0 - tri]
    n_levels = int(math.log2(c))
    level = np.full((c, c), -1, np.int32)
    idx = np.arange(c)
    for l in range(n_levels):
        blk = c >> l
        half = blk // 2
        mid = (idx // blk) * blk + half - 1
        rows.append(tri - tri[mid])
        same = (idx[:, None] // blk) == (idx[None, :] // blk)
        upper = (idx[:, None] % blk) >= half
        lower = (idx[None, :] % blk) < half
        level[same & upper & lower] = l
    level[idx, idx] = n_levels
    return np.concatenate(rows, axis=0), level, n_levels


def _gla_scan_kernel(q_ref, k_ref, v_ref, gk_ref, wf_ref, ws_ref, lvl_ref, gh_ref, o_ref, st_ref,
                     *, fast_chunk, safe_chunk, n_levels, dk, dv):
    tb, kd = gk_ref.shape[1], gk_ref.shape[2]

    @pl.when(pl.program_id(1) == 0)
    def _():
        st_ref[...] = jnp.zeros_like(st_ref)

    def head_chunk(hd, rows, chunk, safe):
        kc = slice(hd * dk, (hd + 1) * dk)
        vc = slice(hd * dv, (hd + 1) * dv)
        w = ws_ref[...] if safe else wf_ref[...]
        r = _dot(w, jnp.concatenate(_split2(gk_ref[0, rows, kc]), axis=0))
        b = r[0:chunk]
        rev = r[chunk:2 * chunk]
        qf = q_ref[0, rows, kc].astype(F32)
        kf = k_ref[0, rows, kc].astype(F32)
        v = v_ref[0, rows, vc]
        st = st_ref[hd]
        q_dec = (qf * jnp.exp(b)).astype(BF16)

        if safe:
            lvl = lvl_ref[...]
            att = jnp.zeros((chunk, chunk), F32)
            for l in range(n_levels + 1):
                if l < n_levels:
                    e = jnp.exp(-jnp.abs(r[(2 + l) * chunk:(3 + l) * chunk]))
                    ql, kl = (qf * e).astype(BF16), (kf * e).astype(BF16)
                else:
                    ql, kl = qf.astype(BF16), kf.astype(BF16)
                att = jnp.where(lvl == l, _dot_nt(ql, kl), att)
        else:
            causal = (lax.broadcasted_iota(jnp.int32, (chunk, chunk), 0)
                      >= lax.broadcasted_iota(jnp.int32, (chunk, chunk), 1))
            att = jnp.where(causal, _dot_nt(q_dec, (kf * jnp.exp(-b)).astype(BF16)), 0.0)

        o = _dot_nt(q_dec, st.astype(BF16)) + _dot(att.astype(BF16), v)
        o_ref[0, rows, vc] = (_rms(o) * gh_ref[:, vc]).astype(o_ref.dtype)
        k_rev = (kf * jnp.exp(rev)).astype(BF16)
        st_ref[hd] = st * jnp.exp(b[chunk - 1:chunk, :]) + _dot_tn(v, k_rev)

    def run(chunk, safe):
        def body(ci, carry):
            rows = pl.ds(pl.multiple_of(ci * chunk, chunk), chunk)
            for hd in range(GLA_HEADS):
                head_chunk(hd, rows, chunk, safe)
            return carry
        lax.fori_loop(0, tb // chunk, body, 0, unroll=2)

    chunk_sums = jnp.sum(gk_ref[0].reshape(tb // fast_chunk, fast_chunk, kd), axis=1)
    fast_ok = jnp.min(chunk_sums) >= -GLA_DECAY_GUARD

    @pl.when(fast_ok)
    def _():
        run(fast_chunk, safe=False)

    @pl.when(jnp.logical_not(fast_ok))
    def _():
        run(safe_chunk, safe=True)


def _gla_scan(q, k, v, gk, g_head):
    bsz, seq, kd = q.shape
    vd = v.shape[-1]
    dk, dv = kd // GLA_HEADS, vd // GLA_HEADS
    tb = min(GLA_BLOCK, seq)
    ws_np, lvl_np, n_levels = _gla_tables(GLA_CHUNK)
    wf_np = _gla_tables(GLA_FAST_CHUNK)[0][:2 * GLA_FAST_CHUNK]
    stack2 = lambda t: jnp.asarray(np.concatenate([t, t], axis=1), BF16)
    wf, ws = stack2(wf_np), stack2(ws_np)
    kern = functools.partial(_gla_scan_kernel, fast_chunk=GLA_FAST_CHUNK, safe_chunk=GLA_CHUNK,
                             n_levels=n_levels, dk=dk, dv=dv)
    return pl.pallas_call(
        kern,
        grid=(bsz, seq // tb),
        in_specs=[_row_spec(tb, kd), _row_spec(tb, kd), _row_spec(tb, vd), _row_spec(tb, kd),
                  _full(wf.shape), _full(ws.shape), _full(lvl_np.shape), _full((1, vd))],
        out_specs=_row_spec(tb, vd),
        out_shape=jax.ShapeDtypeStruct((bsz, seq, vd), BF16),
        scratch_shapes=[pltpu.VMEM((GLA_HEADS, dv, dk), F32)],
        compiler_params=pltpu.CompilerParams(dimension_semantics=("arbitrary", "arbitrary")),
        name="gla_scan",
    )(q, k, v, gk, wf, ws, jnp.asarray(lvl_np), g_head.reshape(1, vd))


def _sgu_in_kernel(*refs, has_tail, width):
    w_ref, lng_ref, lnb_ref, u_ref, v_ref, z_ref, vtmp_ref = _kernel_refs(refs, has_tail, 1)
    tiles = _row_tiles(refs)
    inputs = [_layer_input(refs, has_tail, rows, n_scratch=1) for rows in tiles]
    for rows, h in zip(tiles, inputs):
        for c0 in range(0, width, COL_CHUNK):
            vtmp_ref[rows, c0:c0 + COL_CHUNK] = jax.nn.gelu(
                _dot(h, w_ref[:, width + c0:width + c0 + COL_CHUNK]))
        v = vtmp_ref[rows, :]
        mu = jnp.mean(v, axis=-1, keepdims=True)
        vc = v - mu
        var = jnp.mean(vc * vc, axis=-1, keepdims=True)
        v_ref[0, rows, :] = (vc * lax.rsqrt(var + EPS) * lng_ref[...] + lnb_ref[...]
                             ).astype(v_ref.dtype)
        for c0 in range(0, width, COL_CHUNK):
            cols = slice(c0, c0 + COL_CHUNK)
            u_ref[0, rows, cols] = jax.nn.gelu(_dot(h, w_ref[:, cols])).astype(u_ref.dtype)
            z_ref[0, rows, cols] = _dot(h, w_ref[:, 2 * width + c0:2 * width + c0 + COL_CHUNK]
                                        ).astype(z_ref.dtype)


def _sgu_in(x, g, mod, tail, w_in, ln_g, ln_b):
    bsz, seq, d = x.shape
    width = w_in.shape[1] // 3
    tm = min(ROW_TILE * ROW_TILES_PER_STEP, seq)
    kern = functools.partial(_sgu_in_kernel, width=width)
    act = jax.ShapeDtypeStruct((bsz, seq, width), BF16)
    return _in_proj_call(
        kern, x, g, mod, tail,
        [w_in.astype(BF16), ln_g.reshape(1, width), ln_b.reshape(1, width)],
        out_specs=[_row_spec(tm, width)] * 3, out_shape=[act, act, act],
        name="sgu_in_proj", tm=tm, scratch=[pltpu.VMEM((tm, width), F32)])


def _sgu_mix_kernel(u_ref, v_ref, ws_ref, bs_ref, o_ref, *, chunk, gdim, n_chunks):
    for g in range(SGU_GROUPS):
        cols = slice(g * gdim, (g + 1) * gdim)
        w = ws_ref[g]
        bias = bs_ref[g]
        bias = jnp.concatenate([bias] * (gdim // LANES), axis=1)
        for c in range(n_chunks):
            rows = slice(c * chunk, (c + 1) * chunk)
            mixed = _dot(w, v_ref[0, rows, cols]) + bias
            o_ref[0, rows, cols] = (u_ref[0, rows, cols].astype(F32) * mixed).astype(o_ref.dtype)


def _sgu_mix(u, v, w_s, b_s):
    bsz, seq, width = u.shape
    gdim = width // SGU_GROUPS
    tm = min(ROW_TILE * ROW_TILES_PER_STEP, seq)
    chunk = SGU_CHUNK
    causal = jnp.tril(jnp.ones((chunk, chunk), bool))
    w_causal = jnp.where(causal[None], w_s, 0.0).astype(BF16)
    bias = jnp.broadcast_to(b_s[:, :, None], (SGU_GROUPS, chunk, LANES))
    kern = functools.partial(_sgu_mix_kernel, chunk=chunk, gdim=gdim, n_chunks=tm // chunk)
    return pl.pallas_call(
        kern,
        grid=(bsz, seq // tm),
        in_specs=[_row_spec(tm, width), _row_spec(tm, width),
                  _full((SGU_GROUPS, chunk, chunk)), _full((SGU_GROUPS, chunk, LANES))],
        out_specs=_row_spec(tm, width),
        out_shape=jax.ShapeDtypeStruct((bsz, seq, width), BF16),
        name="sgu_mix",
    )(u, v, w_causal, bias)


def _fox_tables(n_heads, dh):
    width = n_heads * dh
    n_terms = FOX_TERMS
    assert n_terms * n_heads <= LANES and 2 * n_terms + 1 <= dh
    pq = np.zeros((LANES, width), np.float32)
    pk = np.zeros((LANES, width), np.float32)
    cq = np.zeros((1, width), np.float32)
    ck = np.zeros((1, width), np.float32)
    shift_slot = np.zeros((1, width), np.float32)
    for h in range(n_heads):
        base = (h // 2) * 2 * dh + (dh if h % 2 == 0 else 0)
        for p in range(n_terms):
            pq[p * n_heads + h, base + p] = 1.0
            cq[0, base + n_terms + p] = 1.0
            ck[0, base + p] = 1.0
            pk[p * n_heads + h, base + n_terms + p] = -1.0
        cq[0, base + 2 * n_terms] = 1.0
        shift_slot[0, base + 2 * n_terms] = 1.0
    return pq, pk, cq, ck, shift_slot


def _fox_in_kernel(*refs, has_tail, ad, dh, n_heads):
    (w_ref, gq_ref, gk_ref, bf_ref, tri_ref, pq_ref, pk_ref, cq_ref, ck_ref,
     qt_ref, ka_ref, vt_ref, z_ref, carry_ref) = _kernel_refs(refs, has_tail, 1)
    h = _layer_input(refs, has_tail, slice(None), n_scratch=1)

    @pl.when(pl.program_id(1) == 0)
    def _():
        carry_ref[...] = jnp.zeros_like(carry_ref)

    tm = h.shape[0]
    pair = 2 * dh
    lane = lax.broadcasted_iota(jnp.int32, (tm, pair), 1)
    low = lane < dh

    def half_rms(x):
        sq = x * x
        s_lo = jnp.sum(jnp.where(low, sq, 0.0), axis=-1, keepdims=True)
        s_hi = jnp.sum(jnp.where(low, 0.0, sq), axis=-1, keepdims=True)
        return x * lax.rsqrt(jnp.where(low, s_lo, s_hi) * (1.0 / dh) + EPS)

    f_logit = _dot(h, w_ref[:, 4 * ad:4 * ad + LANES])
    tri = tri_ref[...]
    lf_hi, lf_mid, lf_lo = _split3(_log_sigmoid(f_logit + bf_ref[...]))
    f_cum = _dot(tri, lf_hi) + _dot(tri, lf_mid) + _dot(tri, lf_lo) + carry_ref[...]
    carry_ref[...] = f_cum[tm - 1:tm, :]
    term_list = _split3(f_cum * LOG2E)
    terms = term_list[-1]
    for p in range(FOX_TERMS - 2, -1, -1):
        terms = jnp.where(lane < (p + 1) * n_heads, term_list[p], terms)
    aug_q = _dot(terms, pq_ref[...]) + cq_ref[...]
    aug_k = _dot(terms, pk_ref[...]) + ck_ref[...]

    for c0 in range(0, ad, COL_CHUNK):
        acc_q = _dot(h, w_ref[:, c0:c0 + COL_CHUNK])
        acc_k = _dot(h, w_ref[:, ad + c0:ad + c0 + COL_CHUNK])
        acc_v = _dot(h, w_ref[:, 2 * ad + c0:2 * ad + c0 + COL_CHUNK])
        z_ref[0, :, c0:c0 + COL_CHUNK] = _dot(
            h, w_ref[:, 3 * ad + c0:3 * ad + c0 + COL_CHUNK]).astype(z_ref.dtype)
        for g0 in range(0, COL_CHUNK, pair):
            cols = slice(g0, g0 + pair)
            gcols = slice(c0 + g0, c0 + g0 + pair)
            head = (c0 + g0) // dh
            qn = half_rms(acc_q[:, cols]) * gq_ref[...]
            kn = half_rms(acc_k[:, cols]) * gk_ref[...]
            aq, ak = aug_q[:, gcols], aug_k[:, gcols]
            qt_ref[0, head] = jnp.where(low, qn, aq).T.astype(qt_ref.dtype)
            qt_ref[0, head + 1] = jnp.where(low, aq, qn).T.astype(qt_ref.dtype)
            ka_ref[0, head] = jnp.where(low, kn, ak).astype(ka_ref.dtype)
            ka_ref[0, head + 1] = jnp.where(low, ak, kn).astype(ka_ref.dtype)
            v_t = acc_v[:, cols].T.astype(vt_ref.dtype)
            vt_ref[0, head] = v_t[:dh]
            vt_ref[0, head + 1] = v_t[dh:]


def _fox_in(x, g, mod, tail, w_in, b_f, g_q, g_k, shift):
    bsz, seq, d = x.shape
    nh, dh = FOX_HEADS, FOX_DH
    ad = nh * dh
    tm = min(ROW_TILE, seq)
    pad = LANES - FOX_TERMS * nh
    w = jnp.concatenate([w_in[:, :4 * ad]] + [w_in[:, 4 * ad:]] * FOX_TERMS
                        + [jnp.zeros((d, pad), w_in.dtype)], axis=1).astype(BF16)
    bf = jnp.pad(jnp.tile(b_f, FOX_TERMS), (0, pad)).reshape(1, LANES)
    gq2 = jnp.tile(g_q * (dh ** -0.5 * LOG2E), 2).reshape(1, 2 * dh)
    gk2 = jnp.tile(g_k, 2).reshape(1, 2 * dh)
    tri = jnp.asarray(np.tril(np.ones((tm, tm), np.float32)), BF16)
    pq, pk, cq, ck, shift_slot = _fox_tables(nh, dh)
    ck = jnp.asarray(ck) - shift * jnp.asarray(shift_slot)
    kern = functools.partial(_fox_in_kernel, ad=ad, dh=dh, n_heads=nh)
    return _in_proj_call(
        kern, x, g, mod, tail,
        [w, gq2, gk2, bf, tri, jnp.asarray(pq, BF16), jnp.asarray(pk, BF16), jnp.asarray(cq), ck],
        out_specs=[
            pl.BlockSpec((1, nh, LANES, tm), lambda b, s: (b, 0, 0, s)),
            pl.BlockSpec((1, nh, tm, LANES), lambda b, s: (b, 0, s, 0)),
            pl.BlockSpec((1, nh, dh, tm), lambda b, s: (b, 0, 0, s)),
            _row_spec(tm, ad)],
        out_shape=[jax.ShapeDtypeStruct((bsz, nh, LANES, seq), BF16),
                   jax.ShapeDtypeStruct((bsz, nh, seq, LANES), BF16),
                   jax.ShapeDtypeStruct((bsz, nh, dh, seq), BF16),
                   jax.ShapeDtypeStruct((bsz, seq, ad), BF16)],
        name="fox_in_proj", tm=tm, scratch=[pltpu.VMEM((1, LANES), F32)])


def _fox_attn_kernel(qt_ref, ka_ref, vt_ref, o_ref, s0_ref, s1_ref, m_ref, acc_ref,
                     *, tq, tk, dh):
    qi = pl.program_id(2)
    hp = qt_ref.shape[1]
    heads = range(hp)
    sum_rows = jnp.ones((FOX_SUM_ROWS, tk), BF16)

    def scores(hh, kv):
        k0 = pl.multiple_of(kv * tk, tk)
        return _dot(ka_ref[0, hh, pl.ds(k0, tk), :], qt_ref[0, hh])

    def consume(hh, s_ref, kv, masked):
        k0 = pl.multiple_of(kv * tk, tk)
        if masked:
            visible = (lax.broadcasted_iota(jnp.int32, (tk, tq), 0)
                       <= lax.broadcasted_iota(jnp.int32, (tk, tq), 1))
            read = lambda: jnp.where(visible, s_ref[hh], -jnp.inf)
        else:
            read = lambda: s_ref[hh]
        m = m_ref[hh]
        m_new = jnp.maximum(m, jnp.max(read(), axis=0, keepdims=True))
        p = jnp.exp2(read() - m_new).astype(BF16)
        v_sum = jnp.concatenate([vt_ref[0, hh, :, pl.ds(k0, tk)], sum_rows], axis=0)
        acc_ref[hh] = acc_ref[hh] * jnp.exp2(m - m_new) + _dot(v_sum, p)
        m_ref[hh] = m_new

    m_ref[...] = jnp.full(m_ref.shape, -jnp.inf, F32)
    acc_ref[...] = jnp.zeros(acc_ref.shape, F32)
    for hh in heads:
        s0_ref[hh] = scores(hh, 0)

    def two_blocks(i, carry):
        j = 2 * i
        for hh in heads:
            s1_ref[hh] = scores(hh, j + 1)
            consume(hh, s0_ref, j, masked=False)
        for hh in heads:
            s0_ref[hh] = scores(hh, j + 2)
            consume(hh, s1_ref, j + 1, masked=False)
        return carry

    lax.fori_loop(0, qi // 2, two_blocks, 0)

    @pl.when(qi % 2 == 1)
    def _():
        for hh in heads:
            s1_ref[hh] = scores(hh, qi)
            consume(hh, s0_ref, qi - 1, masked=False)
        for hh in heads:
            consume(hh, s1_ref, qi, masked=True)

    @pl.when(qi % 2 == 0)
    def _():
        for hh in heads:
            consume(hh, s0_ref, qi, masked=True)

    out_t = jnp.concatenate([acc_ref[hh, :dh] / acc_ref[hh, dh:dh + 1] for hh in heads], axis=0)
    o_ref[0] = out_t.T.astype(o_ref.dtype)


def _fox_attn_shifted_kernel(qt_ref, ka_ref, vt_ref, o_ref, p0_ref, p1_ref, acc_ref,
                             *, tq, tk, dh):
    qi = pl.program_id(2)
    hp = qt_ref.shape[1]
    bufs = (p0_ref, p1_ref)
    sum_rows = jnp.ones((FOX_SUM_ROWS, tk), BF16)

    def probs(hh, kv, masked):
        k0 = pl.multiple_of(kv * tk, tk)
        p = jnp.exp2(_dot(ka_ref[0, hh, pl.ds(k0, tk), :], qt_ref[0, hh]))
        if masked:
            visible = (lax.broadcasted_iota(jnp.int32, (tk, tq), 0)
                       <= lax.broadcasted_iota(jnp.int32, (tk, tq), 1))
            p = jnp.where(visible, p, 0.0)
        return p.astype(BF16)

    def fold(hh, kv, p_ref):
        k0 = pl.multiple_of(kv * tk, tk)
        v_sum = jnp.concatenate([vt_ref[0, hh, :, pl.ds(k0, tk)], sum_rows], axis=0)
        acc_ref[hh] += _dot(v_sum, p_ref[...])

    def section(blocks, masked, following):
        units = [(kv, hh) for kv in blocks for hh in range(hp)]
        for n, (kv, hh) in enumerate(units):
            nxt = units[n + 1] if n + 1 < len(units) else (following, 0)
            if nxt[0] is not None:
                bufs[(n + 1) % 2][...] = probs(nxt[1], nxt[0], masked and n + 1 < len(units))
            fold(hh, kv, bufs[n % 2])

    acc_ref[...] = jnp.zeros(acc_ref.shape, F32)
    p0_ref[...] = probs(0, qi, masked=True)
    section([qi], masked=True, following=0)

    def two_blocks(i, carry):
        section([2 * i, 2 * i + 1], masked=False, following=2 * i + 2)
        return carry

    lax.fori_loop(0, qi // 2, two_blocks, 0)

    @pl.when(qi % 2 == 1)
    def _():
        section([qi - 1], masked=False, following=None)

    out_t = jnp.concatenate([acc_ref[hh, :dh] / acc_ref[hh, dh:dh + 1] for hh in range(hp)],
                            axis=0)
    o_ref[0] = out_t.T.astype(o_ref.dtype)


def _fox_attn(qt, ka, vt, shifted):
    bsz, nh, _, seq = qt.shape
    dh = FOX_DH
    tq, tk = min(FOX_TQ, seq), min(FOX_TK, seq)
    hp = FOX_HEADS_PER_STEP["shifted" if shifted else "online"]
    assert tq == tk and (hp * dh) % LANES == 0
    acc = pltpu.VMEM((hp, dh + FOX_SUM_ROWS, tq), F32)
    if shifted:
        kern = _fox_attn_shifted_kernel
        scratch = [pltpu.VMEM((tk, tq), BF16), pltpu.VMEM((tk, tq), BF16), acc]
    else:
        kern = _fox_attn_kernel
        scratch = [pltpu.VMEM((hp, tk, tq), F32), pltpu.VMEM((hp, tk, tq), F32),
                   pltpu.VMEM((hp, 1, tq), F32), acc]
    return pl.pallas_call(
        functools.partial(kern, tq=tq, tk=tk, dh=dh),
        grid=(bsz, nh // hp, seq // tq),
        in_specs=[
            pl.BlockSpec((1, hp, LANES, tq), lambda b, h, s: (b, h, 0, s)),
            pl.BlockSpec((1, hp, seq, LANES), lambda b, h, s: (b, h, 0, 0)),
            pl.BlockSpec((1, hp, dh, seq), lambda b, h, s: (b, h, 0, 0)),
        ],
        out_specs=pl.BlockSpec((1, tq, hp * dh), lambda b, h, s: (b, s, h)),
        out_shape=jax.ShapeDtypeStruct((bsz, seq, nh * dh), BF16),
        scratch_shapes=scratch,
        name="fox_attn_shifted" if shifted else "fox_attn",
    )(qt, ka, vt)


def _out_kernel(o_ref, z_ref, w_ref, g_ref, gate_ref, x_ref, y_ref):
    y_ref[0] = _layer_tail(x_ref[0], slice(None), o_ref, z_ref, w_ref, g_ref, gate_ref)


def _out_proj(x, tail):
    o, z, w_out, g_post, mod = tail
    bsz, seq, d = x.shape
    width = o.shape[-1]
    tm = min(ROW_TILE * ROW_TILES_PER_STEP, seq)
    return pl.pallas_call(
        _out_kernel,
        grid=(bsz, seq // tm),
        in_specs=[_row_spec(tm, width), _row_spec(tm, width), _full((width, d)), _full((1, d)),
                  _mod_spec(d, 2), _row_spec(tm, d)],
        out_specs=_row_spec(tm, d),
        out_shape=jax.ShapeDtypeStruct((bsz, seq, d), F32),
        name="out_proj",
    )(o, z, w_out.astype(BF16), g_post.reshape(1, d), mod, x)


def _gla_layer(x, g_pre, mod, tail, w_in, w_a2, b_a, g_head):
    q, k, v, z, gk, *x_new = _gla_in(x, g_pre, mod, tail, w_in, w_a2, b_a)
    return (x_new or [x])[0], _gla_scan(q, k, v, gk, g_head), z


def _sgu_layer(x, g_pre, mod, tail, w_in, ln_g, ln_b, w_s, b_s):
    u, v, z, *x_new = _sgu_in(x, g_pre, mod, tail, w_in, ln_g, ln_b)
    return (x_new or [x])[0], _sgu_mix(u, v, w_s, b_s), z


def _fox_layer(x, g_pre, mod, tail, w_in, b_f, g_q, g_k):
    bound = FOX_DH * (FOX_DH ** -0.5 * LOG2E) * jnp.max(jnp.abs(g_q)) * jnp.max(jnp.abs(g_k))
    shift = 1.02 * bound
    qt, ka, vt, z, *x_new = _fox_in(x, g_pre, mod, tail, w_in, b_f, g_q, g_k, shift)
    o = lax.cond(shift <= FOX_MAX_SHIFT,
                 functools.partial(_fox_attn, shifted=True),
                 functools.partial(_fox_attn, shifted=False), qt, ka, vt)
    return (x_new or [x])[0], o, z


def kernel(x, c, norm_pre_g, norm_post_g, w_mod, b_mod, gla_w_in, gla_w_a2, gla_b_a, gla_g_head, gla_w_out, sgu_w_in, sgu_ln_g, sgu_ln_b, sgu_w_s, sgu_b_s, sgu_w_out, fox_w_in, fox_b_f, fox_g_q, fox_g_k, fox_w_out):
    depth = w_mod.shape[0]
    bsz = x.shape[0]
    mod_all = _modulation(c, w_mod, b_mod)
    tail = None
    for i in range(depth):
        mod = mod_all[i].reshape(bsz, 1, -1)
        kind, j = i % N_MIXERS, i // N_MIXERS
        if kind == 0:
            x, o, z = _gla_layer(x, norm_pre_g[i], mod, tail, gla_w_in[j], gla_w_a2[j],
                                 gla_b_a[j], gla_g_head[j])
            w_out = gla_w_out[j]
        elif kind == 1:
            x, o, z = _sgu_layer(x, norm_pre_g[i], mod, tail, sgu_w_in[j], sgu_ln_g[j],
                                 sgu_ln_b[j], sgu_w_s[j], sgu_b_s[j])
            w_out = sgu_w_out[j]
        else:
            x, o, z = _fox_layer(x, norm_pre_g[i], mod, tail, fox_w_in[j], fox_b_f[j],
                                 fox_g_q[j], fox_g_k[j])
            w_out = fox_w_out[j]
        tail = (o, z, w_out, norm_post_g[i], mod)
    return _out_proj(x, tail)
```

```python
import functools
import math

import numpy as np
import jax
import jax.numpy as jnp
from jax import lax
from jax.experimental import pallas as pl
from jax.experimental.pallas import tpu as pltpu

EPS = 1e-6
N_MIXERS = 3

LANES = 128
ROW_TILE = 512
ROW_TILES_PER_STEP = 2
COL_CHUNK = 512

GLA_HEADS = 4
GLA_RANK = 16
GLA_NORMALIZER = 16.0
GLA_CHUNK = 64
GLA_FAST_CHUNK = 256
GLA_DECAY_GUARD = 70.0
GLA_BLOCK = 1024

SGU_GROUPS = 4
SGU_CHUNK = 128

FOX_HEADS = 16
FOX_DH = 64
FOX_TQ = 512
FOX_TK = 512
FOX_HEADS_PER_STEP = {"shifted": 8, "online": 4}
FOX_SUM_ROWS = 16
FOX_TERMS = 3
FOX_MAX_SHIFT = 50.0
LOG2E = math.log2(math.e)

BF16 = jnp.bfloat16
F32 = jnp.float32


def _dot(a, b):
    return jnp.dot(a, b, preferred_element_type=F32)


def _dot_nt(a, b):
    return lax.dot_general(a, b, (((1,), (1,)), ((), ())), preferred_element_type=F32)


def _dot_tn(a, b):
    return lax.dot_general(a, b, (((0,), (0,)), ((), ())), preferred_element_type=F32)


def _split2(x):
    hi = x.astype(BF16)
    lo = (x - hi.astype(F32)).astype(BF16)
    return hi, lo


def _split3(x):
    hi = x.astype(BF16)
    r = x - hi.astype(F32)
    mid = r.astype(BF16)
    lo = (r - mid.astype(F32)).astype(BF16)
    return hi, mid, lo


def _log_sigmoid(x):
    return jnp.minimum(x, 0.0) - jnp.log1p(jnp.exp(-jnp.abs(x)))


def _rms(x):
    return x * lax.rsqrt(jnp.mean(x * x, axis=-1, keepdims=True) + EPS)


def _mod_kernel(c_ref, w_ref, b_ref, o_ref):
    cond = jax.nn.silu(c_ref[...])
    o_ref[0] = _dot(cond, w_ref[0]) + b_ref[0]


def _modulation(c, w_mod, b_mod):
    depth, d, d3 = w_mod.shape
    bsz = c.shape[0]
    nblk = d3 // d
    return pl.pallas_call(
        _mod_kernel,
        grid=(depth, nblk),
        in_specs=[
            pl.BlockSpec((bsz, d), lambda i, j: (0, 0)),
            pl.BlockSpec((1, d, d), lambda i, j: (i, 0, j)),
            pl.BlockSpec((1, 1, d), lambda i, j: (i, 0, j)),
        ],
        out_specs=pl.BlockSpec((1, bsz, d), lambda i, j: (i, 0, j)),
        out_shape=jax.ShapeDtypeStruct((depth, bsz, d3), F32),
        name="adaln_mod",
    )(c, w_mod, b_mod.reshape(depth, 1, d3))


N_HEAD_REFS = 4
N_TAIL_REFS = 5


def _layer_tail(x, rows, o_ref, z_ref, w_ref, g_ref, gate_ref):
    gated = o_ref[0, rows, :].astype(F32) * jax.nn.silu(z_ref[0, rows, :].astype(F32))
    y = _dot(gated.astype(BF16), w_ref[...])
    return x + gate_ref[0] * (_rms(y) * g_ref[...])


def _kernel_refs(refs, has_tail, n_scratch=0):
    rest = list(refs[N_HEAD_REFS:])
    if has_tail:
        del rest[len(rest) - n_scratch - 1]
        rest = rest[N_TAIL_REFS:]
    return rest


def _row_tiles(refs):
    rows = refs[0].shape[1]
    return [slice(r, r + ROW_TILE) for r in range(0, rows, min(ROW_TILE, rows))]


def _layer_input(refs, has_tail, rows, n_scratch=0):
    x_ref, g_ref, shift_ref, scale_ref = refs[:N_HEAD_REFS]
    x = x_ref[0, rows, :]
    if has_tail:
        x = _layer_tail(x, rows, *refs[N_HEAD_REFS:N_HEAD_REFS + N_TAIL_REFS])
        refs[len(refs) - n_scratch - 1][0, rows, :] = x
    h = _rms(x) * g_ref[...]
    h = h * (1.0 + scale_ref[0]) + shift_ref[0]
    return h.astype(BF16)


def _full(shape):
    nd = len(shape)
    return pl.BlockSpec(shape, lambda b, s: (0,) * nd, pipeline_mode=pl.Buffered(1))


def _row_spec(tm, width, col=0):
    return pl.BlockSpec((1, tm, width), lambda b, s: (b, s, col))


def _mod_spec(d, part):
    return pl.BlockSpec((1, 1, d), lambda b, s: (b, 0, part))


def _in_proj_call(kern, x, g, mod, tail, consts, out_specs, out_shape, name, tm, scratch=()):
    bsz, seq, d = x.shape
    operands = [x, g.reshape(1, d), mod, mod]
    in_specs = [_row_spec(tm, d), _full((1, d)), _mod_spec(d, 0), _mod_spec(d, 1)]
    if tail is not None:
        o, z, w_out, g_post, mod_prev = tail
        width = o.shape[-1]
        operands += [o, z, w_out.astype(BF16), g_post.reshape(1, d), mod_prev]
        in_specs += [_row_spec(tm, width), _row_spec(tm, width), _full((width, d)),
                     _full((1, d)), _mod_spec(d, 2)]
        out_specs = list(out_specs) + [_row_spec(tm, d)]
        out_shape = list(out_shape) + [jax.ShapeDtypeStruct((bsz, seq, d), F32)]
    operands += list(consts)
    in_specs += [_full(c.shape) for c in consts]
    return pl.pallas_call(
        functools.partial(kern, has_tail=tail is not None),
        grid=(bsz, seq // tm),
        in_specs=in_specs, out_specs=out_specs, out_shape=out_shape,
        scratch_shapes=list(scratch),
        compiler_params=pltpu.CompilerParams(dimension_semantics=("arbitrary", "arbitrary")),
        name=name,
    )(*operands)


def _gla_in_kernel(*refs, has_tail, kd, vd, q_scale):
    w_ref, wa2_ref, ba_ref, q_ref, k_ref, v_ref, z_ref, gk_ref = _kernel_refs(refs, has_tail)
    n_main = 2 * kd + 2 * vd
    tiles = _row_tiles(refs)
    inputs = [_layer_input(refs, has_tail, rows) for rows in tiles]
    for rows, h in zip(tiles, inputs):
        a_low = _dot(h, w_ref[:, n_main:n_main + LANES])
        logit = _dot(a_low.astype(BF16), wa2_ref[...]) + ba_ref[...]
        gk_ref[0, rows, :] = _log_sigmoid(logit) * (1.0 / GLA_NORMALIZER)
        col = 0
        for out_ref, width, mul in ((q_ref, kd, q_scale), (k_ref, kd, None),
                                    (v_ref, vd, None), (z_ref, vd, None)):
            for c0 in range(0, width, COL_CHUNK):
                acc = _dot(h, w_ref[:, col + c0:col + c0 + COL_CHUNK])
                if mul is not None:
                    acc = acc * mul
                out_ref[0, rows, c0:c0 + COL_CHUNK] = acc.astype(out_ref.dtype)
            col += width


def _gla_in(x, g, mod, tail, w_in, w_a2, b_a):
    bsz, seq, d = x.shape
    kd = w_a2.shape[1]
    vd = (w_in.shape[1] - 2 * kd - GLA_RANK) // 2
    tm = min(ROW_TILE * ROW_TILES_PER_STEP, seq)
    w = jnp.pad(w_in, ((0, 0), (0, LANES - GLA_RANK))).astype(BF16)
    wa2 = jnp.pad(w_a2, ((0, LANES - GLA_RANK), (0, 0))).astype(BF16)
    dk = kd // GLA_HEADS
    kern = functools.partial(_gla_in_kernel, kd=kd, vd=vd, q_scale=dk ** -0.5)
    act = lambda width, dtype: jax.ShapeDtypeStruct((bsz, seq, width), dtype)
    return _in_proj_call(
        kern, x, g, mod, tail, [w, wa2, b_a.reshape(1, kd)],
        out_specs=[_row_spec(tm, kd), _row_spec(tm, kd), _row_spec(tm, vd),
                   _row_spec(tm, vd), _row_spec(tm, kd)],
        out_shape=[act(kd, BF16), act(kd, BF16), act(vd, BF16), act(vd, BF16), act(kd, F32)],
        name="gla_in_proj", tm=tm)


def _gla_tables(c):
    tri = np.tril(np.ones((c, c), np.float32))
    rows = [tri, 1.0 - tri]
    n_levels = int(math.log2(c))
    level = np.full((c, c), -1, np.int32)
    idx = np.arange(c)
    for l in range(n_levels):
        blk = c >> l
        half = blk // 2
        mid = (idx // blk) * blk + half - 1
        rows.append(tri - tri[mid])
        same = (idx[:, None] // blk) == (idx[None, :] // blk)
        upper = (idx[:, None] % blk) >= half
        lower = (idx[None, :] % blk) < half
        level[same & upper & lower] = l
    level[idx, idx] = n_levels
    return np.concatenate(rows, axis=0), level, n_levels


def _gla_scan_kernel(q_ref, k_ref, v_ref, gk_ref, wf_ref, ws_ref, lvl_ref, gh_ref, o_ref, st_ref,
                     *, fast_chunk, safe_chunk, n_levels, dk, dv):
    tb, kd = gk_ref.shape[1], gk_ref.shape[2]

    @pl.when(pl.program_id(1) == 0)
    def _():
        st_ref[...] = jnp.zeros_like(st_ref)

    def head_chunk(hd, rows, chunk, safe):
        kc = slice(hd * dk, (hd + 1) * dk)
        vc = slice(hd * dv, (hd + 1) * dv)
        w = ws_ref[...] if safe else wf_ref[...]
        r = _dot(w, jnp.concatenate(_split2(gk_ref[0, rows, kc]), axis=0))
        b = r[0:chunk]
        rev = r[chunk:2 * chunk]
        qf = q_ref[0, rows, kc].astype(F32)
        kf = k_ref[0, rows, kc].astype(F32)
        v = v_ref[0, rows, vc]
        st = st_ref[hd]
        q_dec = (qf * jnp.exp(b)).astype(BF16)

        if safe:
            lvl = lvl_ref[...]
            att = jnp.zeros((chunk, chunk), F32)
            for l in range(n_levels + 1):
                if l < n_levels:
                    e = jnp.exp(-jnp.abs(r[(2 + l) * chunk:(3 + l) * chunk]))
                    ql, kl = (qf * e).astype(BF16), (kf * e).astype(BF16)
                else:
                    ql, kl = qf.astype(BF16), kf.astype(BF16)
                att = jnp.where(lvl == l, _dot_nt(ql, kl), att)
        else:
            causal = (lax.broadcasted_iota(jnp.int32, (chunk, chunk), 0)
                      >= lax.broadcasted_iota(jnp.int32, (chunk, chunk), 1))
            att = jnp.where(causal, _dot_nt(q_dec, (kf * jnp.exp(-b)).astype(BF16)), 0.0)

        o = _dot_nt(q_dec, st.astype(BF16)) + _dot(att.astype(BF16), v)
        o_ref[0, rows, vc] = (_rms(o) * gh_ref[:, vc]).astype(o_ref.dtype)
        k_rev = (kf * jnp.exp(rev)).astype(BF16)
        st_ref[hd] = st * jnp.exp(b[chunk - 1:chunk, :]) + _dot_tn(v, k_rev)

    def run(chunk, safe):
        def body(ci, carry):
            rows = pl.ds(pl.multiple_of(ci * chunk, chunk), chunk)
            for hd in range(GLA_HEADS):
                head_chunk(hd, rows, chunk, safe)
            return carry
        lax.fori_loop(0, tb // chunk, body, 0, unroll=2)

    chunk_sums = jnp.sum(gk_ref[0].reshape(tb // fast_chunk, fast_chunk, kd), axis=1)
    fast_ok = jnp.min(chunk_sums) >= -GLA_DECAY_GUARD

    @pl.when(fast_ok)
    def _():
        run(fast_chunk, safe=False)

    @pl.when(jnp.logical_not(fast_ok))
    def _():
        run(safe_chunk, safe=True)


def _gla_scan(q, k, v, gk, g_head):
    bsz, seq, kd = q.shape
    vd = v.shape[-1]
    dk, dv = kd // GLA_HEADS, vd // GLA_HEADS
    tb = min(GLA_BLOCK, seq)
    ws_np, lvl_np, n_levels = _gla_tables(GLA_CHUNK)
    wf_np = _gla_tables(GLA_FAST_CHUNK)[0][:2 * GLA_FAST_CHUNK]
    stack2 = lambda t: jnp.asarray(np.concatenate([t, t], axis=1), BF16)
    wf, ws = stack2(wf_np), stack2(ws_np)
    kern = functools.partial(_gla_scan_kernel, fast_chunk=GLA_FAST_CHUNK, safe_chunk=GLA_CHUNK,
                             n_levels=n_levels, dk=dk, dv=dv)
    return pl.pallas_call(
        kern,
        grid=(bsz, seq // tb),
        in_specs=[_row_spec(tb, kd), _row_spec(tb, kd), _row_spec(tb, vd), _row_spec(tb, kd),
                  _full(wf.shape), _full(ws.shape), _full(lvl_np.shape), _full((1, vd))],
        out_specs=_row_spec(tb, vd),
        out_shape=jax.ShapeDtypeStruct((bsz, seq, vd), BF16),
        scratch_shapes=[pltpu.VMEM((GLA_HEADS, dv, dk), F32)],
        compiler_params=pltpu.CompilerParams(dimension_semantics=("arbitrary", "arbitrary")),
        name="gla_scan",
    )(q, k, v, gk, wf, ws, jnp.asarray(lvl_np), g_head.reshape(1, vd))


def _sgu_in_kernel(*refs, has_tail, width):
    (w_ref, lng_ref, lnb_ref, ws_ref, bs_ref, o_ref, z_ref, vtmp_ref, vln_ref) = _kernel_refs(
        refs, has_tail, 2)
    gdim = width // SGU_GROUPS
    tiles = _row_tiles(refs)
    inputs = [_layer_input(refs, has_tail, rows, n_scratch=2) for rows in tiles]
    for rows, h in zip(tiles, inputs):
        for c0 in range(0, width, COL_CHUNK):
            vtmp_ref[rows, c0:c0 + COL_CHUNK] = jax.nn.gelu(
                _dot(h, w_ref[:, width + c0:width + c0 + COL_CHUNK]))
        v = vtmp_ref[rows, :]
        mu = jnp.mean(v, axis=-1, keepdims=True)
        vc = v - mu
        var = jnp.mean(vc * vc, axis=-1, keepdims=True)
        vln_ref[rows, :] = (vc * lax.rsqrt(var + EPS) * lng_ref[...] + lnb_ref[...]).astype(BF16)
        for g in range(SGU_GROUPS):
            cols = slice(g * gdim, (g + 1) * gdim)
            bias = jnp.concatenate([bs_ref[g]] * (gdim // LANES), axis=1)
            for r0 in range(rows.start, rows.stop, SGU_CHUNK):
                crows = slice(r0, r0 + SGU_CHUNK)
                vtmp_ref[crows, cols] = _dot(ws_ref[g], vln_ref[crows, cols]) + bias
        for c0 in range(0, width, COL_CHUNK):
            cols = slice(c0, c0 + COL_CHUNK)
            o_ref[0, rows, cols] = (jax.nn.gelu(_dot(h, w_ref[:, cols])) * vtmp_ref[rows, cols]
                                    ).astype(o_ref.dtype)
            z_ref[0, rows, cols] = _dot(h, w_ref[:, 2 * width + c0:2 * width + c0 + COL_CHUNK]
                                        ).astype(z_ref.dtype)


def _sgu_in(x, g, mod, tail, w_in, ln_g, ln_b, w_s, b_s):
    bsz, seq, d = x.shape
    width = w_in.shape[1] // 3
    tm = min(ROW_TILE * ROW_TILES_PER_STEP, seq)
    causal = jnp.tril(jnp.ones((SGU_CHUNK, SGU_CHUNK), bool))
    w_causal = jnp.where(causal[None], w_s, 0.0).astype(BF16)
    bias = jnp.broadcast_to(b_s[:, :, None], (SGU_GROUPS, SGU_CHUNK, LANES))
    kern = functools.partial(_sgu_in_kernel, width=width)
    act = jax.ShapeDtypeStruct((bsz, seq, width), BF16)
    return _in_proj_call(
        kern, x, g, mod, tail,
        [w_in.astype(BF16), ln_g.reshape(1, width), ln_b.reshape(1, width), w_causal, bias],
        out_specs=[_row_spec(tm, width)] * 2, out_shape=[act, act],
        name="sgu_in_proj", tm=tm,
        scratch=[pltpu.VMEM((tm, width), F32), pltpu.VMEM((tm, width), BF16)])


def _fox_tables(n_heads, dh):
    width = n_heads * dh
    n_terms = FOX_TERMS
    assert n_terms * n_heads <= LANES and 2 * n_terms + 1 <= dh
    pq = np.zeros((LANES, width), np.float32)
    pk = np.zeros((LANES, width), np.float32)
    cq = np.zeros((1, width), np.float32)
    ck = np.zeros((1, width), np.float32)
    shift_slot = np.zeros((1, width), np.float32)
    for h in range(n_heads):
        base = (h // 2) * 2 * dh + (dh if h % 2 == 0 else 0)
        for p in range(n_terms):
            pq[p * n_heads + h, base + p] = 1.0
            cq[0, base + n_terms + p] = 1.0
            ck[0, base + p] = 1.0
            pk[p * n_heads + h, base + n_terms + p] = -1.0
        cq[0, base + 2 * n_terms] = 1.0
        shift_slot[0, base + 2 * n_terms] = 1.0
    return pq, pk, cq, ck, shift_slot


def _fox_in_kernel(*refs, has_tail, ad, dh, n_heads):
    (w_ref, gq_ref, gk_ref, bf_ref, tri_ref, pq_ref, pk_ref, cq_ref, ck_ref,
     qt_ref, ka_ref, vt_ref, z_ref, carry_ref) = _kernel_refs(refs, has_tail, 1)
    h = _layer_input(refs, has_tail, slice(None), n_scratch=1)

    @pl.when(pl.program_id(1) == 0)
    def _():
        carry_ref[...] = jnp.zeros_like(carry_ref)

    tm = h.shape[0]
    pair = 2 * dh
    lane = lax.broadcasted_iota(jnp.int32, (tm, pair), 1)
    low = lane < dh

    def half_rms(x):
        sq = x * x
        s_lo = jnp.sum(jnp.where(low, sq, 0.0), axis=-1, keepdims=True)
        s_hi = jnp.sum(jnp.where(low, 0.0, sq), axis=-1, keepdims=True)
        return x * lax.rsqrt(jnp.where(low, s_lo, s_hi) * (1.0 / dh) + EPS)

    f_logit = _dot(h, w_ref[:, 4 * ad:4 * ad + LANES])
    tri = tri_ref[...]
    lf_hi, lf_mid, lf_lo = _split3(_log_sigmoid(f_logit + bf_ref[...]))
    f_cum = _dot(tri, lf_hi) + _dot(tri, lf_mid) + _dot(tri, lf_lo) + carry_ref[...]
    carry_ref[...] = f_cum[tm - 1:tm, :]
    term_list = _split3(f_cum * LOG2E)
    terms = term_list[-1]
    for p in range(FOX_TERMS - 2, -1, -1):
        terms = jnp.where(lane < (p + 1) * n_heads, term_list[p], terms)
    aug_q = _dot(terms, pq_ref[...]) + cq_ref[...]
    aug_k = _dot(terms, pk_ref[...]) + ck_ref[...]

    for c0 in range(0, ad, COL_CHUNK):
        acc_q = _dot(h, w_ref[:, c0:c0 + COL_CHUNK])
        acc_k = _dot(h, w_ref[:, ad + c0:ad + c0 + COL_CHUNK])
        acc_v = _dot(h, w_ref[:, 2 * ad + c0:2 * ad + c0 + COL_CHUNK])
        z_ref[0, :, c0:c0 + COL_CHUNK] = _dot(
            h, w_ref[:, 3 * ad + c0:3 * ad + c0 + COL_CHUNK]).astype(z_ref.dtype)
        for g0 in range(0, COL_CHUNK, pair):
            cols = slice(g0, g0 + pair)
            gcols = slice(c0 + g0, c0 + g0 + pair)
            head = (c0 + g0) // dh
            qn = half_rms(acc_q[:, cols]) * gq_ref[...]
            kn = half_rms(acc_k[:, cols]) * gk_ref[...]
            aq, ak = aug_q[:, gcols], aug_k[:, gcols]
            qt_ref[0, head] = jnp.where(low, qn, aq).T.astype(qt_ref.dtype)
            qt_ref[0, head + 1] = jnp.where(low, aq, qn).T.astype(qt_ref.dtype)
            ka_ref[0, head] = jnp.where(low, kn, ak).astype(ka_ref.dtype)
            ka_ref[0, head + 1] = jnp.where(low, ak, kn).astype(ka_ref.dtype)
            v_t = acc_v[:, cols].T.astype(vt_ref.dtype)
            vt_ref[0, head] = v_t[:dh]
            vt_ref[0, head + 1] = v_t[dh:]


def _fox_in(x, g, mod, tail, w_in, b_f, g_q, g_k, shift):
    bsz, seq, d = x.shape
    nh, dh = FOX_HEADS, FOX_DH
    ad = nh * dh
    tm = min(ROW_TILE, seq)
    pad = LANES - FOX_TERMS * nh
    w = jnp.concatenate([w_in[:, :4 * ad]] + [w_in[:, 4 * ad:]] * FOX_TERMS
                        + [jnp.zeros((d, pad), w_in.dtype)], axis=1).astype(BF16)
    bf = jnp.pad(jnp.tile(b_f, FOX_TERMS), (0, pad)).reshape(1, LANES)
    gq2 = jnp.tile(g_q * (dh ** -0.5 * LOG2E), 2).reshape(1, 2 * dh)
    gk2 = jnp.tile(g_k, 2).reshape(1, 2 * dh)
    tri = jnp.asarray(np.tril(np.ones((tm, tm), np.float32)), BF16)
    pq, pk, cq, ck, shift_slot = _fox_tables(nh, dh)
    ck = jnp.asarray(ck) - shift * jnp.asarray(shift_slot)
    kern = functools.partial(_fox_in_kernel, ad=ad, dh=dh, n_heads=nh)
    return _in_proj_call(
        kern, x, g, mod, tail,
        [w, gq2, gk2, bf, tri, jnp.asarray(pq, BF16), jnp.asarray(pk, BF16), jnp.asarray(cq), ck],
        out_specs=[
            pl.BlockSpec((1, nh, LANES, tm), lambda b, s: (b, 0, 0, s)),
            pl.BlockSpec((1, nh, tm, LANES), lambda b, s: (b, 0, s, 0)),
            pl.BlockSpec((1, nh, dh, tm), lambda b, s: (b, 0, 0, s)),
            _row_spec(tm, ad)],
        out_shape=[jax.ShapeDtypeStruct((bsz, nh, LANES, seq), BF16),
                   jax.ShapeDtypeStruct((bsz, nh, seq, LANES), BF16),
                   jax.ShapeDtypeStruct((bsz, nh, dh, seq), BF16),
                   jax.ShapeDtypeStruct((bsz, seq, ad), BF16)],
        name="fox_in_proj", tm=tm, scratch=[pltpu.VMEM((1, LANES), F32)])


def _fox_attn_kernel(qt_ref, ka_ref, vt_ref, o_ref, s0_ref, s1_ref, m_ref, acc_ref,
                     *, tq, tk, dh):
    qi = pl.program_id(2)
    hp = qt_ref.shape[1]
    heads = range(hp)
    sum_rows = jnp.ones((FOX_SUM_ROWS, tk), BF16)

    def scores(hh, kv):
        k0 = pl.multiple_of(kv * tk, tk)
        return _dot(ka_ref[0, hh, pl.ds(k0, tk), :], qt_ref[0, hh])

    def consume(hh, s_ref, kv, masked):
        k0 = pl.multiple_of(kv * tk, tk)
        if masked:
            visible = (lax.broadcasted_iota(jnp.int32, (tk, tq), 0)
                       <= lax.broadcasted_iota(jnp.int32, (tk, tq), 1))
            read = lambda: jnp.where(visible, s_ref[hh], -jnp.inf)
        else:
            read = lambda: s_ref[hh]
        m = m_ref[hh]
        m_new = jnp.maximum(m, jnp.max(read(), axis=0, keepdims=True))
        p = jnp.exp2(read() - m_new).astype(BF16)
        v_sum = jnp.concatenate([vt_ref[0, hh, :, pl.ds(k0, tk)], sum_rows], axis=0)
        acc_ref[hh] = acc_ref[hh] * jnp.exp2(m - m_new) + _dot(v_sum, p)
        m_ref[hh] = m_new

    m_ref[...] = jnp.full(m_ref.shape, -jnp.inf, F32)
    acc_ref[...] = jnp.zeros(acc_ref.shape, F32)
    for hh in heads:
        s0_ref[hh] = scores(hh, 0)

    def two_blocks(i, carry):
        j = 2 * i
        for hh in heads:
            s1_ref[hh] = scores(hh, j + 1)
            consume(hh, s0_ref, j, masked=False)
        for hh in heads:
            s0_ref[hh] = scores(hh, j + 2)
            consume(hh, s1_ref, j + 1, masked=False)
        return carry

    lax.fori_loop(0, qi // 2, two_blocks, 0)

    @pl.when(qi % 2 == 1)
    def _():
        for hh in heads:
            s1_ref[hh] = scores(hh, qi)
            consume(hh, s0_ref, qi - 1, masked=False)
        for hh in heads:
            consume(hh, s1_ref, qi, masked=True)

    @pl.when(qi % 2 == 0)
    def _():
        for hh in heads:
            consume(hh, s0_ref, qi, masked=True)

    out_t = jnp.concatenate([acc_ref[hh, :dh] / acc_ref[hh, dh:dh + 1] for hh in heads], axis=0)
    o_ref[0] = out_t.T.astype(o_ref.dtype)


def _fox_attn_shifted_kernel(qt_ref, ka_ref, vt_ref, o_ref, p0_ref, p1_ref, acc_ref,
                             *, tq, tk, dh):
    qi = pl.program_id(2)
    hp = qt_ref.shape[1]
    bufs = (p0_ref, p1_ref)
    sum_rows = jnp.ones((FOX_SUM_ROWS, tk), BF16)

    def probs(hh, kv, masked):
        k0 = pl.multiple_of(kv * tk, tk)
        p = jnp.exp2(_dot(ka_ref[0, hh, pl.ds(k0, tk), :], qt_ref[0, hh]))
        if masked:
            visible = (lax.broadcasted_iota(jnp.int32, (tk, tq), 0)
                       <= lax.broadcasted_iota(jnp.int32, (tk, tq), 1))
            p = jnp.where(visible, p, 0.0)
        return p.astype(BF16)

    def fold(hh, kv, p_ref):
        k0 = pl.multiple_of(kv * tk, tk)
        v_sum = jnp.concatenate([vt_ref[0, hh, :, pl.ds(k0, tk)], sum_rows], axis=0)
        acc_ref[hh] += _dot(v_sum, p_ref[...])

    def section(blocks, masked, following):
        units = [(kv, hh) for kv in blocks for hh in range(hp)]
        for n, (kv, hh) in enumerate(units):
            nxt = units[n + 1] if n + 1 < len(units) else (following, 0)
            if nxt[0] is not None:
                bufs[(n + 1) % 2][...] = probs(nxt[1], nxt[0], masked and n + 1 < len(units))
            fold(hh, kv, bufs[n % 2])

    acc_ref[...] = jnp.zeros(acc_ref.shape, F32)
    p0_ref[...] = probs(0, qi, masked=True)
    section([qi], masked=True, following=0)

    def two_blocks(i, carry):
        section([2 * i, 2 * i + 1], masked=False, following=2 * i + 2)
        return carry

    lax.fori_loop(0, qi // 2, two_blocks, 0)

    @pl.when(qi % 2 == 1)
    def _():
        section([qi - 1], masked=False, following=None)

    out_t = jnp.concatenate([acc_ref[hh, :dh] / acc_ref[hh, dh:dh + 1] for hh in range(hp)],
                            axis=0)
    o_ref[0] = out_t.T.astype(o_ref.dtype)


def _fox_attn(qt, ka, vt, shifted):
    bsz, nh, _, seq = qt.shape
    dh = FOX_DH
    tq, tk = min(FOX_TQ, seq), min(FOX_TK, seq)
    hp = FOX_HEADS_PER_STEP["shifted" if shifted else "online"]
    assert tq == tk and (hp * dh) % LANES == 0
    acc = pltpu.VMEM((hp, dh + FOX_SUM_ROWS, tq), F32)
    if shifted:
        kern = _fox_attn_shifted_kernel
        scratch = [pltpu.VMEM((tk, tq), BF16), pltpu.VMEM((tk, tq), BF16), acc]
    else:
        kern = _fox_attn_kernel
        scratch = [pltpu.VMEM((hp, tk, tq), F32), pltpu.VMEM((hp, tk, tq), F32),
                   pltpu.VMEM((hp, 1, tq), F32), acc]
    return pl.pallas_call(
        functools.partial(kern, tq=tq, tk=tk, dh=dh),
        grid=(bsz, nh // hp, seq // tq),
        in_specs=[
            pl.BlockSpec((1, hp, LANES, tq), lambda b, h, s: (b, h, 0, s)),
            pl.BlockSpec((1, hp, seq, LANES), lambda b, h, s: (b, h, 0, 0)),
            pl.BlockSpec((1, hp, dh, seq), lambda b, h, s: (b, h, 0, 0)),
        ],
        out_specs=pl.BlockSpec((1, tq, hp * dh), lambda b, h, s: (b, s, h)),
        out_shape=jax.ShapeDtypeStruct((bsz, seq, nh * dh), BF16),
        scratch_shapes=scratch,
        name="fox_attn_shifted" if shifted else "fox_attn",
    )(qt, ka, vt)


def _out_kernel(o_ref, z_ref, w_ref, g_ref, gate_ref, x_ref, y_ref):
    y_ref[0] = _layer_tail(x_ref[0], slice(None), o_ref, z_ref, w_ref, g_ref, gate_ref)


def _out_proj(x, tail):
    o, z, w_out, g_post, mod = tail
    bsz, seq, d = x.shape
    width = o.shape[-1]
    tm = min(ROW_TILE * ROW_TILES_PER_STEP, seq)
    return pl.pallas_call(
        _out_kernel,
        grid=(bsz, seq // tm),
        in_specs=[_row_spec(tm, width), _row_spec(tm, width), _full((width, d)), _full((1, d)),
                  _mod_spec(d, 2), _row_spec(tm, d)],
        out_specs=_row_spec(tm, d),
        out_shape=jax.ShapeDtypeStruct((bsz, seq, d), F32),
        name="out_proj",
    )(o, z, w_out.astype(BF16), g_post.reshape(1, d), mod, x)


def _gla_layer(x, g_pre, mod, tail, w_in, w_a2, b_a, g_head):
    q, k, v, z, gk, *x_new = _gla_in(x, g_pre, mod, tail, w_in, w_a2, b_a)
    return (x_new or [x])[0], _gla_scan(q, k, v, gk, g_head), z


def _sgu_layer(x, g_pre, mod, tail, w_in, ln_g, ln_b, w_s, b_s):
    o, z, *x_new = _sgu_in(x, g_pre, mod, tail, w_in, ln_g, ln_b, w_s, b_s)
    return (x_new or [x])[0], o, z


def _fox_layer(x, g_pre, mod, tail, w_in, b_f, g_q, g_k):
    bound = FOX_DH * (FOX_DH ** -0.5 * LOG2E) * jnp.max(jnp.abs(g_q)) * jnp.max(jnp.abs(g_k))
    shift = 1.02 * bound
    qt, ka, vt, z, *x_new = _fox_in(x, g_pre, mod, tail, w_in, b_f, g_q, g_k, shift)
    o = lax.cond(shift <= FOX_MAX_SHIFT,
                 functools.partial(_fox_attn, shifted=True),
                 functools.partial(_fox_attn, shifted=False), qt, ka, vt)
    return (x_new or [x])[0], o, z


def kernel(x, c, norm_pre_g, norm_post_g, w_mod, b_mod, gla_w_in, gla_w_a2, gla_b_a, gla_g_head, gla_w_out, sgu_w_in, sgu_ln_g, sgu_ln_b, sgu_w_s, sgu_b_s, sgu_w_out, fox_w_in, fox_b_f, fox_g_q, fox_g_k, fox_w_out):
    depth = w_mod.shape[0]
    bsz = x.shape[0]
    mod_all = _modulation(c, w_mod, b_mod)
    tail = None
    for i in range(depth):
        mod = mod_all[i].reshape(bsz, 1, -1)
        kind, j = i % N_MIXERS, i // N_MIXERS
        if kind == 0:
            x, o, z = _gla_layer(x, norm_pre_g[i], mod, tail, gla_w_in[j], gla_w_a2[j],
                                 gla_b_a[j], gla_g_head[j])
            w_out = gla_w_out[j]
        elif kind == 1:
            x, o, z = _sgu_layer(x, norm_pre_g[i], mod, tail, sgu_w_in[j], sgu_ln_g[j],
                                 sgu_ln_b[j], sgu_w_s[j], sgu_b_s[j])
            w_out = sgu_w_out[j]
        else:
            x, o, z = _fox_layer(x, norm_pre_g[i], mod, tail, fox_w_in[j], fox_b_f[j],
                                 fox_g_q[j], fox_g_k[j])
            w_out = fox_w_out[j]
        tail = (o, z, w_out, norm_post_g[i], mod)
    return _out_proj(x, tail)
```
